```python
import math
import jax, jax.numpy as jnp
from jax import lax
import numpy as np

D_MODEL = 1024
BATCH = 16
SEQ = 4096
DEPTH = 2
DEC_BATCH = 16
DEC_SEQ = 16
PAST_LEN = 4096

CHUNK = 64
N_MIXERS = 2
N_A_LAYERS = (DEPTH + 1) // 2
N_B_LAYERS = DEPTH // 2

A_HEADS = 16
A_HEAD_DIM = D_MODEL // A_HEADS
A_WIDTH = A_HEADS * A_HEAD_DIM
A_PAST_CHUNKS = 8
A_PAST = A_PAST_CHUNKS * CHUNK
A_BAND = A_PAST + CHUNK
REL_CLIP = 128

B_HEADS = 16
B_KV_HEADS = 4
B_HEAD_DIM = D_MODEL // B_HEADS
B_GROUP = B_HEADS // B_KV_HEADS
IDX_HEADS = 8
IDX_DIM = 64
TOPK_MAX = 256
B_QBLOCK = 64
B_Q = B_HEADS * B_HEAD_DIM
B_KV = B_KV_HEADS * B_HEAD_DIM
B_QI = IDX_HEADS * IDX_DIM
B_SPLITS = [B_Q, B_Q + B_KV, B_Q + 2 * B_KV, B_Q + 2 * B_KV + B_QI, B_Q + 2 * B_KV + B_QI + IDX_DIM]
B_IN = B_Q + 2 * B_KV + B_QI + IDX_DIM + IDX_HEADS

D_FF = 2816
ALPHA = (2.0 * DEPTH) ** 0.25
BETA = (8.0 * DEPTH) ** -0.25
LN_EPS = 1e-5

kernel_name = 'hybrid_streaming_encoder_step'


def layer_norm(x, g, b):
    xf = x.astype(jnp.float32)
    mu = jnp.mean(xf, axis=-1, keepdims=True)
    var = jnp.mean(jnp.square(xf - mu), axis=-1, keepdims=True)
    y = (xf - mu) * lax.rsqrt(var + LN_EPS) * g.astype(jnp.float32) + b.astype(jnp.float32)
    return y.astype(x.dtype)


def swiglu(x, wg, wu, wd):
    return (jax.nn.silu(x @ wg) * (x @ wu)) @ wd


def alibi_slopes(n):
    return 2.0 ** (-8.0 * jnp.arange(1, n + 1, dtype=jnp.float32) / n)


def _split_a(h):
    bsz, t, _ = h.shape
    q, k, v = jnp.split(h, 3, axis=-1)
    shp = (bsz, t, A_HEADS, A_HEAD_DIM)
    return q.reshape(shp), k.reshape(shp), v.reshape(shp)


def band_attention(q, k, v, q_pos, k_pos, k_valid, rel_bias):
    rel = jnp.clip(q_pos[:, None] - k_pos[None, :], -REL_CLIP, REL_CLIP) + REL_CLIP
    bias = rel_bias[:, rel].astype(jnp.float32)
    s = jnp.einsum('bqhd,bkhd->bhqk', q, k).astype(jnp.float32) * (A_HEAD_DIM ** -0.5) + bias
    s = jnp.where(k_valid[None, None, None, :], s, -jnp.inf)
    p = jax.nn.softmax(s, axis=-1).astype(v.dtype)
    return jnp.einsum('bhqk,bkhd->bqhd', p, v)


def mixer_a_prompt(x, w_in, rel_bias, w_out):
    bsz, s_len, _ = x.shape
    q, k, v = _split_a(x @ w_in)
    pad = ((0, 0), (A_PAST, 0), (0, 0), (0, 0))
    kp = jnp.pad(k, pad)
    vp = jnp.pad(v, pad)

    def one_chunk(c):
        start = c * CHUNK
        qc = lax.dynamic_slice_in_dim(q, start, CHUNK, axis=1)
        kc = lax.dynamic_slice_in_dim(kp, start, A_BAND, axis=1)
        vc = lax.dynamic_slice_in_dim(vp, start, A_BAND, axis=1)
        q_pos = start + jnp.arange(CHUNK)
        k_pos = start - A_PAST + jnp.arange(A_BAND)
        return band_attention(qc, kc, vc, q_pos, k_pos, k_pos >= 0, rel_bias)

    o = lax.map(one_chunk, jnp.arange(s_len // CHUNK))
    o = jnp.moveaxis(o, 0, 1).reshape(bsz, s_len, A_WIDTH)
    keep = min(A_PAST, s_len)
    return o @ w_out, k[:, s_len - keep:], v[:, s_len - keep:]


def mixer_a_sample(x, cache_k, cache_v, w_in, rel_bias, w_out):
    bsz, t, _ = x.shape
    n_cache = cache_k.shape[1]
    q, k, v = _split_a(x @ w_in)
    kk = jnp.concatenate([cache_k.astype(k.dtype), k], axis=1)
    vv = jnp.concatenate([cache_v.astype(v.dtype), v], axis=1)
    q_pos = PAST_LEN + jnp.arange(t)
    k_pos = PAST_LEN + jnp.arange(-n_cache, t)
    o = band_attention(q, kk, vv, q_pos, k_pos, k_pos >= 0, rel_bias)
    return o.reshape(bsz, t, A_WIDTH) @ w_out, k, v


def _split_b(h):
    bsz, t, _ = h.shape
    q, k, v, qi, ki, wi = jnp.split(h, B_SPLITS, axis=-1)
    return (q.reshape(bsz, t, B_HEADS, B_HEAD_DIM),
            k.reshape(bsz, t, B_KV_HEADS, B_HEAD_DIM),
            v.reshape(bsz, t, B_KV_HEADS, B_HEAD_DIM),
            qi.reshape(bsz, t, IDX_HEADS, IDX_DIM), ki, wi)


def sparse_attention(q, qi, wi, k, v, ki, q_pos, n_sel):
    bsz, nq = q.shape[:2]
    n_keys = k.shape[1]
    limit = (q_pos // CHUNK + 1) * CHUNK
    admissible = jnp.arange(n_keys)[None, :] < limit[:, None]
    dots = jnp.einsum('bqhd,bld->bqhl', qi.astype(jnp.float32), ki.astype(jnp.float32)) * (IDX_DIM ** -0.5)
    score = jnp.einsum('bqhl,bqh->bql', jax.nn.relu(dots), wi.astype(jnp.float32) * (IDX_HEADS ** -0.5))
    score = jnp.where(admissible[None], score, -jnp.inf)
    _, sel = lax.top_k(score, n_sel)
    sel_valid = sel < limit[None, :, None]
    k_sel = jax.vmap(lambda a, i: a[i])(k, sel)
    v_sel = jax.vmap(lambda a, i: a[i])(v, sel)
    qg = q.reshape(bsz, nq, B_KV_HEADS, B_GROUP, B_HEAD_DIM)
    s = jnp.einsum('bqkgd,bqnkd->bqkgn', qg, k_sel).astype(jnp.float32) * (B_HEAD_DIM ** -0.5)
    dist = jnp.abs(q_pos[None, :, None] - sel).astype(jnp.float32)
    slopes = alibi_slopes(B_HEADS).reshape(B_KV_HEADS, B_GROUP)
    s = s - slopes[None, None, :, :, None] * dist[:, :, None, None, :]
    s = jnp.where(sel_valid[:, :, None, None, :], s, -jnp.inf)
    p = jax.nn.softmax(s, axis=-1).astype(v.dtype)
    o = jnp.einsum('bqkgn,bqnkd->bqkgd', p, v_sel)
    return o.reshape(bsz, nq, B_Q)


def mixer_b_prompt(x, w_in, w_out):
    bsz, s_len, _ = x.shape
    q, k, v, qi, ki, wi = _split_b(x @ w_in)
    n_sel = min(TOPK_MAX, s_len // 4)

    def one_block(i):
        start = i * B_QBLOCK
        sl = lambda a: lax.dynamic_slice_in_dim(a, start, B_QBLOCK, axis=1)
        return sparse_attention(sl(q), sl(qi), sl(wi), k, v, ki, start + jnp.arange(B_QBLOCK), n_sel)

    o = lax.map(one_block, jnp.arange(s_len // B_QBLOCK))
    o = jnp.moveaxis(o, 0, 1).reshape(bsz, s_len, B_Q)
    return o @ w_out, k, v, ki


def mixer_b_sample(x, cache_k, cache_v, cache_idx, w_in, w_out):
    bsz, t, _ = x.shape
    past = cache_k.shape[1]
    q, k, v, qi, ki, wi = _split_b(x @ w_in)
    kk = jnp.concatenate([cache_k.astype(k.dtype), k], axis=1)
    vv = jnp.concatenate([cache_v.astype(v.dtype), v], axis=1)
    kki = jnp.concatenate([cache_idx.astype(ki.dtype), ki], axis=1)
    n_sel = min(TOPK_MAX, (past + t) // 4)
    o = sparse_attention(q, qi, wi, kk, vv, kki, past + jnp.arange(t), n_sel)
    return o @ w_out, k, v, ki


def half_ffn_block(x, wg, wu, wd, g, b):
    return layer_norm(ALPHA * x + 0.5 * swiglu(x, wg, wu, wd), g, b)


def setup_inputs(seed: int = 0) -> dict:
    key = jax.random.key(seed)
    ks = jax.random.split(key, 20)
    f32 = jnp.float32
    n_a_cache = min(A_PAST, PAST_LEN)

    def nrm(k, shape, scale):
        return jax.random.normal(k, shape, f32) * scale

    a_w_in = nrm(ks[8], (N_A_LAYERS, D_MODEL, 3 * A_WIDTH), D_MODEL ** -0.5)
    a_w_in = a_w_in.at[..., 2 * A_WIDTH:].multiply(BETA)
    b_w_in = nrm(ks[11], (N_B_LAYERS, D_MODEL, B_IN), D_MODEL ** -0.5)
    b_w_in = b_w_in.at[..., B_Q + B_KV:B_Q + 2 * B_KV].multiply(BETA)
    return {
        'x_prompt': nrm(ks[0], (BATCH, SEQ, D_MODEL), 1.0),
        'x_sample': nrm(ks[1], (DEC_BATCH, DEC_SEQ, D_MODEL), 1.0),
        'cache_a_k': nrm(ks[2], (N_A_LAYERS, DEC_BATCH, n_a_cache, A_HEADS, A_HEAD_DIM), 1.0),
        'cache_a_v': nrm(ks[3], (N_A_LAYERS, DEC_BATCH, n_a_cache, A_HEADS, A_HEAD_DIM), BETA),
        'cache_b_k': nrm(ks[4], (N_B_LAYERS, DEC_BATCH, PAST_LEN, B_KV_HEADS, B_HEAD_DIM), 1.0),
        'cache_b_v': nrm(ks[5], (N_B_LAYERS, DEC_BATCH, PAST_LEN, B_KV_HEADS, B_HEAD_DIM), BETA),
        'cache_b_idx': nrm(ks[6], (N_B_LAYERS, DEC_BATCH, PAST_LEN, IDX_DIM), 1.0),
        'ln_g': 1.0 + nrm(ks[7], (DEPTH, 3, D_MODEL), 0.02),
        'ln_b': nrm(ks[13], (DEPTH, 3, D_MODEL), 0.02),
        'ffn_w_gate': nrm(ks[14], (DEPTH, 2, D_MODEL, D_FF), D_MODEL ** -0.5),
        'ffn_w_up': nrm(ks[15], (DEPTH, 2, D_MODEL, D_FF), D_MODEL ** -0.5),
        'ffn_w_down': nrm(ks[16], (DEPTH, 2, D_FF, D_MODEL), BETA * D_FF ** -0.5),
        'a_w_in': a_w_in,
        'a_rel_bias': nrm(ks[9], (N_A_LAYERS, A_HEADS, 2 * REL_CLIP + 1), 0.5),
        'a_w_out': nrm(ks[10], (N_A_LAYERS, A_WIDTH, D_MODEL), BETA * A_WIDTH ** -0.5),
        'b_w_in': b_w_in,
        'b_w_out': nrm(ks[12], (N_B_LAYERS, B_Q, D_MODEL), BETA * B_Q ** -0.5),
    }


def reference(x_prompt, x_sample, cache_a_k, cache_a_v, cache_b_k, cache_b_v, cache_b_idx,
              ln_g, ln_b, ffn_w_gate, ffn_w_up, ffn_w_down,
              a_w_in, a_rel_bias, a_w_out, b_w_in, b_w_out):
    xp, xs = x_prompt, x_sample
    a_kp, a_vp, a_ks, a_vs = [], [], [], []
    b_kp, b_vp, b_ip, b_ks, b_vs, b_is = [], [], [], [], [], []
    for layer in range(DEPTH):
        mixer = layer % N_MIXERS
        j = layer // N_MIXERS
        ffn1 = (ffn_w_gate[layer, 0], ffn_w_up[layer, 0], ffn_w_down[layer, 0], ln_g[layer, 0], ln_b[layer, 0])
        xp = half_ffn_block(xp, *ffn1)
        xs = half_ffn_block(xs, *ffn1)
        if mixer == 0:
            yp, kp, vp = mixer_a_prompt(xp, a_w_in[j], a_rel_bias[j], a_w_out[j])
            ys, k_s, v_s = mixer_a_sample(xs, cache_a_k[j], cache_a_v[j], a_w_in[j], a_rel_bias[j], a_w_out[j])
            a_kp.append(kp); a_vp.append(vp); a_ks.append(k_s); a_vs.append(v_s)
        else:
            yp, kp, vp, ip = mixer_b_prompt(xp, b_w_in[j], b_w_out[j])
            ys, k_s, v_s, i_s = mixer_b_sample(xs, cache_b_k[j], cache_b_v[j], cache_b_idx[j], b_w_in[j], b_w_out[j])
            b_kp.append(kp); b_vp.append(vp); b_ip.append(ip)
            b_ks.append(k_s); b_vs.append(v_s); b_is.append(i_s)
        xp = layer_norm(ALPHA * xp + yp, ln_g[layer, 1], ln_b[layer, 1])
        xs = layer_norm(ALPHA * xs + ys, ln_g[layer, 1], ln_b[layer, 1])
        ffn2 = (ffn_w_gate[layer, 1], ffn_w_up[layer, 1], ffn_w_down[layer, 1], ln_g[layer, 2], ln_b[layer, 2])
        xp = half_ffn_block(xp, *ffn2)
        xs = half_ffn_block(xs, *ffn2)
    a_k_prompt = jnp.stack(a_kp, 0)
    a_v_prompt = jnp.stack(a_vp, 0)
    a_k_sample = jnp.stack(a_ks, 0)
    a_v_sample = jnp.stack(a_vs, 0)
    b_k_prompt = jnp.stack(b_kp, 0)
    b_v_prompt = jnp.stack(b_vp, 0)
    b_idx_prompt = jnp.stack(b_ip, 0)
    b_k_sample = jnp.stack(b_ks, 0)
    b_v_sample = jnp.stack(b_vs, 0)
    b_idx_sample = jnp.stack(b_is, 0)
    return (xp, xs, a_k_prompt, a_v_prompt, a_k_sample, a_v_sample,
            b_k_prompt, b_v_prompt, b_idx_prompt, b_k_sample, b_v_sample, b_idx_sample)
```

```python
import functools
import math

import jax
import jax.numpy as jnp
from jax import lax
from jax.experimental import pallas as pl
from jax.experimental.pallas import tpu as pltpu

F32 = jnp.float32
BF16 = jnp.bfloat16
I32 = jnp.int32

CHUNK = 64
A_PAST = 512
REL_CLIP = 128
HEAD_DIM = 64
LANES = 128
IDX_HEADS = 8
TOPK_MAX = 256
LN_EPS = 1e-5
A_SUB = 2 * CHUNK
A_WIN = A_PAST + A_SUB
VMEM_LIMIT = 52 * 1024 * 1024

NT_DIMS = (((1,), (1,)), ((), ()))
INT_MIN = -2147483648
NEG_INF_KEY = -2139095041


def _params(n_grid):
    return pltpu.CompilerParams(dimension_semantics=("arbitrary",) * n_grid,
                                vmem_limit_bytes=VMEM_LIMIT)


def _layer_norm(z, g, b):
    mu = jnp.mean(z, axis=-1, keepdims=True)
    d = z - mu
    var = jnp.mean(d * d, axis=-1, keepdims=True)
    return d * lax.rsqrt(var + LN_EPS) * g + b


def _dot(a, b):
    return jnp.dot(a, b, preferred_element_type=F32)


def _dot_nt(a, b):
    return lax.dot_general(a, b, NT_DIMS, preferred_element_type=F32)


def _ffn_kernel(x_ref, wg_ref, wu_ref, wd_ref, g_ref, b_ref, o_ref, h_scr, *, n_chunks, alpha):
    x = x_ref[...]
    xb = x.astype(BF16)
    cf = wg_ref.shape[1] // n_chunks
    for c in range(n_chunks):
        sl = slice(c * cf, (c + 1) * cf)
        gate = _dot(xb, wg_ref[:, sl])
        up = _dot(xb, wu_ref[:, sl])
        h_scr[:, sl] = (gate * (1.0 / (1.0 + jnp.exp(-gate))) * up).astype(BF16)
    y = _dot(h_scr[...], wd_ref[...])
    o_ref[...] = _layer_norm(alpha * x + 0.5 * y, g_ref[...], b_ref[...])


def _ffn_block(x, wg, wu, wd, g, b, alpha, tm):
    m, d = x.shape
    f = wg.shape[1]
    const = lambda i: (0, 0)
    return pl.pallas_call(
        functools.partial(_ffn_kernel, n_chunks=4, alpha=alpha),
        grid=(m // tm,),
        in_specs=[pl.BlockSpec((tm, d), lambda i: (i, 0)),
                  pl.BlockSpec((d, f), const), pl.BlockSpec((d, f), const), pl.BlockSpec((f, d), const),
                  pl.BlockSpec((1, d), const), pl.BlockSpec((1, d), const)],
        out_specs=pl.BlockSpec((tm, d), lambda i: (i, 0)),
        out_shape=jax.ShapeDtypeStruct((m, d), F32),
        scratch_shapes=[pltpu.VMEM((tm, f), BF16)],
        compiler_params=_params(1),
        name="ffn_ln",
    )(x, wg, wu, wd, g, b)


def _proj_a_kernel(x_ref, w_ref, q_ref, k_ref, v_ref, kf_ref, vf_ref, *, width):
    h = _dot(x_ref[0].astype(BF16), w_ref[...])
    k = h[:, width:2 * width]
    v = h[:, 2 * width:]
    q_ref[0] = (h[:, :width] * (HEAD_DIM ** -0.5)).astype(BF16)
    k_ref[0] = k.astype(BF16)
    v_ref[0] = v.astype(BF16)

    @pl.when(pl.program_id(1) == pl.num_programs(1) - 1)
    def _():
        kf_ref[0] = k
        vf_ref[0] = v


def _proj_a(x, w, tm):
    bsz, s, d = x.shape
    width = w.shape[1] // 3
    row = pl.BlockSpec((1, tm, width), lambda b, t: (b, t, 0))
    last = pl.BlockSpec((1, tm, width), lambda b, t: (b, 0, 0))
    return pl.pallas_call(
        functools.partial(_proj_a_kernel, width=width),
        grid=(bsz, s // tm),
        in_specs=[pl.BlockSpec((1, tm, d), lambda b, t: (b, t, 0)),
                  pl.BlockSpec(w.shape, lambda b, t: (0, 0))],
        out_specs=[row, row, row, last, last],
        out_shape=[jax.ShapeDtypeStruct((bsz, s, width), BF16)] * 3
        + [jax.ShapeDtypeStruct((bsz, tm, width), F32)] * 2,
        compiler_params=_params(2),
        name="proj_a",
    )(x, w)


def _attn_a_kernel(*refs, n_sub, prompt, alpha):
    if prompt:
        (x_ref, q_ref, kp_ref, kc_ref, vp_ref, vc_ref, bias_ref, wo_ref, g_ref, b_ref,
         o_ref, kwin, vwin, o_scr) = refs
        kwin[0:A_PAST] = kp_ref[0]
        kwin[A_PAST:2 * A_PAST] = kc_ref[0]
        vwin[0:A_PAST] = vp_ref[0]
        vwin[A_PAST:2 * A_PAST] = vc_ref[0]
        first_valid = jnp.where(pl.program_id(1) == 0, A_PAST, 0)
        load_k = lambda r0, c0: kwin[pl.ds(r0, A_WIN), c0:c0 + LANES]
        load_v = lambda r0, c0: vwin[pl.ds(r0, A_WIN), c0:c0 + LANES]
    else:
        x_ref, q_ref, k_ref, v_ref, bias_ref, wo_ref, g_ref, b_ref, o_ref, o_scr = refs
        load_k = lambda r0, c0: k_ref[0, pl.ds(r0, A_WIN), c0:c0 + LANES]
        load_v = lambda r0, c0: v_ref[0, pl.ds(r0, A_WIN), c0:c0 + LANES]
    lane = lax.broadcasted_iota(I32, (A_SUB, LANES), 1)
    col = lax.broadcasted_iota(I32, (A_SUB, A_WIN), 1)

    def sub_tile(j, carry):
        r0 = pl.multiple_of(j * A_SUB, A_SUB)
        for p in range(q_ref.shape[2] // LANES):
            c0 = p * LANES
            q2 = q_ref[0, pl.ds(r0, A_SUB), c0:c0 + LANES]
            k2 = load_k(r0, c0)
            v2 = load_v(r0, c0)
            outs = []
            for half in range(2):
                in_half = (lane < HEAD_DIM) if half == 0 else (lane >= HEAD_DIM)
                s = _dot_nt(jnp.where(in_half, q2, jnp.zeros_like(q2)), k2) + bias_ref[2 * p + half]
                if prompt:
                    s = jnp.where(col >= first_valid - r0, s, -jnp.inf)
                e = jnp.exp(s - jnp.max(s, axis=-1, keepdims=True))
                outs.append(_dot(e.astype(BF16), v2) / jnp.sum(e, axis=-1, keepdims=True))
            o_scr[pl.ds(r0, A_SUB), c0:c0 + LANES] = jnp.where(lane < HEAD_DIM, outs[0], outs[1]).astype(BF16)
        return carry

    lax.fori_loop(0, n_sub, sub_tile, 0)
    y = _dot(o_scr[...], wo_ref[...])
    o_ref[0] = _layer_norm(alpha * x_ref[0] + y, g_ref[...], b_ref[...])


def _attn_a_prompt(x, q, k, v, bias, wo, g, b, alpha):
    bsz, s, d = x.shape
    width = q.shape[2]
    tq = A_PAST
    cur = lambda bi, t: (bi, t, 0)
    prev = lambda bi, t: (bi, jnp.maximum(t - 1, 0), 0)
    const2 = lambda bi, t: (0, 0)
    return pl.pallas_call(
        functools.partial(_attn_a_kernel, n_sub=tq // A_SUB, prompt=True, alpha=alpha),
        grid=(bsz, s // tq),
        in_specs=[pl.BlockSpec((1, tq, d), cur), pl.BlockSpec((1, tq, width), cur),
                  pl.BlockSpec((1, tq, width), prev), pl.BlockSpec((1, tq, width), cur),
                  pl.BlockSpec((1, tq, width), prev), pl.BlockSpec((1, tq, width), cur),
                  pl.BlockSpec(bias.shape, lambda bi, t: (0, 0, 0)),
                  pl.BlockSpec(wo.shape, const2), pl.BlockSpec((1, d), const2), pl.BlockSpec((1, d), const2)],
        out_specs=pl.BlockSpec((1, tq, d), cur),
        out_shape=jax.ShapeDtypeStruct((bsz, s, d), F32),
        scratch_shapes=[pltpu.VMEM((2 * tq, width), BF16), pltpu.VMEM((2 * tq, width), BF16),
                        pltpu.VMEM((tq, width), BF16)],
        compiler_params=_params(2),
        name="attn_a_prompt",
    )(x, q, k, k, v, v, bias, wo, g, b)


def _attn_a_sample(x, q, kwin, vwin, bias, wo, g, b, alpha):
    bsz, _, d = x.shape
    width = q.shape[2]
    blk = lambda n, c: pl.BlockSpec((1, n, c), lambda bi: (bi, 0, 0))
    const2 = lambda bi: (0, 0)
    return pl.pallas_call(
        functools.partial(_attn_a_kernel, n_sub=1, prompt=False, alpha=alpha),
        grid=(bsz,),
        in_specs=[blk(A_SUB, d), blk(A_SUB, width), blk(A_WIN, width), blk(A_WIN, width),
                  pl.BlockSpec(bias.shape, lambda bi: (0, 0, 0)),
                  pl.BlockSpec(wo.shape, const2), pl.BlockSpec((1, d), const2), pl.BlockSpec((1, d), const2)],
        out_specs=blk(A_SUB, d),
        out_shape=jax.ShapeDtypeStruct((bsz, A_SUB, d), F32),
        scratch_shapes=[pltpu.VMEM((A_SUB, width), BF16)],
        compiler_params=_params(1),
        name="attn_a_sample",
    )(x, q, kwin, vwin, bias, wo, g, b)


def _rel_bias_tables(rel_bias, n_sample):
    r = jnp.arange(A_SUB)[:, None]
    c = jnp.arange(A_WIN)[None, :]
    rel = jnp.clip(r + A_PAST - c, -REL_CLIP, REL_CLIP) + REL_CLIP
    table = rel_bias[:, rel].astype(F32)
    lo = (r // CHUNK) * CHUNK
    band = (c >= lo) & (c < lo + A_PAST + CHUNK)
    prompt = jnp.where(band[None], table, -jnp.inf)
    live = (c < A_PAST + n_sample)
    sample = jnp.where(live[None], jnp.where((r < n_sample)[None], table, 0.0), -jnp.inf)
    return prompt, sample


def _proj_b_kernel(x_ref, w_ref, q_ref, qi_ref, k_ref, kf_ref, vf_ref, va_ref, kiwi_ref, kd_ref, wit_ref,
                   *, nq, nkv, nqi):
    h = _dot(x_ref[...].astype(BF16), w_ref[...])
    q_ref[...] = (h[:, :nq] * (HEAD_DIM ** -0.5)).astype(BF16)
    k = h[:, nq:nq + nkv]
    v = h[:, nq + nkv:nq + 2 * nkv]
    k_ref[...] = k.astype(BF16)
    kf_ref[...] = k
    vf_ref[...] = v
    o0 = nq + 2 * nkv
    qi_ref[...] = (h[:, o0:o0 + nqi] * (HEAD_DIM ** -0.5)).astype(BF16)
    tail = h[:, o0 + nqi:o0 + nqi + LANES]
    kiwi_ref[...] = tail
    lane = lax.broadcasted_iota(I32, tail.shape, 1)
    kd_ref[...] = jnp.where(lane < HEAD_DIM, tail, pltpu.roll(tail, HEAD_DIM, 1)).astype(BF16)
    wit_ref[...] = tail.T[HEAD_DIM:HEAD_DIM + IDX_HEADS, :] * (IDX_HEADS ** -0.5)
    for pair in range(nkv // LANES):
        vp = v[:, pair * LANES:(pair + 1) * LANES]
        va_ref[:, (2 * pair) * LANES:(2 * pair + 1) * LANES] = jnp.where(lane < HEAD_DIM, vp, 1.0).astype(BF16)
        va_ref[:, (2 * pair + 1) * LANES:(2 * pair + 2) * LANES] = jnp.where(
            lane < HEAD_DIM, pltpu.roll(vp, HEAD_DIM, 1), 1.0).astype(BF16)


def _proj_b(x, w_pad, nq, nkv, nqi, tm):
    m, d = x.shape
    rows = lambda c: pl.BlockSpec((tm, c), lambda i: (i, 0))
    outs = [(nq, BF16), (nqi, BF16), (nkv, BF16), (nkv, F32), (nkv, F32), (2 * nkv, BF16), (LANES, F32), (LANES, BF16)]
    return pl.pallas_call(
        functools.partial(_proj_b_kernel, nq=nq, nkv=nkv, nqi=nqi),
        grid=(m // tm,),
        in_specs=[rows(d), pl.BlockSpec(w_pad.shape, lambda i: (0, 0))],
        out_specs=[rows(c) for c, _ in outs] + [pl.BlockSpec((IDX_HEADS, tm), lambda i: (0, i))],
        out_shape=[jax.ShapeDtypeStruct((m, c), dt) for c, dt in outs] + [jax.ShapeDtypeStruct((IDX_HEADS, m), F32)],
        compiler_params=_params(1),
        name="proj_b",
    )(x, w_pad)


def _sparse_kernel(x_ref, q_ref, qi_ref, wit_ref, k_ref, va_ref, kd_ref, wo_ref, g_ref, b_ref, o_ref,
                   key_scr, sel_scr, qm_scr, acc_scr, m_scr, o_scr,
                   *, tq, kb, pos0, n_keys, n_sel, n_heads, group, slopes, alpha):
    t = pl.program_id(1)
    q0 = pos0 + t * tq
    qpos_row = q0 + lax.broadcasted_iota(I32, (1, tq), 1)
    lim_row = jnp.minimum((lax.shift_right_logical(qpos_row, 6) + 1) * CHUNK, n_keys)
    kmax = jnp.minimum((lax.shift_right_logical(q0 + tq - 1, 6) + 1) * CHUNK, n_keys)
    nkb = lax.shift_right_logical(kmax + kb - 1, int(math.log2(kb)))
    lane = lax.broadcasted_iota(I32, (tq, LANES), 1)
    lo_half = lane < HEAD_DIM

    def key_block(kbi):
        return pl.multiple_of(kbi * kb, kb)

    def key_index(off):
        return off + lax.broadcasted_iota(I32, (kb, tq), 0)

    qis = []
    for p in range(IDX_HEADS // 2):
        q2 = qi_ref[0, :, p * LANES:(p + 1) * LANES]
        qis.append(jnp.where(lo_half, q2, jnp.zeros_like(q2)))
        qis.append(jnp.where(lo_half, jnp.zeros_like(q2), q2))
    wis = wit_ref[...]

    def score_block(kbi, carry):
        off = key_block(kbi)
        kid = kd_ref[0, pl.ds(off, kb), :]
        acc = jnp.zeros((kb, tq), F32)
        for h in range(IDX_HEADS):
            acc = acc + jnp.maximum(_dot_nt(kid, qis[h]), 0.0) * wis[h:h + 1, :]
        acc = jnp.where(key_index(off) < lim_row, acc, -jnp.inf)
        bits = lax.bitcast_convert_type(acc, I32)
        key_scr[pl.ds(off, kb), :] = bits ^ (lax.shift_right_arithmetic(bits, 31) & 0x7FFFFFFF)
        return carry

    lax.fori_loop(0, nkb, score_block, 0)

    def count(pred):
        def body(kbi, c8):
            hit = pred(key_scr[pl.ds(key_block(kbi), kb), :])
            return c8 + jnp.sum(jnp.where(hit, 1, 0).reshape(kb // 8, 8, tq), axis=0)
        c8 = lax.fori_loop(0, nkb, body, jnp.zeros((8, tq), I32))
        return jnp.sum(c8, axis=0, keepdims=True)

    def bisect(i, thr):
        cand = thr + lax.shift_left(jnp.int32(1), 31 - i)
        return jnp.where(count(lambda key: key >= cand) >= n_sel, cand, thr)

    thr = lax.fori_loop(0, 32, bisect, jnp.full((1, tq), INT_MIN, I32))
    n_gt = count(lambda key: key > thr)
    n_eq = count(lambda key: key == thr)
    need = n_sel - n_gt
    tie = jnp.max(jnp.where((n_eq > need) & (thr > NEG_INF_KEY), 1, 0)) > 0

    @pl.when(jnp.logical_not(tie))
    def _():
        def body(kbi, carry):
            off = key_block(kbi)
            sel = (key_scr[pl.ds(off, kb), :] >= thr) & (key_index(off) < lim_row)
            sel_scr[pl.ds(off, kb), :] = jnp.where(sel, 1.0, 0.0).astype(BF16)
            return carry
        lax.fori_loop(0, nkb, body, 0)

    @pl.when(tie)
    def _():
        tri = jnp.where(lax.broadcasted_iota(I32, (kb, kb), 0) > lax.broadcasted_iota(I32, (kb, kb), 1),
                        1.0, 0.0).astype(BF16)
        need_f = need.astype(F32)

        def body(kbi, seen):
            off = key_block(kbi)
            key = key_scr[pl.ds(off, kb), :]
            eq = jnp.where(key == thr, 1.0, 0.0)
            earlier = _dot(tri, eq.astype(BF16)) + seen
            sel = ((key > thr) | ((key == thr) & (earlier < need_f))) & (key_index(off) < lim_row)
            sel_scr[pl.ds(off, kb), :] = jnp.where(sel, 1.0, 0.0).astype(BF16)
            return seen + jnp.sum(eq, axis=0, keepdims=True)
        lax.fori_loop(0, nkb, body, jnp.zeros((1, tq), F32))

    for h in range(n_heads):
        kv_half = (h // group) % 2
        q2 = q_ref[0, :, (h // 2) * LANES:(h // 2 + 1) * LANES].astype(F32)
        if h % 2 != kv_half:
            q2 = pltpu.roll(q2, HEAD_DIM, 1)
        qm_scr[h] = jnp.where(lo_half if kv_half == 0 else jnp.logical_not(lo_half), q2, 0.0).astype(BF16)
    m_scr[...] = jnp.full(m_scr.shape, -jnp.inf, F32)
    acc_scr[...] = jnp.zeros(acc_scr.shape, F32)
    eye = jnp.where(lax.broadcasted_iota(I32, (tq, tq), 0) == lax.broadcasted_iota(I32, (tq, tq), 1),
                    1.0, 0.0).astype(BF16)
    qpos_col = (q0 + lax.broadcasted_iota(I32, (tq, 1), 0)).astype(F32)

    def attend(kbi, carry):
        off = key_block(kbi)
        picked = _dot_nt(eye, sel_scr[pl.ds(off, kb), :]) > 0.5
        kpos = (off + lax.broadcasted_iota(I32, (1, kb), 1)).astype(F32)
        dist = jnp.abs(qpos_col - kpos)
        for kvh in range(n_heads // group):
            k2 = k_ref[0, pl.ds(off, kb), (kvh // 2) * LANES:(kvh // 2 + 1) * LANES]
            va = va_ref[0, pl.ds(off, kb), kvh * LANES:(kvh + 1) * LANES]
            for j in range(group):
                h = kvh * group + j
                s = jnp.where(picked, _dot_nt(qm_scr[h], k2) - slopes[h] * dist, -jnp.inf)
                m_old = m_scr[h]
                m_new = jnp.maximum(m_old, jnp.max(s, axis=-1, keepdims=True))
                m_use = jnp.where(m_new == -jnp.inf, 0.0, m_new)
                p = jnp.exp(s - m_use)
                acc_scr[h] = jnp.exp(m_old - m_use) * acc_scr[h] + _dot(p.astype(BF16), va)
                m_scr[h] = m_new
        return carry

    lax.fori_loop(0, nkb, attend, 0)

    for pair in range(n_heads // 2):
        a0 = acc_scr[2 * pair]
        a1 = acc_scr[2 * pair + 1]
        o_pair = jnp.where(lo_half, a0 / pltpu.roll(a0, HEAD_DIM, 1), pltpu.roll(a1, HEAD_DIM, 1) / a1)
        o_scr[:, pair * LANES:(pair + 1) * LANES] = o_pair.astype(BF16)
    y = _dot(o_scr[...], wo_ref[...])
    o_ref[0] = _layer_norm(alpha * x_ref[0] + y, g_ref[...], b_ref[...])


def _sparse_attn(x, q, qi, wit, k, va, kd, wo, g, b, *, tq, kb, pos0, n_keys, alpha):
    bsz, s, d = x.shape
    n_t = s // tq
    l_pad = k.shape[1]
    n_heads = q.shape[2] // HEAD_DIM
    group = n_heads // (k.shape[2] // HEAD_DIM)
    slopes = tuple(2.0 ** (-8.0 * (h + 1) / n_heads) for h in range(n_heads))
    qrow = lambda c: pl.BlockSpec((1, tq, c), lambda bi, t: (bi, t, 0))
    keys = lambda c: pl.BlockSpec((1, l_pad, c), lambda bi, t: (bi, 0, 0))
    const2 = lambda bi, t: (0, 0)
    kern = functools.partial(_sparse_kernel, tq=tq, kb=kb, pos0=pos0, n_keys=n_keys, n_sel=min(TOPK_MAX, n_keys // 4),
                             n_heads=n_heads, group=group, slopes=slopes, alpha=alpha)
    return pl.pallas_call(
        kern,
        grid=(bsz, n_t),
        in_specs=[qrow(d), qrow(q.shape[2]), qrow(qi.shape[2]),
                  pl.BlockSpec((IDX_HEADS, tq), lambda bi, t: (0, bi * n_t + t)),
                  keys(k.shape[2]), keys(va.shape[2]), keys(kd.shape[2]),
                  pl.BlockSpec(wo.shape, const2), pl.BlockSpec((1, d), const2), pl.BlockSpec((1, d), const2)],
        out_specs=qrow(d),
        out_shape=jax.ShapeDtypeStruct((bsz, s, d), F32),
        scratch_shapes=[pltpu.VMEM((l_pad, tq), I32), pltpu.VMEM((l_pad, tq), BF16),
                        pltpu.VMEM((n_heads, tq, LANES), BF16), pltpu.VMEM((n_heads, tq, LANES), F32),
                        pltpu.VMEM((n_heads, tq, 1), F32), pltpu.VMEM((tq, q.shape[2]), BF16)],
        compiler_params=_params(2),
        name="sparse_attn",
    )(x, q, qi, wit, k, va, kd, wo, g, b)


def _pad_rows(a, n):
    return jnp.pad(a, ((0, 0), (0, n - a.shape[1]), (0, 0)))


def _mixer_a(xp, xs, cache_k, cache_v, w_in, rel_bias, w_out, g, b, alpha):
    bsz, s, d = xp.shape
    dbs, t, _ = xs.shape
    width = w_out.shape[0]
    w_in = w_in.astype(BF16)
    w_out = w_out.astype(BF16)
    bias_p, bias_s = _rel_bias_tables(rel_bias, t)
    keep = min(A_PAST, s)
    q, k, v, kf, vf = _proj_a(xp, w_in, keep)
    yp = _attn_a_prompt(xp, q, k, v, bias_p, w_out, g, b, alpha)
    qs, ks, vs, ksf, vsf = _proj_a(xs.reshape(1, dbs * t, d), w_in, dbs * t)
    n_cache = cache_k.shape[1]
    kwin = _pad_rows(jnp.concatenate([cache_k.reshape(dbs, n_cache, width).astype(BF16), ks.reshape(dbs, t, width)], 1), A_WIN)
    vwin = _pad_rows(jnp.concatenate([cache_v.reshape(dbs, n_cache, width).astype(BF16), vs.reshape(dbs, t, width)], 1), A_WIN)
    ys = _attn_a_sample(_pad_rows(xs, A_SUB), _pad_rows(qs.reshape(dbs, t, width), A_SUB), kwin, vwin,
                        bias_s, w_out, g, b, alpha)[:, :t]
    heads = width // HEAD_DIM
    return (yp, ys, kf.reshape(bsz, keep, heads, HEAD_DIM), vf.reshape(bsz, keep, heads, HEAD_DIM),
            ksf.reshape(dbs, t, heads, HEAD_DIM), vsf.reshape(dbs, t, heads, HEAD_DIM))


def _mixer_b(xp, xs, cache_k, cache_v, cache_idx, w_in, w_out, g, b, alpha):
    bsz, s, d = xp.shape
    dbs, t, _ = xs.shape
    nq = w_out.shape[0]
    nkv = cache_k.shape[2] * cache_k.shape[3]
    nqi = IDX_HEADS * HEAD_DIM
    kvh = nkv // HEAD_DIM
    past = cache_k.shape[1]
    w_pad = jnp.pad(w_in, ((0, 0), (0, nq + 2 * nkv + nqi + LANES - w_in.shape[1]))).astype(BF16)
    w_out = w_out.astype(BF16)
    kb = 512

    q, qi, k, kf, vf, va, kiwi, kd, wit = _proj_b(xp.reshape(bsz * s, d), w_pad, nq, nkv, nqi, 512)
    r3 = lambda a: a.reshape(bsz, s, a.shape[1])
    yp = _sparse_attn(xp, r3(q), r3(qi), wit, r3(k), r3(va), r3(kd), w_out, g, b,
                      tq=256, kb=kb, pos0=0, n_keys=s, alpha=alpha)

    qs, qis, ks, ksf, vsf, vas, kiwis, kds, wits = _proj_b(xs.reshape(dbs * t, d), w_pad, nq, nkv, nqi, dbs * t)
    tq_s = LANES
    n_keys = past + t
    l_pad = -(-n_keys // kb) * kb
    s3 = lambda a: a.reshape(dbs, t, a.shape[1])
    ones = jnp.ones((dbs, past, kvh, HEAD_DIM), BF16)
    va_cache = jnp.concatenate([cache_v.astype(BF16), ones], axis=-1).reshape(dbs, past, 2 * nkv)
    kd_cache = jnp.concatenate([cache_idx, cache_idx], axis=-1).astype(BF16)
    keys_k = _pad_rows(jnp.concatenate([cache_k.reshape(dbs, past, nkv).astype(BF16), s3(ks)], 1), l_pad)
    keys_va = _pad_rows(jnp.concatenate([va_cache, s3(vas)], 1), l_pad)
    keys_kd = _pad_rows(jnp.concatenate([kd_cache, s3(kds)], 1), l_pad)
    wit_pad = jnp.pad(wits.reshape(IDX_HEADS, dbs, t), ((0, 0), (0, 0), (0, tq_s - t))).reshape(IDX_HEADS, dbs * tq_s)
    ys = _sparse_attn(_pad_rows(xs, tq_s), _pad_rows(s3(qs), tq_s), _pad_rows(s3(qis), tq_s), wit_pad,
                      keys_k, keys_va, keys_kd, w_out, g, b,
                      tq=tq_s, kb=kb, pos0=past, n_keys=n_keys, alpha=alpha)[:, :t]
    return (yp, ys,
            kf.reshape(bsz, s, kvh, HEAD_DIM), vf.reshape(bsz, s, kvh, HEAD_DIM), kiwi[:, :HEAD_DIM].reshape(bsz, s, HEAD_DIM),
            ksf.reshape(dbs, t, kvh, HEAD_DIM), vsf.reshape(dbs, t, kvh, HEAD_DIM), kiwis[:, :HEAD_DIM].reshape(dbs, t, HEAD_DIM))


def kernel(x_prompt, x_sample, cache_a_k, cache_a_v, cache_b_k, cache_b_v, cache_b_idx, ln_g, ln_b, ffn_w_gate, ffn_w_up, ffn_w_down, a_w_in, a_rel_bias, a_w_out, b_w_in, b_w_out):
    depth = ln_g.shape[0]
    alpha = (2.0 * depth) ** 0.25
    bsz, s, d = x_prompt.shape
    dbs, t, _ = x_sample.shape
    xp, xs = x_prompt, x_sample
    a_out, b_out = [], []

    def ffn(x, layer, i, tm):
        shp = x.shape
        y = _ffn_block(x.reshape(-1, d), ffn_w_gate[layer, i].astype(BF16), ffn_w_up[layer, i].astype(BF16),
                       ffn_w_down[layer, i].astype(BF16), ln_g[layer, 2 * i][None], ln_b[layer, 2 * i][None], alpha, tm)
        return y.reshape(shp)

    for layer in range(depth):
        j = layer // 2
        xp = ffn(xp, layer, 0, 512)
        xs = ffn(xs, layer, 0, dbs * t)
        g, b = ln_g[layer, 1][None], ln_b[layer, 1][None]
        if layer % 2 == 0:
            xp, xs, *rest = _mixer_a(xp, xs, cache_a_k[j], cache_a_v[j], a_w_in[j], a_rel_bias[j], a_w_out[j], g, b, alpha)
            a_out.append(rest)
        else:
            xp, xs, *rest = _mixer_b(xp, xs, cache_b_k[j], cache_b_v[j], cache_b_idx[j], b_w_in[j], b_w_out[j], g, b, alpha)
            b_out.append(rest)
        xp = ffn(xp, layer, 1, 512)
        xs = ffn(xs, layer, 1, dbs * t)

    stack = lambda outs, i: jnp.stack([o[i] for o in outs], 0)
    return (xp, xs,
            stack(a_out, 0), stack(a_out, 1), stack(a_out, 2), stack(a_out, 3),
            stack(b_out, 0), stack(b_out, 1), stack(b_out, 2), stack(b_out, 3), stack(b_out, 4), stack(b_out, 5))
```

```python
import functools
import math
import struct

import jax
import jax.numpy as jnp
from jax import lax
from jax.experimental import pallas as pl
from jax.experimental.pallas import tpu as pltpu

F32 = jnp.float32
BF16 = jnp.bfloat16
I32 = jnp.int32

CHUNK = 64
A_PAST = 512
REL_CLIP = 128
HEAD_DIM = 64
LANES = 128
IDX_HEADS = 8
TOPK_MAX = 256
LN_EPS = 1e-5
A_SUB = 4 * CHUNK
A_WIN = A_PAST + A_SUB
VMEM_LIMIT = 52 * 1024 * 1024

LOG2E = 1.4426950408889634
POS_LANE0 = HEAD_DIM
MASK_BIAS = 1e30
ATTEND_ROWS = 128

NT_DIMS = (((1,), (1,)), ((), ()))
INT_MIN = -2147483648
NEG_INF_KEY = -2139095041


def _params(n_grid):
    return pltpu.CompilerParams(dimension_semantics=("arbitrary",) * n_grid,
                                vmem_limit_bytes=VMEM_LIMIT)


def _resident(shape):
    zeros = (0,) * len(shape)
    return pl.BlockSpec(shape, lambda *_: zeros, pipeline_mode=pl.Buffered(1))


def _layer_norm(z, g, b):
    mu = jnp.mean(z, axis=-1, keepdims=True)
    d = z - mu
    var = jnp.mean(d * d, axis=-1, keepdims=True)
    return d * lax.rsqrt(var + LN_EPS) * g + b


def _dot(a, b):
    return jnp.dot(a, b, preferred_element_type=F32)


def _dot_nt(a, b):
    return lax.dot_general(a, b, NT_DIMS, preferred_element_type=F32)


def _ffn_kernel(x_ref, wg_ref, wu_ref, wd_ref, g_ref, b_ref, o_ref, h_scr, *, n_chunks, alpha):
    x = x_ref[...]
    xb = x.astype(BF16)
    cf = wg_ref.shape[1] // n_chunks
    for c in range(n_chunks):
        sl = slice(c * cf, (c + 1) * cf)
        gate = _dot(xb, wg_ref[:, sl])
        up = _dot(xb, wu_ref[:, sl])
        h_scr[:, sl] = (gate * (1.0 / (1.0 + jnp.exp(-gate))) * up).astype(BF16)
    y = _dot(h_scr[...], wd_ref[...])
    o_ref[...] = _layer_norm(alpha * x + 0.5 * y, g_ref[...], b_ref[...])


def _ffn_block(x, wg, wu, wd, g, b, alpha, tm):
    m, d = x.shape
    f = wg.shape[1]
    const = lambda i: (0, 0)
    return pl.pallas_call(
        functools.partial(_ffn_kernel, n_chunks=4, alpha=alpha),
        grid=(m // tm,),
        in_specs=[pl.BlockSpec((tm, d), lambda i: (i, 0)),
                  _resident((d, f)), _resident((d, f)), _resident((f, d)),
                  pl.BlockSpec((1, d), const), pl.BlockSpec((1, d), const)],
        out_specs=pl.BlockSpec((tm, d), lambda i: (i, 0)),
        out_shape=jax.ShapeDtypeStruct((m, d), F32),
        scratch_shapes=[pltpu.VMEM((tm, f), BF16)],
        compiler_params=_params(1),
        name="ffn_ln",
    )(x, wg, wu, wd, g, b)


def _proj_a_kernel(x_ref, w_ref, q_ref, k_ref, vt_ref, kf_ref, vf_ref, *, width):
    h = _dot(x_ref[0].astype(BF16), w_ref[...])
    k = h[:, width:2 * width]
    v = h[:, 2 * width:]
    q_ref[0] = (h[:, :width] * (HEAD_DIM ** -0.5 * LOG2E)).astype(BF16)
    k_ref[0] = k.astype(BF16)
    vt_ref[0] = v.T.astype(BF16)

    @pl.when(pl.program_id(1) == pl.num_programs(1) - 1)
    def _():
        kf_ref[0] = k
        vf_ref[0] = v


def _proj_a(x, w, tm):
    bsz, s, d = x.shape
    width = w.shape[1] // 3
    row = pl.BlockSpec((1, tm, width), lambda b, t: (b, t, 0))
    last = pl.BlockSpec((1, tm, width), lambda b, t: (b, 0, 0))
    return pl.pallas_call(
        functools.partial(_proj_a_kernel, width=width),
        grid=(bsz, s // tm),
        in_specs=[pl.BlockSpec((1, tm, d), lambda b, t: (b, t, 0)), _resident(w.shape)],
        out_specs=[row, row, pl.BlockSpec((1, width, tm), lambda b, t: (b, 0, t)), last, last],
        out_shape=[jax.ShapeDtypeStruct((bsz, s, width), BF16)] * 2 + [jax.ShapeDtypeStruct((bsz, width, s), BF16)]
        + [jax.ShapeDtypeStruct((bsz, tm, width), F32)] * 2,
        compiler_params=_params(2),
        name="proj_a",
    )(x, w)


def _attn_a_kernel(*refs, n_sub, prompt, alpha):
    if prompt:
        (x_ref, q_ref, kp_ref, kc_ref, vtp_ref, vtc_ref, bias_ref, wo_ref, g_ref, b_ref,
         o_ref, kwin, vtwin, ot_scr) = refs
        kwin[0:A_PAST] = kp_ref[0]
        kwin[A_PAST:2 * A_PAST] = kc_ref[0]
        vtwin[:, 0:A_PAST] = vtp_ref[0]
        vtwin[:, A_PAST:2 * A_PAST] = vtc_ref[0]
        first_valid = jnp.where(pl.program_id(1) == 0, A_PAST, 0)
        k_at = lambda r0, c0: kwin[r0:r0 + A_WIN, c0:c0 + LANES]
        vt_at = lambda r0, c0: vtwin[c0:c0 + LANES, r0:r0 + A_WIN]
    else:
        x_ref, q_ref, k_ref, vt_ref, bias_ref, wo_ref, g_ref, b_ref, o_ref, ot_scr = refs
        k_at = lambda r0, c0: k_ref[0, r0:r0 + A_WIN, c0:c0 + LANES]
        vt_at = lambda r0, c0: vt_ref[0, c0:c0 + LANES, r0:r0 + A_WIN]
    lane = lax.broadcasted_iota(I32, (A_SUB, LANES), 1)
    key_row = lax.broadcasted_iota(I32, (A_WIN, 2 * A_SUB), 0)
    out_row = lax.broadcasted_iota(I32, (LANES, A_SUB), 0)

    for j in range(n_sub):
        r0 = j * A_SUB
        for p in range(q_ref.shape[2] // LANES):
            c0 = p * LANES
            q2 = q_ref[0, r0:r0 + A_SUB, c0:c0 + LANES].astype(F32)
            qt = jnp.concatenate([jnp.where(lane < HEAD_DIM, q2, 0.0).T, jnp.where(lane < HEAD_DIM, 0.0, q2).T], axis=1)
            s = _dot(k_at(r0, c0), qt.astype(BF16)) + bias_ref[p]
            if prompt:
                s = jnp.where(key_row >= first_valid - r0, s, -jnp.inf)
            e = jnp.exp2(s - jnp.max(s, axis=0, keepdims=True))
            o = _dot(vt_at(r0, c0), e.astype(BF16)) / jnp.sum(e, axis=0, keepdims=True)
            ot_scr[c0:c0 + LANES, r0:r0 + A_SUB] = jnp.where(out_row < HEAD_DIM, o[:, :A_SUB], o[:, A_SUB:]).astype(BF16)

    y = lax.dot_general(ot_scr[...], wo_ref[...], (((0,), (0,)), ((), ())), preferred_element_type=F32)
    o_ref[0] = _layer_norm(alpha * x_ref[0] + y, g_ref[...], b_ref[...])


def _attn_a_prompt(x, q, k, vt, bias, wo, g, b, alpha):
    bsz, s, d = x.shape
    width = q.shape[2]
    tq = A_PAST
    cur = lambda bi, t: (bi, t, 0)
    prev = lambda bi, t: (bi, jnp.maximum(t - 1, 0), 0)
    cur_t = lambda bi, t: (bi, 0, t)
    prev_t = lambda bi, t: (bi, 0, jnp.maximum(t - 1, 0))
    const2 = lambda bi, t: (0, 0)
    return pl.pallas_call(
        functools.partial(_attn_a_kernel, n_sub=tq // A_SUB, prompt=True, alpha=alpha),
        grid=(bsz, s // tq),
        in_specs=[pl.BlockSpec((1, tq, d), cur), pl.BlockSpec((1, tq, width), cur),
                  pl.BlockSpec((1, tq, width), prev), pl.BlockSpec((1, tq, width), cur),
                  pl.BlockSpec((1, width, tq), prev_t), pl.BlockSpec((1, width, tq), cur_t),
                  _resident(bias.shape), _resident(wo.shape),
                  pl.BlockSpec((1, d), const2), pl.BlockSpec((1, d), const2)],
        out_specs=pl.BlockSpec((1, tq, d), cur),
        out_shape=jax.ShapeDtypeStruct((bsz, s, d), F32),
        scratch_shapes=[pltpu.VMEM((2 * tq, width), BF16), pltpu.VMEM((width, 2 * tq), BF16),
                        pltpu.VMEM((width, tq), BF16)],
        compiler_params=_params(2),
        name="attn_a_prompt",
    )(x, q, k, k, vt, vt, bias, wo, g, b)


def _attn_a_sample(x, q, kwin, vtwin, bias, wo, g, b, alpha):
    bsz, _, d = x.shape
    width = q.shape[2]
    blk = lambda n, c: pl.BlockSpec((1, n, c), lambda bi: (bi, 0, 0))
    const2 = lambda bi: (0, 0)
    return pl.pallas_call(
        functools.partial(_attn_a_kernel, n_sub=1, prompt=False, alpha=alpha),
        grid=(bsz,),
        in_specs=[blk(A_SUB, d), blk(A_SUB, width), blk(A_WIN, width), blk(width, A_WIN),
                  _resident(bias.shape), _resident(wo.shape),
                  pl.BlockSpec((1, d), const2), pl.BlockSpec((1, d), const2)],
        out_specs=blk(A_SUB, d),
        out_shape=jax.ShapeDtypeStruct((bsz, A_SUB, d), F32),
        scratch_shapes=[pltpu.VMEM((width, A_SUB), BF16)],
        compiler_params=_params(1),
        name="attn_a_sample",
    )(x, q, kwin, vtwin, bias, wo, g, b)


def _rel_bias_tables(rel_bias, n_sample):
    r = jnp.arange(A_SUB)[:, None]
    c = jnp.arange(A_WIN)[None, :]
    rel = jnp.clip(r + A_PAST - c, -REL_CLIP, REL_CLIP) + REL_CLIP
    table = rel_bias[:, rel].astype(F32) * LOG2E
    lo = (r // CHUNK) * CHUNK
    band = (c >= lo) & (c < lo + A_PAST + CHUNK)
    prompt = jnp.where(band[None], table, -jnp.inf)
    live = (c < A_PAST + n_sample)
    sample = jnp.where(live[None], jnp.where((r < n_sample)[None], table, 0.0), -jnp.inf)

    def pair_layout(tab):
        h = tab.shape[0]
        return tab.transpose(0, 2, 1).reshape(h // 2, 2, A_WIN, A_SUB).transpose(0, 2, 1, 3).reshape(h // 2, A_WIN, 2 * A_SUB)

    return pair_layout(prompt), pair_layout(sample)


def _round_to_bf16(x):
    bits = struct.unpack("<I", struct.pack("<f", x))[0]
    bits = (bits + 0x7FFF + ((bits >> 16) & 1)) & 0xFFFF0000
    return struct.unpack("<f", struct.pack("<I", bits))[0]


def _bf16_pieces(x):
    p1 = _round_to_bf16(x)
    p2 = _round_to_bf16(x - p1)
    return (p1, p2, _round_to_bf16(x - p1 - p2))


def _pos_lanes(pos, lane):
    hi = (lax.shift_right_logical(pos, 6) * CHUNK).astype(F32)
    lo = (pos & (CHUNK - 1)).astype(F32)
    return jnp.where(lane < POS_LANE0 + 3, hi, jnp.where(lane < POS_LANE0 + 6, lo, 0.0))


def _proj_b_kernel(x_ref, w_ref, q_ref, qi_ref, ka_ref, kf_ref, vf_ref, kiwi_ref, kd_ref, wit_ref, vt_ref,
                   *, nq, nkv, nqi, pos0, period):
    h = _dot(x_ref[...].astype(BF16), w_ref[...])
    q_ref[...] = (h[:, :nq] * (HEAD_DIM ** -0.5 * LOG2E)).astype(BF16)
    k = h[:, nq:nq + nkv]
    v = h[:, nq + nkv:nq + 2 * nkv]
    kf_ref[...] = k
    vf_ref[...] = v
    o0 = nq + 2 * nkv
    qi_ref[...] = (h[:, o0:o0 + nqi] * (HEAD_DIM ** -0.5)).astype(BF16)
    tail = h[:, o0 + nqi:o0 + nqi + LANES]
    kiwi_ref[...] = tail
    lane = lax.broadcasted_iota(I32, tail.shape, 1)
    kd_ref[...] = jnp.where(lane < HEAD_DIM, tail, pltpu.roll(tail, HEAD_DIM, 1)).astype(BF16)
    wit_ref[...] = tail.T[HEAD_DIM:HEAD_DIM + IDX_HEADS, :] * (IDX_HEADS ** -0.5)
    tm = tail.shape[0]
    row = pl.program_id(0) * tm + lax.broadcasted_iota(I32, (tm, 1), 0)
    pos = _pos_lanes(pos0 + (row & (period - 1)), lane)
    for pair in range(nkv // LANES):
        for half in range(2):
            tile = slice((2 * pair + half) * LANES, (2 * pair + half + 1) * LANES)
            vp = v[:, pair * LANES:(pair + 1) * LANES]
            kp = k[:, pair * LANES:(pair + 1) * LANES]
            if half:
                vp, kp = pltpu.roll(vp, HEAD_DIM, 1), pltpu.roll(kp, HEAD_DIM, 1)
            vt_ref[0, tile, :] = jnp.where(lane < HEAD_DIM, vp, 1.0).T.astype(BF16)
            ka_ref[:, tile] = jnp.where(lane < HEAD_DIM, kp, pos).astype(BF16)


def _proj_b(x, w_pad, nq, nkv, nqi, tm, pos0, period):
    m, d = x.shape
    assert period & (period - 1) == 0
    rows = lambda c: pl.BlockSpec((tm, c), lambda i: (i, 0))
    outs = [(nq, BF16), (nqi, BF16), (2 * nkv, BF16), (nkv, F32), (nkv, F32), (LANES, F32), (LANES, BF16)]
    return pl.pallas_call(
        functools.partial(_proj_b_kernel, nq=nq, nkv=nkv, nqi=nqi, pos0=pos0, period=period),
        grid=(m // tm,),
        in_specs=[rows(d), _resident(w_pad.shape)],
        out_specs=[rows(c) for c, _ in outs] + [pl.BlockSpec((IDX_HEADS, tm), lambda i: (0, i)),
                                                 pl.BlockSpec((1, 2 * nkv, tm), lambda i: (i, 0, 0))],
        out_shape=[jax.ShapeDtypeStruct((m, c), dt) for c, dt in outs]
        + [jax.ShapeDtypeStruct((IDX_HEADS, m), F32), jax.ShapeDtypeStruct((m // tm, 2 * nkv, tm), BF16)],
        compiler_params=_params(1),
        name="proj_b",
    )(x, w_pad)


def _sparse_kernel(x_ref, q_ref, qi_ref, wit_ref, ka_ref, vt_ref, kd_ref, wo_ref, g_ref, b_ref, o_ref,
                   key_scr, bias_scr, qt_scr, acc_scr, m_scr, ot_scr,
                   *, tq, kb, pos0, n_keys, n_sel, n_heads, group, slopes, alpha):
    t = pl.program_id(1)
    q0 = pos0 + t * tq
    qpos_row = q0 + lax.broadcasted_iota(I32, (1, tq), 1)
    lim_row = jnp.minimum((lax.shift_right_logical(qpos_row, 6) + 1) * CHUNK, n_keys)
    kmax = jnp.minimum((lax.shift_right_logical(q0 + tq - 1, 6) + 1) * CHUNK, n_keys)
    nkb = lax.shift_right_logical(kmax + kb - 1, int(math.log2(kb)))
    lane = lax.broadcasted_iota(I32, (tq, LANES), 1)
    lo_half = lane < HEAD_DIM

    def key_block(kbi):
        return pl.multiple_of(kbi * kb, kb)

    def key_index(off):
        return off + lax.broadcasted_iota(I32, (kb, tq), 0)

    qis = []
    for p in range(IDX_HEADS // 2):
        q2 = qi_ref[0, :, p * LANES:(p + 1) * LANES]
        qis.append(jnp.where(lo_half, q2, jnp.zeros_like(q2)))
        qis.append(jnp.where(lo_half, jnp.zeros_like(q2), q2))
    wis = wit_ref[...]

    def score_block(kbi, carry):
        off = key_block(kbi)
        kid = kd_ref[0, pl.ds(off, kb), :]
        acc = jnp.zeros((kb, tq), F32)
        for h in range(IDX_HEADS):
            acc = acc + jnp.maximum(_dot_nt(kid, qis[h]), 0.0) * wis[h:h + 1, :]
        acc = jnp.where(key_index(off) < lim_row, acc, -jnp.inf)
        bits = lax.bitcast_convert_type(acc, I32)
        key_scr[pl.ds(off, kb), :] = bits ^ (lax.shift_right_arithmetic(bits, 31) & 0x7FFFFFFF)
        return carry

    lax.fori_loop(0, nkb, score_block, 0)

    def count(pred):
        def body(kbi, c8):
            hit = pred(key_scr[pl.ds(key_block(kbi), kb), :])
            return c8 + jnp.sum(jnp.where(hit, 1, 0).reshape(kb // 8, 8, tq), axis=0)
        c8 = lax.fori_loop(0, nkb, body, jnp.zeros((8, tq), I32))
        return jnp.sum(c8, axis=0, keepdims=True)

    def bisect(i, thr):
        cand = thr + lax.shift_left(jnp.int32(1), 31 - i)
        return jnp.where(count(lambda key: key >= cand) >= n_sel, cand, thr)

    thr = lax.fori_loop(0, 32, bisect, jnp.full((1, tq), INT_MIN, I32))
    n_gt = count(lambda key: key > thr)
    n_eq = count(lambda key: key == thr)
    need = n_sel - n_gt
    tie = jnp.max(jnp.where((n_eq > need) & (thr > NEG_INF_KEY), 1, 0)) > 0

    @pl.when(jnp.logical_not(tie))
    def _():
        def body(kbi, carry):
            off = key_block(kbi)
            sel = (key_scr[pl.ds(off, kb), :] >= thr) & (key_index(off) < lim_row)
            bias_scr[pl.ds(off, kb), :] = jnp.where(sel, 0.0, -MASK_BIAS)
            return carry
        lax.fori_loop(0, nkb, body, 0)

    @pl.when(tie)
    def _():
        tri = jnp.where(lax.broadcasted_iota(I32, (kb, kb), 0) > lax.broadcasted_iota(I32, (kb, kb), 1),
                        1.0, 0.0).astype(BF16)
        need_f = need.astype(F32)

        def body(kbi, seen):
            off = key_block(kbi)
            key = key_scr[pl.ds(off, kb), :]
            eq = jnp.where(key == thr, 1.0, 0.0)
            earlier = _dot(tri, eq.astype(BF16)) + seen
            sel = ((key > thr) | ((key == thr) & (earlier < need_f))) & (key_index(off) < lim_row)
            bias_scr[pl.ds(off, kb), :] = jnp.where(sel, 0.0, -MASK_BIAS)
            return seen + jnp.sum(eq, axis=0, keepdims=True)
        lax.fori_loop(0, nkb, body, jnp.zeros((1, tq), F32))

    n_kv = n_heads // group
    for h in range(n_heads):
        q2 = q_ref[0, :, (h // 2) * LANES:(h // 2 + 1) * LANES].astype(F32)
        if h % 2:
            q2 = pltpu.roll(q2, HEAD_DIM, 1)
        sl = jnp.zeros((1, LANES), F32)
        for i, piece in enumerate(_bf16_pieces(slopes[h] * LOG2E) * 2):
            sl = jnp.where(lane[:1] == POS_LANE0 + i, piece, sl)
        qt_scr[h // group, :, (h % group) * tq:(h % group + 1) * tq] = jnp.where(lo_half, q2, sl).T.astype(BF16)
    m_scr[...] = jnp.full(m_scr.shape, -jnp.inf, F32)
    acc_scr[...] = jnp.zeros(acc_scr.shape, F32)

    def attend(kbi, last):
        off = key_block(kbi)
        bias = bias_scr[pl.ds(off, kb), :]
        bias = jnp.concatenate([bias] * group, axis=1)
        if last:
            ahead = jnp.maximum(key_index(off) - qpos_row, 0).astype(F32)
        for g in range(n_kv):
            s = _dot(ka_ref[0, pl.ds(off, kb), g * LANES:(g + 1) * LANES], qt_scr[g]) + bias
            if last:
                s = s - jnp.concatenate([(2.0 * slopes[g * group + j] * LOG2E) * ahead for j in range(group)], axis=1)
            m_old = m_scr[g]
            m_new = jnp.maximum(m_old, jnp.max(s, axis=0, keepdims=True))
            p = jnp.exp2(s - m_new).astype(BF16)
            acc_scr[g] = acc_scr[g] * jnp.exp2(m_old - m_new) + _dot(vt_ref[0, kbi, g * LANES:(g + 1) * LANES, :], p)
            m_scr[g] = m_new

    def attend_body(kbi, carry):
        attend(kbi, False)
        return carry

    lax.fori_loop(0, nkb - 1, attend_body, 0)
    attend(nkb - 1, True)

    for g in range(n_kv):
        acc = acc_scr[g]
        o_t = acc[:HEAD_DIM] / acc[HEAD_DIM:HEAD_DIM + 1]
        for j in range(group):
            h = g * group + j
            ot_scr[h * HEAD_DIM:(h + 1) * HEAD_DIM, :] = o_t[:, j * tq:(j + 1) * tq].astype(BF16)
    y = lax.dot_general(ot_scr[...], wo_ref[...], (((0,), (0,)), ((), ())), preferred_element_type=F32)
    o_ref[0] = _layer_norm(alpha * x_ref[0] + y, g_ref[...], b_ref[...])


def _sparse_attn(x, q, qi, wit, k, vt, kd, wo, g, b, *, tq, kb, pos0, n_keys, alpha):
    bsz, s, d = x.shape
    n_t = s // tq
    l_pad = k.shape[1]
    n_heads = q.shape[2] // HEAD_DIM
    n_kv = k.shape[2] // LANES
    group = n_heads // n_kv
    assert kb % tq == 0 and pos0 % tq == 0 and l_pad % kb == 0 and vt.shape == (bsz, l_pad // kb, k.shape[2], kb)
    slopes = tuple(2.0 ** (-8.0 * (h + 1) / n_heads) for h in range(n_heads))
    qrow = lambda c: pl.BlockSpec((1, tq, c), lambda bi, t: (bi, t, 0))
    keys = lambda c: pl.BlockSpec((1, l_pad, c), lambda bi, t: (bi, 0, 0))
    const2 = lambda bi, t: (0, 0)
    kern = functools.partial(_sparse_kernel, tq=tq, kb=kb, pos0=pos0, n_keys=n_keys, n_sel=min(TOPK_MAX, n_keys // 4),
                             n_heads=n_heads, group=group, slopes=slopes, alpha=alpha)
    return pl.pallas_call(
        kern,
        grid=(bsz, n_t),
        in_specs=[qrow(d), qrow(q.shape[2]), qrow(qi.shape[2]),
                  pl.BlockSpec((IDX_HEADS, tq), lambda bi, t: (0, bi * n_t + t)),
                  keys(k.shape[2]), pl.BlockSpec((1,) + vt.shape[1:], lambda bi, t: (bi, 0, 0, 0)), keys(kd.shape[2]),
                  _resident(wo.shape), pl.BlockSpec((1, d), const2), pl.BlockSpec((1, d), const2)],
        out_specs=qrow(d),
        out_shape=jax.ShapeDtypeStruct((bsz, s, d), F32),
        scratch_shapes=[pltpu.VMEM((l_pad, tq), I32), pltpu.VMEM((l_pad, tq), F32),
                        pltpu.VMEM((n_kv, LANES, group * tq), BF16), pltpu.VMEM((n_kv, LANES, group * tq), F32),
                        pltpu.VMEM((n_kv, 1, group * tq), F32), pltpu.VMEM((q.shape[2], tq), BF16)],
        compiler_params=_params(2),
        name="sparse_attn",
    )(x, q, qi, wit, k, vt, kd, wo, g, b)


def _pad_rows(a, n):
    return jnp.pad(a, ((0, 0), (0, n - a.shape[1]), (0, 0)))


def _mixer_a(xp, xs, cache_k, cache_v, w_in, rel_bias, w_out, g, b, alpha):
    bsz, s, d = xp.shape
    dbs, t, _ = xs.shape
    width = w_out.shape[0]
    w_in = w_in.astype(BF16)
    w_out = w_out.astype(BF16)
    bias_p, bias_s = _rel_bias_tables(rel_bias, t)
    keep = min(A_PAST, s)
    q, k, vt, kf, vf = _proj_a(xp, w_in, keep)
    yp = _attn_a_prompt(xp, q, k, vt, bias_p, w_out, g, b, alpha)
    qs, ks, _, ksf, vsf = _proj_a(xs.reshape(1, dbs * t, d), w_in, dbs * t)
    n_cache = cache_k.shape[1]
    kwin = _pad_rows(jnp.concatenate([cache_k.reshape(dbs, n_cache, width).astype(BF16), ks.reshape(dbs, t, width)], 1), A_WIN)
    vwin = _pad_rows(jnp.concatenate([cache_v.reshape(dbs, n_cache, width), vsf.reshape(dbs, t, width)], 1).astype(BF16), A_WIN)
    ys = _attn_a_sample(_pad_rows(xs, A_SUB), _pad_rows(qs.reshape(dbs, t, width), A_SUB), kwin, vwin.transpose(0, 2, 1),
                        bias_s, w_out, g, b, alpha)[:, :t]
    heads = width // HEAD_DIM
    return (yp, ys, kf.reshape(bsz, keep, heads, HEAD_DIM), vf.reshape(bsz, keep, heads, HEAD_DIM),
            ksf.reshape(dbs, t, heads, HEAD_DIM), vsf.reshape(dbs, t, heads, HEAD_DIM))


def _mixer_b(xp, xs, cache_k, cache_v, cache_idx, w_in, w_out, g, b, alpha):
    bsz, s, d = xp.shape
    dbs, t, _ = xs.shape
    nq = w_out.shape[0]
    nkv = cache_k.shape[2] * cache_k.shape[3]
    nqi = IDX_HEADS * HEAD_DIM
    kvh = nkv // HEAD_DIM
    past = cache_k.shape[1]
    w_pad = jnp.pad(w_in, ((0, 0), (0, nq + 2 * nkv + nqi + LANES - w_in.shape[1]))).astype(BF16)
    w_out = w_out.astype(BF16)
    kb = 512

    q, qi, k, kf, vf, kiwi, kd, wit, vt = _proj_b(xp.reshape(bsz * s, d), w_pad, nq, nkv, nqi, kb, 0, s)
    r3 = lambda a: a.reshape(bsz, s, a.shape[1])
    yp = _sparse_attn(xp, r3(q), r3(qi), wit, r3(k), vt.reshape(bsz, s // kb, 2 * nkv, kb), r3(kd), w_out, g, b,
                      tq=256, kb=kb, pos0=0, n_keys=s, alpha=alpha)

    qs, qis, ks, ksf, vsf, kiwis, kds, wits, _ = _proj_b(xs.reshape(dbs * t, d), w_pad, nq, nkv, nqi, dbs * t, past, t)
    tq_s = LANES
    n_keys = past + t
    l_pad = -(-n_keys // kb) * kb
    s3 = lambda a: a.reshape(dbs, t, a.shape[1])
    v_all = jnp.concatenate([cache_v, vsf.reshape(dbs, t, kvh, HEAD_DIM)], axis=1).astype(BF16)
    v_all = jnp.concatenate([v_all, jnp.ones_like(v_all)], axis=-1).reshape(dbs, n_keys, 2 * nkv)
    keys_vt = _pad_rows(v_all, l_pad).reshape(dbs, l_pad // kb, kb, 2 * nkv).transpose(0, 1, 3, 2)
    kd_cache = jnp.concatenate([cache_idx, cache_idx], axis=-1).astype(BF16)
    frame = jnp.arange(past, dtype=I32)[:, None]
    pos = _pos_lanes(frame, HEAD_DIM + jnp.arange(HEAD_DIM, dtype=I32)[None, :]).astype(BF16)
    pos = jnp.broadcast_to(pos[None, :, None, :], (dbs, past, kvh, HEAD_DIM))
    k_cache = jnp.concatenate([cache_k.astype(BF16), pos], axis=-1).reshape(dbs, past, 2 * nkv)
    keys_k = _pad_rows(jnp.concatenate([k_cache, s3(ks)], 1), l_pad)
    keys_kd = _pad_rows(jnp.concatenate([kd_cache, s3(kds)], 1), l_pad)
    wit_pad = jnp.pad(wits.reshape(IDX_HEADS, dbs, t), ((0, 0), (0, 0), (0, tq_s - t))).reshape(IDX_HEADS, dbs * tq_s)
    ys = _sparse_attn(_pad_rows(xs, tq_s), _pad_rows(s3(qs), tq_s), _pad_rows(s3(qis), tq_s), wit_pad,
                      keys_k, keys_vt, keys_kd, w_out, g, b,
                      tq=tq_s, kb=kb, pos0=past, n_keys=n_keys, alpha=alpha)[:, :t]
    return (yp, ys,
            kf.reshape(bsz, s, kvh, HEAD_DIM), vf.reshape(bsz, s, kvh, HEAD_DIM), kiwi[:, :HEAD_DIM].reshape(bsz, s, HEAD_DIM),
            ksf.reshape(dbs, t, kvh, HEAD_DIM), vsf.reshape(dbs, t, kvh, HEAD_DIM), kiwis[:, :HEAD_DIM].reshape(dbs, t, HEAD_DIM))


def kernel(x_prompt, x_sample, cache_a_k, cache_a_v, cache_b_k, cache_b_v, cache_b_idx, ln_g, ln_b, ffn_w_gate, ffn_w_up, ffn_w_down, a_w_in, a_rel_bias, a_w_out, b_w_in, b_w_out):
    depth = ln_g.shape[0]
    alpha = (2.0 * depth) ** 0.25
    bsz, s, d = x_prompt.shape
    dbs, t, _ = x_sample.shape
    xp, xs = x_prompt, x_sample
    a_out, b_out = [], []

    def ffn(x, layer, i, tm):
        shp = x.shape
        y = _ffn_block(x.reshape(-1, d), ffn_w_gate[layer, i].astype(BF16), ffn_w_up[layer, i].astype(BF16),
                       ffn_w_down[layer, i].astype(BF16), ln_g[layer, 2 * i][None], ln_b[layer, 2 * i][None], alpha, tm)
        return y.reshape(shp)

    for layer in range(depth):
        j = layer // 2
        xp = ffn(xp, layer, 0, 512)
        xs = ffn(xs, layer, 0, dbs * t)
        g, b = ln_g[layer, 1][None], ln_b[layer, 1][None]
        if layer % 2 == 0:
            xp, xs, *rest = _mixer_a(xp, xs, cache_a_k[j], cache_a_v[j], a_w_in[j], a_rel_bias[j], a_w_out[j], g, b, alpha)
            a_out.append(rest)
        else:
            xp, xs, *rest = _mixer_b(xp, xs, cache_b_k[j], cache_b_v[j], cache_b_idx[j], b_w_in[j], b_w_out[j], g, b, alpha)
            b_out.append(rest)
        xp = ffn(xp, layer, 1, 512)
        xs = ffn(xs, layer, 1, dbs * t)

    stack = lambda outs, i: jnp.stack([o[i] for o in outs], 0)
    return (xp, xs,
            stack(a_out, 0), stack(a_out, 1), stack(a_out, 2), stack(a_out, 3),
            stack(b_out, 0), stack(b_out, 1), stack(b_out, 2), stack(b_out, 3), stack(b_out, 4), stack(b_out, 5))
```

```python
import functools
import math
import struct

import jax
import jax.numpy as jnp
from jax import lax
from jax.experimental import pallas as pl
from jax.experimental.pallas import tpu as pltpu

F32 = jnp.float32
BF16 = jnp.bfloat16
I32 = jnp.int32

CHUNK = 64
A_PAST = 512
REL_CLIP = 128
HEAD_DIM = 64
LANES = 128
IDX_HEADS = 8
TOPK_MAX = 256
LN_EPS = 1e-5
A_SUB = 4 * CHUNK
A_WIN = A_PAST + A_SUB
VMEM_LIMIT = 52 * 1024 * 1024

LOG2E = 1.4426950408889634
POS_LANE0 = HEAD_DIM
MASK_BIAS = 1e30
ONES_ROWS = 16
COUNT_ROWS = 16

NT_DIMS = (((1,), (1,)), ((), ()))
INT_MIN = -2147483648
NEG_INF_KEY = -2139095041


def _params(n_grid, flags=None):
    return pltpu.CompilerParams(dimension_semantics=("arbitrary",) * n_grid,
                                vmem_limit_bytes=VMEM_LIMIT, flags=flags)


def _resident(shape):
    zeros = (0,) * len(shape)
    return pl.BlockSpec(shape, lambda *_: zeros, pipeline_mode=pl.Buffered(1))


def _layer_norm(z, g, b):
    mu = jnp.mean(z, axis=-1, keepdims=True)
    d = z - mu
    var = jnp.mean(d * d, axis=-1, keepdims=True)
    return d * lax.rsqrt(var + LN_EPS) * g + b


def _dot(a, b):
    return jnp.dot(a, b, preferred_element_type=F32)


def _dot_nt(a, b):
    return lax.dot_general(a, b, NT_DIMS, preferred_element_type=F32)


def _ffn_kernel(x_ref, wg_ref, wu_ref, wd_ref, g_ref, b_ref, o_ref, h_scr, *, n_chunks, alpha):
    x = x_ref[...]
    xb = x.astype(BF16)
    cf = wg_ref.shape[1] // n_chunks
    for c in range(n_chunks):
        sl = slice(c * cf, (c + 1) * cf)
        gate = _dot(xb, wg_ref[:, sl])
        up = _dot(xb, wu_ref[:, sl])
        h_scr[:, sl] = (gate * (1.0 / (1.0 + jnp.exp(-gate))) * up).astype(BF16)
    y = _dot(h_scr[...], wd_ref[...])
    o_ref[...] = _layer_norm(alpha * x + 0.5 * y, g_ref[...], b_ref[...])


def _ffn_block(x, wg, wu, wd, g, b, alpha, tm):
    m, d = x.shape
    f = wg.shape[1]
    const = lambda i: (0, 0)
    return pl.pallas_call(
        functools.partial(_ffn_kernel, n_chunks=4, alpha=alpha),
        grid=(m // tm,),
        in_specs=[pl.BlockSpec((tm, d), lambda i: (i, 0)),
                  _resident((d, f)), _resident((d, f)), _resident((f, d)),
                  pl.BlockSpec((1, d), const), pl.BlockSpec((1, d), const)],
        out_specs=pl.BlockSpec((tm, d), lambda i: (i, 0)),
        out_shape=jax.ShapeDtypeStruct((m, d), F32),
        scratch_shapes=[pltpu.VMEM((tm, f), BF16)],
        compiler_params=_params(1),
        name="ffn_ln",
    )(x, wg, wu, wd, g, b)


def _proj_a_kernel(x_ref, w_ref, q_ref, k_ref, vt_ref, kf_ref, vf_ref, *, width):
    h = _dot(x_ref[0].astype(BF16), w_ref[...])
    k = h[:, width:2 * width]
    v = h[:, 2 * width:]
    q_ref[0] = (h[:, :width] * (HEAD_DIM ** -0.5 * LOG2E)).astype(BF16)
    k_ref[0] = k.astype(BF16)
    vt_ref[0] = v.T.astype(BF16)

    @pl.when(pl.program_id(1) == pl.num_programs(1) - 1)
    def _():
        kf_ref[0] = k
        vf_ref[0] = v


def _proj_a(x, w, tm):
    bsz, s, d = x.shape
    width = w.shape[1] // 3
    row = pl.BlockSpec((1, tm, width), lambda b, t: (b, t, 0))
    last = pl.BlockSpec((1, tm, width), lambda b, t: (b, 0, 0))
    return pl.pallas_call(
        functools.partial(_proj_a_kernel, width=width),
        grid=(bsz, s // tm),
        in_specs=[pl.BlockSpec((1, tm, d), lambda b, t: (b, t, 0)), _resident(w.shape)],
        out_specs=[row, row, pl.BlockSpec((1, width, tm), lambda b, t: (b, 0, t)), last, last],
        out_shape=[jax.ShapeDtypeStruct((bsz, s, width), BF16)] * 2 + [jax.ShapeDtypeStruct((bsz, width, s), BF16)]
        + [jax.ShapeDtypeStruct((bsz, tm, width), F32)] * 2,
        compiler_params=_params(2),
        name="proj_a",
    )(x, w)


def _attn_a_kernel(*refs, n_sub, prompt, alpha):
    if prompt:
        (x_ref, q_ref, kp_ref, kc_ref, vtp_ref, vtc_ref, bias_ref, wo_ref, g_ref, b_ref,
         o_ref, kwin, vtwin, ot_scr) = refs
        kwin[0:A_PAST] = kp_ref[0]
        kwin[A_PAST:2 * A_PAST] = kc_ref[0]
        vtwin[:, 0:A_PAST] = vtp_ref[0]
        vtwin[:, A_PAST:2 * A_PAST] = vtc_ref[0]
        k_at = lambda r0, c0: kwin[r0:r0 + A_WIN, c0:c0 + LANES]
        vt_at = lambda r0, c0: vtwin[c0:c0 + LANES, r0:r0 + A_WIN]
    else:
        x_ref, q_ref, k_ref, vt_ref, bias_ref, wo_ref, g_ref, b_ref, o_ref, ot_scr = refs
        k_at = lambda r0, c0: k_ref[0, r0:r0 + A_WIN, c0:c0 + LANES]
        vt_at = lambda r0, c0: vt_ref[0, c0:c0 + LANES, r0:r0 + A_WIN]
    lane = lax.broadcasted_iota(I32, (A_SUB, LANES), 1)
    key_row = lax.broadcasted_iota(I32, (A_WIN, 2 * A_SUB), 0)
    out_row = lax.broadcasted_iota(I32, (LANES, A_SUB), 0)
    ones_rows = jnp.ones((ONES_ROWS, A_WIN), BF16)

    def attend(first_valid):
        for j in range(n_sub):
            r0 = j * A_SUB
            for p in range(q_ref.shape[2] // LANES):
                c0 = p * LANES
                q2 = q_ref[0, r0:r0 + A_SUB, c0:c0 + LANES].astype(F32)
                qt = jnp.concatenate([jnp.where(lane < HEAD_DIM, q2, 0.0).T, jnp.where(lane < HEAD_DIM, 0.0, q2).T], axis=1)
                s = _dot(k_at(r0, c0), qt.astype(BF16)) + bias_ref[p]
                if first_valid > r0:
                    s = jnp.where(key_row >= first_valid - r0, s, -jnp.inf)
                e = jnp.exp2((s - jnp.max(s, axis=0, keepdims=True)).astype(BF16))
                o = _dot(jnp.concatenate([vt_at(r0, c0), ones_rows], axis=0), e)
                o = o[:LANES] / o[LANES:LANES + 1]
                ot_scr[c0:c0 + LANES, r0:r0 + A_SUB] = jnp.where(out_row < HEAD_DIM, o[:, :A_SUB], o[:, A_SUB:]).astype(BF16)

    if prompt:
        pl.when(pl.program_id(1) == 0)(lambda: attend(A_PAST))
        pl.when(pl.program_id(1) > 0)(lambda: attend(0))
    else:
        attend(0)

    y = lax.dot_general(ot_scr[...], wo_ref[...], (((0,), (0,)), ((), ())), preferred_element_type=F32)
    o_ref[0] = _layer_norm(alpha * x_ref[0] + y, g_ref[...], b_ref[...])


def _attn_a_prompt(x, q, k, vt, bias, wo, g, b, alpha):
    bsz, s, d = x.shape
    width = q.shape[2]
    tq = A_PAST
    cur = lambda bi, t: (bi, t, 0)
    prev = lambda bi, t: (bi, jnp.maximum(t - 1, 0), 0)
    cur_t = lambda bi, t: (bi, 0, t)
    prev_t = lambda bi, t: (bi, 0, jnp.maximum(t - 1, 0))
    const2 = lambda bi, t: (0, 0)
    return pl.pallas_call(
        functools.partial(_attn_a_kernel, n_sub=tq // A_SUB, prompt=True, alpha=alpha),
        grid=(bsz, s // tq),
        in_specs=[pl.BlockSpec((1, tq, d), cur), pl.BlockSpec((1, tq, width), cur),
                  pl.BlockSpec((1, tq, width), prev), pl.BlockSpec((1, tq, width), cur),
                  pl.BlockSpec((1, width, tq), prev_t), pl.BlockSpec((1, width, tq), cur_t),
                  _resident(bias.shape), _resident(wo.shape),
                  pl.BlockSpec((1, d), const2), pl.BlockSpec((1, d), const2)],
        out_specs=pl.BlockSpec((1, tq, d), cur),
        out_shape=jax.ShapeDtypeStruct((bsz, s, d), F32),
        scratch_shapes=[pltpu.VMEM((2 * tq, width), BF16), pltpu.VMEM((width, 2 * tq), BF16),
                        pltpu.VMEM((width, tq), BF16)],
        compiler_params=_params(2),
        name="attn_a_prompt",
    )(x, q, k, k, vt, vt, bias, wo, g, b)


def _attn_a_sample(x, q, kwin, vtwin, bias, wo, g, b, alpha):
    bsz, _, d = x.shape
    width = q.shape[2]
    blk = lambda n, c: pl.BlockSpec((1, n, c), lambda bi: (bi, 0, 0))
    const2 = lambda bi: (0, 0)
    return pl.pallas_call(
        functools.partial(_attn_a_kernel, n_sub=1, prompt=False, alpha=alpha),
        grid=(bsz,),
        in_specs=[blk(A_SUB, d), blk(A_SUB, width), blk(A_WIN, width), blk(width, A_WIN),
                  _resident(bias.shape), _resident(wo.shape),
                  pl.BlockSpec((1, d), const2), pl.BlockSpec((1, d), const2)],
        out_specs=blk(A_SUB, d),
        out_shape=jax.ShapeDtypeStruct((bsz, A_SUB, d), F32),
        scratch_shapes=[pltpu.VMEM((width, A_SUB), BF16)],
        compiler_params=_params(1),
        name="attn_a_sample",
    )(x, q, kwin, vtwin, bias, wo, g, b)


def _rel_bias_tables(rel_bias, n_sample):
    r = jnp.arange(A_SUB)[:, None]
    c = jnp.arange(A_WIN)[None, :]
    period = A_SUB + A_WIN
    diff = jnp.arange(period)
    diff = jnp.where(diff < A_WIN, diff, diff - period)
    line = rel_bias[:, jnp.clip(A_PAST - diff, -REL_CLIP, REL_CLIP) + REL_CLIP].astype(F32) * LOG2E
    table = jnp.tile(line, (1, A_SUB))[:, :A_SUB * (period - 1)].reshape(-1, A_SUB, period - 1)[:, :, :A_WIN]
    lo = (r // CHUNK) * CHUNK
    band = (c >= lo) & (c < lo + A_PAST + CHUNK)
    prompt = jnp.where(band[None], table, -jnp.inf)
    live = (c < A_PAST + n_sample)
    sample = jnp.where(live[None], jnp.where((r < n_sample)[None], table, 0.0), -jnp.inf)

    def pair_layout(tab):
        h = tab.shape[0]
        return tab.transpose(0, 2, 1).reshape(h // 2, 2, A_WIN, A_SUB).transpose(0, 2, 1, 3).reshape(h // 2, A_WIN, 2 * A_SUB)

    return pair_layout(prompt), pair_layout(sample)


def _round_to_bf16(x):
    bits = struct.unpack("<I", struct.pack("<f", x))[0]
    bits = (bits + 0x7FFF + ((bits >> 16) & 1)) & 0xFFFF0000
    return struct.unpack("<f", struct.pack("<I", bits))[0]


def _bf16_pieces(x):
    p1 = _round_to_bf16(x)
    p2 = _round_to_bf16(x - p1)
    return (p1, p2, _round_to_bf16(x - p1 - p2))


def _pos_lanes(pos, lane):
    hi = (lax.shift_right_logical(pos, 6) * CHUNK).astype(F32)
    lo = (pos & (CHUNK - 1)).astype(F32)
    return jnp.where(lane < POS_LANE0 + 3, hi, jnp.where(lane < POS_LANE0 + 6, lo, 0.0))


def _proj_b_kernel(x_ref, w_ref, q_ref, qi_ref, ka_ref, kf_ref, vf_ref, kiwi_ref, kd_ref, wit_ref, vt_ref,
                   *, nq, nkv, nqi, pos0, period):
    h = _dot(x_ref[...].astype(BF16), w_ref[...])
    q_ref[...] = (h[:, :nq] * (HEAD_DIM ** -0.5 * LOG2E)).astype(BF16)
    k = h[:, nq:nq + nkv]
    v = h[:, nq + nkv:nq + 2 * nkv]
    kf_ref[...] = k
    vf_ref[...] = v
    o0 = nq + 2 * nkv
    qi_ref[...] = (h[:, o0:o0 + nqi] * (HEAD_DIM ** -0.5)).astype(BF16)
    tail = h[:, o0 + nqi:o0 + nqi + LANES]
    kiwi_ref[...] = tail
    lane = lax.broadcasted_iota(I32, tail.shape, 1)
    kd_ref[...] = jnp.where(lane < HEAD_DIM, tail, pltpu.roll(tail, HEAD_DIM, 1)).astype(BF16)
    wit_ref[...] = tail.T[HEAD_DIM:HEAD_DIM + IDX_HEADS, :] * (IDX_HEADS ** -0.5)
    tm = tail.shape[0]
    row = pl.program_id(0) * tm + lax.broadcasted_iota(I32, (tm, 1), 0)
    pos = _pos_lanes(pos0 + (row & (period - 1)), lane)
    for pair in range(nkv // LANES):
        for half in range(2):
            tile = slice((2 * pair + half) * LANES, (2 * pair + half + 1) * LANES)
            vp = v[:, pair * LANES:(pair + 1) * LANES]
            kp = k[:, pair * LANES:(pair + 1) * LANES]
            if half:
                vp, kp = pltpu.roll(vp, HEAD_DIM, 1), pltpu.roll(kp, HEAD_DIM, 1)
            vt_ref[0, tile, :] = jnp.where(lane < HEAD_DIM, vp, 1.0).T.astype(BF16)
            ka_ref[:, tile] = jnp.where(lane < HEAD_DIM, kp, pos).astype(BF16)


def _proj_b(x, w_pad, nq, nkv, nqi, tm, pos0, period):
    m, d = x.shape
    assert period & (period - 1) == 0
    rows = lambda c: pl.BlockSpec((tm, c), lambda i: (i, 0))
    outs = [(nq, BF16), (nqi, BF16), (2 * nkv, BF16), (nkv, F32), (nkv, F32), (LANES, F32), (LANES, BF16)]
    return pl.pallas_call(
        functools.partial(_proj_b_kernel, nq=nq, nkv=nkv, nqi=nqi, pos0=pos0, period=period),
        grid=(m // tm,),
        in_specs=[rows(d), _resident(w_pad.shape)],
        out_specs=[rows(c) for c, _ in outs] + [pl.BlockSpec((IDX_HEADS, tm), lambda i: (0, i)),
                                                 pl.BlockSpec((1, 2 * nkv, tm), lambda i: (i, 0, 0))],
        out_shape=[jax.ShapeDtypeStruct((m, c), dt) for c, dt in outs]
        + [jax.ShapeDtypeStruct((IDX_HEADS, m), F32), jax.ShapeDtypeStruct((m // tm, 2 * nkv, tm), BF16)],
        compiler_params=_params(1),
        name="proj_b",
    )(x, w_pad)


def _sparse_kernel(x_ref, q_ref, qi_ref, wit_ref, ka_ref, vt_ref, kd_ref, wo_ref, g_ref, b_ref, o_ref,
                   key_scr, bias_scr, qt_scr, acc_scr, m_scr, ot_scr,
                   *, tq, kb, pos0, n_keys, n_sel, n_heads, group, slopes, alpha):
    t = pl.program_id(1)
    q0 = pos0 + t * tq
    qpos_row = q0 + lax.broadcasted_iota(I32, (1, tq), 1)
    lim_row = jnp.minimum((lax.shift_right_logical(qpos_row, 6) + 1) * CHUNK, n_keys)
    kmax = jnp.minimum((lax.shift_right_logical(q0 + tq - 1, 6) + 1) * CHUNK, n_keys)
    nkb = lax.shift_right_logical(kmax + kb - 1, int(math.log2(kb)))
    lane = lax.broadcasted_iota(I32, (tq, LANES), 1)
    lo_half = lane < HEAD_DIM

    def key_block(kbi):
        return pl.multiple_of(kbi * kb, kb)

    def key_index(off):
        return off + lax.broadcasted_iota(I32, (kb, tq), 0)

    qis = []
    for p in range(IDX_HEADS // 2):
        q2 = qi_ref[0, :, p * LANES:(p + 1) * LANES]
        qis.append(jnp.where(lo_half, q2, jnp.zeros_like(q2)))
        qis.append(jnp.where(lo_half, jnp.zeros_like(q2), q2))
    wis = wit_ref[...]
    qi_all = jnp.concatenate(qis, axis=0)

    def score_block(kbi, carry):
        off = key_block(kbi)
        dots = _dot_nt(kd_ref[0, pl.ds(off, kb), :], qi_all)
        acc = jnp.zeros((kb, tq), F32)
        for h in range(IDX_HEADS):
            acc = acc + jnp.maximum(dots[:, h * tq:(h + 1) * tq], 0.0) * wis[h:h + 1, :]
        acc = jnp.where(key_index(off) < lim_row, acc, -jnp.inf)
        bits = lax.bitcast_convert_type(acc, I32)
        key_scr[pl.ds(off, kb), :] = bits ^ (lax.shift_right_arithmetic(bits, 31) & 0x7FFFFFFF)
        return carry

    lax.fori_loop(0, nkb, score_block, 0)

    @pl.when(nkb % 2 == 1)
    def _():
        key_scr[pl.ds(key_block(nkb), kb), :] = jnp.full((kb, tq), INT_MIN, I32)

    n_steps = lax.shift_right_logical(nkb + 1, 1)

    def count(pred):
        def body(i, part):
            hit = pred(key_scr[pl.ds(pl.multiple_of(i * 2 * kb, 2 * kb), 2 * kb), :])
            return part + jnp.sum(jnp.where(hit, 1, 0).reshape(-1, COUNT_ROWS, tq), axis=0)
        part = lax.fori_loop(0, n_steps, body, jnp.zeros((COUNT_ROWS, tq), I32))
        return jnp.sum(part, axis=0, keepdims=True)

    def bisect(i, thr):
        cand = thr + lax.shift_left(jnp.int32(1), 31 - i)
        return jnp.where(count(lambda key: key >= cand) >= n_sel, cand, thr)

    thr = lax.fori_loop(0, 32, bisect, jnp.full((1, tq), INT_MIN, I32))
    n_gt = count(lambda key: key > thr)
    n_eq = count(lambda key: key == thr)
    need = n_sel - n_gt
    tie = jnp.max(jnp.where((n_eq > need) & (thr > NEG_INF_KEY), 1, 0)) > 0

    @pl.when(jnp.logical_not(tie))
    def _():
        def body(kbi, carry):
            off = key_block(kbi)
            sel = (key_scr[pl.ds(off, kb), :] >= thr) & (key_index(off) < lim_row)
            bias_scr[pl.ds(off, kb), :] = jnp.where(sel, 0.0, -MASK_BIAS)
            return carry
        lax.fori_loop(0, nkb, body, 0)

    @pl.when(tie)
    def _():
        tri = jnp.where(lax.broadcasted_iota(I32, (kb, kb), 0) > lax.broadcasted_iota(I32, (kb, kb), 1),
                        1.0, 0.0).astype(BF16)
        need_f = need.astype(F32)

        def body(kbi, seen):
            off = key_block(kbi)
            key = key_scr[pl.ds(off, kb), :]
            eq = jnp.where(key == thr, 1.0, 0.0)
            earlier = _dot(tri, eq.astype(BF16)) + seen
            sel = ((key > thr) | ((key == thr) & (earlier < need_f))) & (key_index(off) < lim_row)
            bias_scr[pl.ds(off, kb), :] = jnp.where(sel, 0.0, -MASK_BIAS)
            return seen + jnp.sum(eq, axis=0, keepdims=True)
        lax.fori_loop(0, nkb, body, jnp.zeros((1, tq), F32))

    n_kv = n_heads // group
    for h in range(n_heads):
        q2 = q_ref[0, :, (h // 2) * LANES:(h // 2 + 1) * LANES].astype(F32)
        if h % 2:
            q2 = pltpu.roll(q2, HEAD_DIM, 1)
        sl = jnp.zeros((1, LANES), F32)
        for i, piece in enumerate(_bf16_pieces(slopes[h] * LOG2E) * 2):
            sl = jnp.where(lane[:1] == POS_LANE0 + i, piece, sl)
        qt_scr[h // group, :, (h % group) * tq:(h % group + 1) * tq] = jnp.where(lo_half, q2, sl).T.astype(BF16)
    m_scr[...] = jnp.full(m_scr.shape, -jnp.inf, F32)
    acc_scr[...] = jnp.zeros(acc_scr.shape, F32)

    def attend(kbi, last):
        off = key_block(kbi)
        bias = bias_scr[pl.ds(off, kb), :]
        bias = jnp.concatenate([bias] * group, axis=1)
        if last:
            ahead = jnp.maximum(key_index(off) - qpos_row, 0).astype(F32)
        for g in range(n_kv):
            s = _dot(ka_ref[0, pl.ds(off, kb), g * LANES:(g + 1) * LANES], qt_scr[g]) + bias
            if last:
                s = s - jnp.concatenate([(2.0 * slopes[g * group + j] * LOG2E) * ahead for j in range(group)], axis=1)
            m_old = m_scr[g]
            m_new = jnp.maximum(m_old, jnp.max(s, axis=0, keepdims=True))
            p = jnp.exp2((s - m_new).astype(BF16))
            acc_scr[g] = acc_scr[g] * jnp.exp2(m_old - m_new) + _dot(vt_ref[0, kbi, g * LANES:(g + 1) * LANES, :], p)
            m_scr[g] = m_new

    def attend_body(kbi, carry):
        attend(kbi, False)
        return carry

    lax.fori_loop(0, nkb - 1, attend_body, 0)
    attend(nkb - 1, True)

    for g in range(n_kv):
        acc = acc_scr[g]
        o_t = acc[:HEAD_DIM] / acc[HEAD_DIM:HEAD_DIM + 1]
        for j in range(group):
            h = g * group + j
            ot_scr[h * HEAD_DIM:(h + 1) * HEAD_DIM, :] = o_t[:, j * tq:(j + 1) * tq].astype(BF16)
    y = lax.dot_general(ot_scr[...], wo_ref[...], (((0,), (0,)), ((), ())), preferred_element_type=F32)
    o_ref[0] = _layer_norm(alpha * x_ref[0] + y, g_ref[...], b_ref[...])


def _sparse_attn(x, q, qi, wit, k, vt, kd, wo, g, b, *, tq, kb, pos0, n_keys, alpha):
    bsz, s, d = x.shape
    n_t = s // tq
    l_pad = k.shape[1]
    n_heads = q.shape[2] // HEAD_DIM
    n_kv = k.shape[2] // LANES
    group = n_heads // n_kv
    assert kb % tq == 0 and pos0 % tq == 0 and l_pad % kb == 0 and vt.shape == (bsz, l_pad // kb, k.shape[2], kb)
    slopes = tuple(2.0 ** (-8.0 * (h + 1) / n_heads) for h in range(n_heads))
    qrow = lambda c: pl.BlockSpec((1, tq, c), lambda bi, t: (bi, t, 0))
    keys = lambda c: pl.BlockSpec((1, l_pad, c), lambda bi, t: (bi, 0, 0))
    const2 = lambda bi, t: (0, 0)
    kern = functools.partial(_sparse_kernel, tq=tq, kb=kb, pos0=pos0, n_keys=n_keys, n_sel=min(TOPK_MAX, n_keys // 4),
                             n_heads=n_heads, group=group, slopes=slopes, alpha=alpha)
    return pl.pallas_call(
        kern,
        grid=(bsz, n_t),
        in_specs=[qrow(d), qrow(q.shape[2]), qrow(qi.shape[2]),
                  pl.BlockSpec((IDX_HEADS, tq), lambda bi, t: (0, bi * n_t + t)),
                  keys(k.shape[2]), pl.BlockSpec((1,) + vt.shape[1:], lambda bi, t: (bi, 0, 0, 0)), keys(kd.shape[2]),
                  _resident(wo.shape), pl.BlockSpec((1, d), const2), pl.BlockSpec((1, d), const2)],
        out_specs=qrow(d),
        out_shape=jax.ShapeDtypeStruct((bsz, s, d), F32),
        scratch_shapes=[pltpu.VMEM((-(-l_pad // (2 * kb)) * 2 * kb, tq), I32), pltpu.VMEM((l_pad, tq), F32),
                        pltpu.VMEM((n_kv, LANES, group * tq), BF16), pltpu.VMEM((n_kv, LANES, group * tq), F32),
                        pltpu.VMEM((n_kv, 1, group * tq), F32), pltpu.VMEM((q.shape[2], tq), BF16)],
        compiler_params=_params(2),
        name="sparse_attn",
    )(x, q, qi, wit, k, vt, kd, wo, g, b)


def _pad_rows(a, n):
    return jnp.pad(a, ((0, 0), (0, n - a.shape[1]), (0, 0)))


def _mixer_a(xp, xs, cache_k, cache_v, w_in, rel_bias, w_out, g, b, alpha):
    bsz, s, d = xp.shape
    dbs, t, _ = xs.shape
    width = w_out.shape[0]
    w_in = w_in.astype(BF16)
    w_out = w_out.astype(BF16)
    bias_p, bias_s = _rel_bias_tables(rel_bias, t)
    keep = min(A_PAST, s)
    q, k, vt, kf, vf = _proj_a(xp, w_in, keep)
    yp = _attn_a_prompt(xp, q, k, vt, bias_p, w_out, g, b, alpha)
    qs, ks, _, ksf, vsf = _proj_a(xs.reshape(1, dbs * t, d), w_in, dbs * t)
    n_cache = cache_k.shape[1]
    kwin = _pad_rows(jnp.concatenate([cache_k.reshape(dbs, n_cache, width).astype(BF16), ks.reshape(dbs, t, width)], 1), A_WIN)
    vwin = _pad_rows(jnp.concatenate([cache_v.reshape(dbs, n_cache, width), vsf.reshape(dbs, t, width)], 1).astype(BF16), A_WIN)
    ys = _attn_a_sample(_pad_rows(xs, A_SUB), _pad_rows(qs.reshape(dbs, t, width), A_SUB), kwin, vwin.transpose(0, 2, 1),
                        bias_s, w_out, g, b, alpha)[:, :t]
    heads = width // HEAD_DIM
    return (yp, ys, kf.reshape(bsz, keep, heads, HEAD_DIM), vf.reshape(bsz, keep, heads, HEAD_DIM),
            ksf.reshape(dbs, t, heads, HEAD_DIM), vsf.reshape(dbs, t, heads, HEAD_DIM))


def _mixer_b(xp, xs, cache_k, cache_v, cache_idx, w_in, w_out, g, b, alpha):
    bsz, s, d = xp.shape
    dbs, t, _ = xs.shape
    nq = w_out.shape[0]
    nkv = cache_k.shape[2] * cache_k.shape[3]
    nqi = IDX_HEADS * HEAD_DIM
    kvh = nkv // HEAD_DIM
    past = cache_k.shape[1]
    w_pad = jnp.pad(w_in, ((0, 0), (0, nq + 2 * nkv + nqi + LANES - w_in.shape[1]))).astype(BF16)
    w_out = w_out.astype(BF16)
    kb = 512

    q, qi, k, kf, vf, kiwi, kd, wit, vt = _proj_b(xp.reshape(bsz * s, d), w_pad, nq, nkv, nqi, kb, 0, s)
    r3 = lambda a: a.reshape(bsz, s, a.shape[1])
    yp = _sparse_attn(xp, r3(q), r3(qi), wit, r3(k), vt.reshape(bsz, s // kb, 2 * nkv, kb), r3(kd), w_out, g, b,
                      tq=256, kb=kb, pos0=0, n_keys=s, alpha=alpha)

    qs, qis, ks, ksf, vsf, kiwis, kds, wits, _ = _proj_b(xs.reshape(dbs * t, d), w_pad, nq, nkv, nqi, dbs * t, past, t)
    tq_s = LANES
    n_keys = past + t
    l_pad = -(-n_keys // kb) * kb
    s3 = lambda a: a.reshape(dbs, t, a.shape[1])
    v_all = jnp.concatenate([cache_v, vsf.reshape(dbs, t, kvh, HEAD_DIM)], axis=1).astype(BF16)
    v_all = jnp.concatenate([v_all, jnp.ones_like(v_all)], axis=-1).reshape(dbs, n_keys, 2 * nkv)
    keys_vt = _pad_rows(v_all, l_pad).reshape(dbs, l_pad // kb, kb, 2 * nkv).transpose(0, 1, 3, 2)
    kd_cache = jnp.concatenate([cache_idx, cache_idx], axis=-1).astype(BF16)
    frame = jnp.arange(past, dtype=I32)[:, None]
    pos = _pos_lanes(frame, HEAD_DIM + jnp.arange(HEAD_DIM, dtype=I32)[None, :]).astype(BF16)
    pos = jnp.broadcast_to(pos[None, :, None, :], (dbs, past, kvh, HEAD_DIM))
    k_cache = jnp.concatenate([cache_k.astype(BF16), pos], axis=-1).reshape(dbs, past, 2 * nkv)
    keys_k = _pad_rows(jnp.concatenate([k_cache, s3(ks)], 1), l_pad)
    keys_kd = _pad_rows(jnp.concatenate([kd_cache, s3(kds)], 1), l_pad)
    wit_pad = jnp.pad(wits.reshape(IDX_HEADS, dbs, t), ((0, 0), (0, 0), (0, tq_s - t))).reshape(IDX_HEADS, dbs * tq_s)
    ys = _sparse_attn(_pad_rows(xs, tq_s), _pad_rows(s3(qs), tq_s), _pad_rows(s3(qis), tq_s), wit_pad,
                      keys_k, keys_vt, keys_kd, w_out, g, b,
                      tq=tq_s, kb=kb, pos0=past, n_keys=n_keys, alpha=alpha)[:, :t]
    return (yp, ys,
            kf.reshape(bsz, s, kvh, HEAD_DIM), vf.reshape(bsz, s, kvh, HEAD_DIM), kiwi[:, :HEAD_DIM].reshape(bsz, s, HEAD_DIM),
            ksf.reshape(dbs, t, kvh, HEAD_DIM), vsf.reshape(dbs, t, kvh, HEAD_DIM), kiwis[:, :HEAD_DIM].reshape(dbs, t, HEAD_DIM))


def kernel(x_prompt, x_sample, cache_a_k, cache_a_v, cache_b_k, cache_b_v, cache_b_idx, ln_g, ln_b, ffn_w_gate, ffn_w_up, ffn_w_down, a_w_in, a_rel_bias, a_w_out, b_w_in, b_w_out):
    depth = ln_g.shape[0]
    alpha = (2.0 * depth) ** 0.25
    bsz, s, d = x_prompt.shape
    dbs, t, _ = x_sample.shape
    xp, xs = x_prompt, x_sample
    a_out, b_out = [], []

    def ffn(x, layer, i, tm):
        shp = x.shape
        y = _ffn_block(x.reshape(-1, d), ffn_w_gate[layer, i].astype(BF16), ffn_w_up[layer, i].astype(BF16),
                       ffn_w_down[layer, i].astype(BF16), ln_g[layer, 2 * i][None], ln_b[layer, 2 * i][None], alpha, tm)
        return y.reshape(shp)

    for layer in range(depth):
        j = layer // 2
        xp = ffn(xp, layer, 0, 512)
        xs = ffn(xs, layer, 0, dbs * t)
        g, b = ln_g[layer, 1][None], ln_b[layer, 1][None]
        if layer % 2 == 0:
            xp, xs, *rest = _mixer_a(xp, xs, cache_a_k[j], cache_a_v[j], a_w_in[j], a_rel_bias[j], a_w_out[j], g, b, alpha)
            a_out.append(rest)
        else:
            xp, xs, *rest = _mixer_b(xp, xs, cache_b_k[j], cache_b_v[j], cache_b_idx[j], b_w_in[j], b_w_out[j], g, b, alpha)
            b_out.append(rest)
        xp = ffn(xp, layer, 1, 512)
        xs = ffn(xs, layer, 1, dbs * t)

    stack = lambda outs, i: jnp.stack([o[i] for o in outs], 0)
    return (xp, xs,
            stack(a_out, 0), stack(a_out, 1), stack(a_out, 2), stack(a_out, 3),
            stack(b_out, 0), stack(b_out, 1), stack(b_out, 2), stack(b_out, 3), stack(b_out, 4), stack(b_out, 5))
```

```python
import functools
import math
import struct

import jax
import jax.numpy as jnp
from jax import lax
from jax.experimental import pallas as pl
from jax.experimental.pallas import tpu as pltpu

F32 = jnp.float32
BF16 = jnp.bfloat16
I32 = jnp.int32
I16 = jnp.int16

CHUNK = 64
A_PAST = 512
REL_CLIP = 128
HEAD_DIM = 64
LANES = 128
IDX_HEADS = 8
TOPK_MAX = 256
LN_EPS = 1e-5
A_SUB = 4 * CHUNK
A_WIN = A_PAST + A_SUB
VMEM_LIMIT = 52 * 1024 * 1024

LOG2E = 1.4426950408889634
POS_LANE0 = HEAD_DIM
MASK_BIAS = 1e30
ONES_ROWS = 16
ATTEND_KEYS = 512
COUNT_ROWS = 16

NT_DIMS = (((1,), (1,)), ((), ()))
INT_MIN = -2147483648
I16_MIN = -32768
COUNT16_ROWS = 16
COUNT16_CHAINS = 4
NEG_INF_KEY = -2139095041


def _params(n_grid, flags=None):
    return pltpu.CompilerParams(dimension_semantics=("arbitrary",) * n_grid,
                                vmem_limit_bytes=VMEM_LIMIT, flags=flags)


def _resident(shape):
    zeros = (0,) * len(shape)
    return pl.BlockSpec(shape, lambda *_: zeros, pipeline_mode=pl.Buffered(1))


def _layer_norm(z, g, b):
    mu = jnp.mean(z, axis=-1, keepdims=True)
    d = z - mu
    var = jnp.mean(d * d, axis=-1, keepdims=True)
    return d * lax.rsqrt(var + LN_EPS) * g + b


def _dot(a, b):
    return jnp.dot(a, b, preferred_element_type=F32)


def _dot_nt(a, b):
    return lax.dot_general(a, b, NT_DIMS, preferred_element_type=F32)


def _ffn_kernel(x_ref, wg_ref, wu_ref, wd_ref, g_ref, b_ref, o_ref, h_scr, *, n_chunks, alpha):
    x = x_ref[...]
    xb = x.astype(BF16)
    cf = wg_ref.shape[1] // n_chunks
    for c in range(n_chunks):
        sl = slice(c * cf, (c + 1) * cf)
        gate = _dot(xb, wg_ref[:, sl])
        up = _dot(xb, wu_ref[:, sl])
        h_scr[:, sl] = (gate * (1.0 / (1.0 + jnp.exp(-gate))) * up).astype(BF16)
    y = _dot(h_scr[...], wd_ref[...])
    o_ref[...] = _layer_norm(alpha * x + 0.5 * y, g_ref[...], b_ref[...])


def _ffn_block(x, wg, wu, wd, g, b, alpha, tm):
    m, d = x.shape
    f = wg.shape[1]
    const = lambda i: (0, 0)
    return pl.pallas_call(
        functools.partial(_ffn_kernel, n_chunks=4, alpha=alpha),
        grid=(m // tm,),
        in_specs=[pl.BlockSpec((tm, d), lambda i: (i, 0)),
                  _resident((d, f)), _resident((d, f)), _resident((f, d)),
                  pl.BlockSpec((1, d), const), pl.BlockSpec((1, d), const)],
        out_specs=pl.BlockSpec((tm, d), lambda i: (i, 0)),
        out_shape=jax.ShapeDtypeStruct((m, d), F32),
        scratch_shapes=[pltpu.VMEM((tm, f), BF16)],
        compiler_params=_params(1),
        name="ffn_ln",
    )(x, wg, wu, wd, g, b)


def _proj_a_kernel(x_ref, w_ref, q_ref, k_ref, vt_ref, kf_ref, vf_ref, *, width):
    h = _dot(x_ref[0].astype(BF16), w_ref[...])
    k = h[:, width:2 * width]
    v = h[:, 2 * width:]
    q_ref[0] = (h[:, :width] * (HEAD_DIM ** -0.5 * LOG2E)).astype(BF16)
    k_ref[0] = k.astype(BF16)
    vt_ref[0] = v.T.astype(BF16)

    @pl.when(pl.program_id(1) == pl.num_programs(1) - 1)
    def _():
        kf_ref[0] = k
        vf_ref[0] = v


def _proj_a(x, w, tm):
    bsz, s, d = x.shape
    width = w.shape[1] // 3
    row = pl.BlockSpec((1, tm, width), lambda b, t: (b, t, 0))
    last = pl.BlockSpec((1, tm, width), lambda b, t: (b, 0, 0))
    return pl.pallas_call(
        functools.partial(_proj_a_kernel, width=width),
        grid=(bsz, s // tm),
        in_specs=[pl.BlockSpec((1, tm, d), lambda b, t: (b, t, 0)), _resident(w.shape)],
        out_specs=[row, row, pl.BlockSpec((1, width, tm), lambda b, t: (b, 0, t)), last, last],
        out_shape=[jax.ShapeDtypeStruct((bsz, s, width), BF16)] * 2 + [jax.ShapeDtypeStruct((bsz, width, s), BF16)]
        + [jax.ShapeDtypeStruct((bsz, tm, width), F32)] * 2,
        compiler_params=_params(2),
        name="proj_a",
    )(x, w)


def _attn_a_kernel(*refs, n_sub, prompt, alpha):
    if prompt:
        (x_ref, q_ref, kp_ref, kc_ref, vtp_ref, vtc_ref, bias_ref, wo_ref, g_ref, b_ref,
         o_ref, kwin, vtwin, ot_scr) = refs
        kwin[0:A_PAST] = kp_ref[0]
        kwin[A_PAST:2 * A_PAST] = kc_ref[0]
        vtwin[:, 0:A_PAST] = vtp_ref[0]
        vtwin[:, A_PAST:2 * A_PAST] = vtc_ref[0]
        k_at = lambda r0, c0: kwin[r0:r0 + A_WIN, c0:c0 + LANES]
        vt_at = lambda r0, c0: vtwin[c0:c0 + LANES, r0:r0 + A_WIN]
    else:
        x_ref, q_ref, k_ref, vt_ref, bias_ref, wo_ref, g_ref, b_ref, o_ref, ot_scr = refs
        k_at = lambda r0, c0: k_ref[0, r0:r0 + A_WIN, c0:c0 + LANES]
        vt_at = lambda r0, c0: vt_ref[0, c0:c0 + LANES, r0:r0 + A_WIN]
    lane = lax.broadcasted_iota(I32, (A_SUB, LANES), 1)
    key_row = lax.broadcasted_iota(I32, (A_WIN, 2 * A_SUB), 0)
    out_row = lax.broadcasted_iota(I32, (LANES, A_SUB), 0)
    ones_rows = jnp.ones((ONES_ROWS, A_WIN), BF16)

    def attend(first_valid):
        for j in range(n_sub):
            r0 = j * A_SUB
            for p in range(q_ref.shape[2] // LANES):
                c0 = p * LANES
                q2 = q_ref[0, r0:r0 + A_SUB, c0:c0 + LANES].astype(F32)
                qt = jnp.concatenate([jnp.where(lane < HEAD_DIM, q2, 0.0).T, jnp.where(lane < HEAD_DIM, 0.0, q2).T], axis=1)
                s = _dot(k_at(r0, c0), qt.astype(BF16)) + bias_ref[p]
                if first_valid > r0:
                    s = jnp.where(key_row >= first_valid - r0, s, -jnp.inf)
                e = jnp.exp2((s - jnp.max(s, axis=0, keepdims=True)).astype(BF16))
                o = _dot(jnp.concatenate([vt_at(r0, c0), ones_rows], axis=0), e)
                o = o[:LANES] / o[LANES:LANES + 1]
                ot_scr[c0:c0 + LANES, r0:r0 + A_SUB] = jnp.where(out_row < HEAD_DIM, o[:, :A_SUB], o[:, A_SUB:]).astype(BF16)

    if prompt:
        pl.when(pl.program_id(1) == 0)(lambda: attend(A_PAST))
        pl.when(pl.program_id(1) > 0)(lambda: attend(0))
    else:
        attend(0)

    y = lax.dot_general(ot_scr[...], wo_ref[...], (((0,), (0,)), ((), ())), preferred_element_type=F32)
    o_ref[0] = _layer_norm(alpha * x_ref[0] + y, g_ref[...], b_ref[...])


def _attn_a_prompt(x, q, k, vt, bias, wo, g, b, alpha):
    bsz, s, d = x.shape
    width = q.shape[2]
    tq = A_PAST
    cur = lambda bi, t: (bi, t, 0)
    prev = lambda bi, t: (bi, jnp.maximum(t - 1, 0), 0)
    cur_t = lambda bi, t: (bi, 0, t)
    prev_t = lambda bi, t: (bi, 0, jnp.maximum(t - 1, 0))
    const2 = lambda bi, t: (0, 0)
    return pl.pallas_call(
        functools.partial(_attn_a_kernel, n_sub=tq // A_SUB, prompt=True, alpha=alpha),
        grid=(bsz, s // tq),
        in_specs=[pl.BlockSpec((1, tq, d), cur), pl.BlockSpec((1, tq, width), cur),
                  pl.BlockSpec((1, tq, width), prev), pl.BlockSpec((1, tq, width), cur),
                  pl.BlockSpec((1, width, tq), prev_t), pl.BlockSpec((1, width, tq), cur_t),
                  _resident(bias.shape), _resident(wo.shape),
                  pl.BlockSpec((1, d), const2), pl.BlockSpec((1, d), const2)],
        out_specs=pl.BlockSpec((1, tq, d), cur),
        out_shape=jax.ShapeDtypeStruct((bsz, s, d), F32),
        scratch_shapes=[pltpu.VMEM((2 * tq, width), BF16), pltpu.VMEM((width, 2 * tq), BF16),
                        pltpu.VMEM((width, tq), BF16)],
        compiler_params=_params(2),
        name="attn_a_prompt",
    )(x, q, k, k, vt, vt, bias, wo, g, b)


def _attn_a_sample(x, q, kwin, vtwin, bias, wo, g, b, alpha):
    bsz, _, d = x.shape
    width = q.shape[2]
    blk = lambda n, c: pl.BlockSpec((1, n, c), lambda bi: (bi, 0, 0))
    const2 = lambda bi: (0, 0)
    return pl.pallas_call(
        functools.partial(_attn_a_kernel, n_sub=1, prompt=False, alpha=alpha),
        grid=(bsz,),
        in_specs=[blk(A_SUB, d), blk(A_SUB, width), blk(A_WIN, width), blk(width, A_WIN),
                  _resident(bias.shape), _resident(wo.shape),
                  pl.BlockSpec((1, d), const2), pl.BlockSpec((1, d), const2)],
        out_specs=blk(A_SUB, d),
        out_shape=jax.ShapeDtypeStruct((bsz, A_SUB, d), F32),
        scratch_shapes=[pltpu.VMEM((width, A_SUB), BF16)],
        compiler_params=_params(1),
        name="attn_a_sample",
    )(x, q, kwin, vtwin, bias, wo, g, b)


def _rel_bias_tables(rel_bias, n_sample):
    r = jnp.arange(A_SUB)[:, None]
    c = jnp.arange(A_WIN)[None, :]
    period = A_SUB + A_WIN
    diff = jnp.arange(period)
    diff = jnp.where(diff < A_WIN, diff, diff - period)
    line = rel_bias[:, jnp.clip(A_PAST - diff, -REL_CLIP, REL_CLIP) + REL_CLIP].astype(F32) * LOG2E
    table = jnp.tile(line, (1, A_SUB))[:, :A_SUB * (period - 1)].reshape(-1, A_SUB, period - 1)[:, :, :A_WIN]
    lo = (r // CHUNK) * CHUNK
    band = (c >= lo) & (c < lo + A_PAST + CHUNK)
    prompt = jnp.where(band[None], table, -jnp.inf)
    live = (c < A_PAST + n_sample)
    sample = jnp.where(live[None], jnp.where((r < n_sample)[None], table, 0.0), -jnp.inf)

    def pair_layout(tab):
        h = tab.shape[0]
        return tab.transpose(0, 2, 1).reshape(h // 2, 2, A_WIN, A_SUB).transpose(0, 2, 1, 3).reshape(h // 2, A_WIN, 2 * A_SUB)

    return pair_layout(prompt), pair_layout(sample)


def _round_to_bf16(x):
    bits = struct.unpack("<I", struct.pack("<f", x))[0]
    bits = (bits + 0x7FFF + ((bits >> 16) & 1)) & 0xFFFF0000
    return struct.unpack("<f", struct.pack("<I", bits))[0]


def _bf16_pieces(x):
    p1 = _round_to_bf16(x)
    p2 = _round_to_bf16(x - p1)
    return (p1, p2, _round_to_bf16(x - p1 - p2))


def _pos_lanes(pos, lane):
    hi = (lax.shift_right_logical(pos, 6) * CHUNK).astype(F32)
    lo = (pos & (CHUNK - 1)).astype(F32)
    return jnp.where(lane < POS_LANE0 + 3, hi, jnp.where(lane < POS_LANE0 + 6, lo, 0.0))


def _proj_b_kernel(x_ref, w_ref, q_ref, qi_ref, ka_ref, kf_ref, vf_ref, kiwi_ref, kd_ref, wit_ref, vt_ref,
                   *, nq, nkv, nqi, pos0, period):
    h = _dot(x_ref[...].astype(BF16), w_ref[...])
    q_ref[...] = (h[:, :nq] * (HEAD_DIM ** -0.5 * LOG2E)).astype(BF16)
    k = h[:, nq:nq + nkv]
    v = h[:, nq + nkv:nq + 2 * nkv]
    kf_ref[...] = k
    vf_ref[...] = v
    o0 = nq + 2 * nkv
    qi_ref[...] = (h[:, o0:o0 + nqi] * (HEAD_DIM ** -0.5)).astype(BF16)
    tail = h[:, o0 + nqi:o0 + nqi + LANES]
    kiwi_ref[...] = tail
    lane = lax.broadcasted_iota(I32, tail.shape, 1)
    kd_ref[...] = jnp.where(lane < HEAD_DIM, tail, pltpu.roll(tail, HEAD_DIM, 1)).astype(BF16)
    wit_ref[...] = tail.T[HEAD_DIM:HEAD_DIM + IDX_HEADS, :] * (IDX_HEADS ** -0.5)
    tm = tail.shape[0]
    row = pl.program_id(0) * tm + lax.broadcasted_iota(I32, (tm, 1), 0)
    pos = _pos_lanes(pos0 + (row & (period - 1)), lane)
    for pair in range(nkv // LANES):
        for half in range(2):
            tile = slice((2 * pair + half) * LANES, (2 * pair + half + 1) * LANES)
            vp = v[:, pair * LANES:(pair + 1) * LANES]
            kp = k[:, pair * LANES:(pair + 1) * LANES]
            if half:
                vp, kp = pltpu.roll(vp, HEAD_DIM, 1), pltpu.roll(kp, HEAD_DIM, 1)
            vt_ref[0, tile, :] = jnp.where(lane < HEAD_DIM, vp, 1.0).T.astype(BF16)
            ka_ref[:, tile] = jnp.where(lane < HEAD_DIM, kp, pos).astype(BF16)


def _proj_b(x, w_pad, nq, nkv, nqi, tm, pos0, period):
    m, d = x.shape
    assert period & (period - 1) == 0
    rows = lambda c: pl.BlockSpec((tm, c), lambda i: (i, 0))
    outs = [(nq, BF16), (nqi, BF16), (2 * nkv, BF16), (nkv, F32), (nkv, F32), (LANES, F32), (LANES, BF16)]
    return pl.pallas_call(
        functools.partial(_proj_b_kernel, nq=nq, nkv=nkv, nqi=nqi, pos0=pos0, period=period),
        grid=(m // tm,),
        in_specs=[rows(d), _resident(w_pad.shape)],
        out_specs=[rows(c) for c, _ in outs] + [pl.BlockSpec((IDX_HEADS, tm), lambda i: (0, i)),
                                                 pl.BlockSpec((1, 2 * nkv, tm), lambda i: (i, 0, 0))],
        out_shape=[jax.ShapeDtypeStruct((m, c), dt) for c, dt in outs]
        + [jax.ShapeDtypeStruct((IDX_HEADS, m), F32), jax.ShapeDtypeStruct((m // tm, 2 * nkv, tm), BF16)],
        compiler_params=_params(1),
        name="proj_b",
    )(x, w_pad)


def _sparse_kernel(x_ref, q_ref, qi_ref, wit_ref, ka_ref, vt_ref, kd_ref, wo_ref, g_ref, b_ref, o_ref,
                   key_scr, hi_scr, lo_scr, bias_scr, qt_scr, acc_scr, m_scr, ot_scr,
                   *, tq, kb, pos0, n_keys, n_sel, n_heads, group, slopes, alpha):
    t = pl.program_id(1)
    q0 = pos0 + t * tq
    qpos_row = q0 + lax.broadcasted_iota(I32, (1, tq), 1)
    lim_row = jnp.minimum((lax.shift_right_logical(qpos_row, 6) + 1) * CHUNK, n_keys)
    kmax = jnp.minimum((lax.shift_right_logical(q0 + tq - 1, 6) + 1) * CHUNK, n_keys)
    nkb = lax.shift_right_logical(kmax + kb - 1, int(math.log2(kb)))
    lane = lax.broadcasted_iota(I32, (tq, LANES), 1)
    lo_half = lane < HEAD_DIM

    def key_block(kbi):
        return pl.multiple_of(kbi * kb, kb)

    def key_index(off):
        return off + lax.broadcasted_iota(I32, (kb, tq), 0)

    qis = []
    for p in range(IDX_HEADS // 2):
        q2 = qi_ref[0, :, p * LANES:(p + 1) * LANES]
        qis.append(jnp.where(lo_half, q2, jnp.zeros_like(q2)))
        qis.append(jnp.where(lo_half, jnp.zeros_like(q2), q2))
    wis = wit_ref[...]
    qi_all = jnp.concatenate(qis, axis=0)

    def score_block(kbi, carry):
        off = key_block(kbi)
        dots = _dot_nt(kd_ref[0, pl.ds(off, kb), :], qi_all)
        acc = jnp.zeros((kb, tq), F32)
        for h in range(IDX_HEADS):
            acc = acc + jnp.maximum(dots[:, h * tq:(h + 1) * tq], 0.0) * wis[h:h + 1, :]
        acc = jnp.where(key_index(off) < lim_row, acc, -jnp.inf)
        bits = lax.bitcast_convert_type(acc, I32)
        key = bits ^ (lax.shift_right_arithmetic(bits, 31) & 0x7FFFFFFF)
        key_scr[pl.ds(off, kb), :] = key
        hi_scr[pl.ds(off, kb), :] = lax.shift_right_arithmetic(key, 16).astype(I16)
        lo_scr[pl.ds(off, kb), :] = ((key & 0xFFFF) + I16_MIN).astype(I16)
        return carry

    lax.fori_loop(0, nkb, score_block, 0)

    @pl.when(nkb % 2 == 1)
    def _():
        key_scr[pl.ds(key_block(nkb), kb), :] = jnp.full((kb, tq), INT_MIN, I32)
        hi_scr[pl.ds(key_block(nkb), kb), :] = jnp.full((kb, tq), I16_MIN, I16)
        lo_scr[pl.ds(key_block(nkb), kb), :] = jnp.full((kb, tq), I16_MIN, I16)

    n_steps = lax.shift_right_logical(nkb + 1, 1)

    def step_rows(i):
        return pl.ds(pl.multiple_of(i * 2 * kb, 2 * kb), 2 * kb)

    def count(pred):
        def body(i, part):
            hit = pred(key_scr[step_rows(i), :])
            return part + jnp.sum(jnp.where(hit, 1, 0).reshape(-1, COUNT_ROWS, tq), axis=0)
        part = lax.fori_loop(0, n_steps, body, jnp.zeros((COUNT_ROWS, tq), I32))
        return jnp.sum(part, axis=0, keepdims=True)

    def count16(ref, pred):
        def body(i, parts):
            ind = jnp.where(pred(ref[step_rows(i), :]), jnp.int16(1), jnp.int16(0))
            parts = list(parts)
            for j, r in enumerate(range(0, 2 * kb, COUNT16_ROWS)):
                parts[j % len(parts)] = parts[j % len(parts)] + ind[r:r + COUNT16_ROWS]
            return tuple(parts)
        parts = lax.fori_loop(0, n_steps, body, (jnp.zeros((COUNT16_ROWS, tq), I16),) * COUNT16_CHAINS)
        return jnp.sum(sum(p.astype(I32) for p in parts), axis=0, keepdims=True)

    def as_i16(row):
        return jnp.broadcast_to(row, (COUNT16_ROWS, tq)).astype(I16)[:1]

    def bisect16(ref, want):
        def body(i, thr):
            cand = thr + lax.shift_left(jnp.int32(1), 15 - i)
            cand16 = as_i16(cand)
            return jnp.where(count16(ref, lambda v: v >= cand16) >= want, cand, thr)
        return lax.fori_loop(0, 16, body, jnp.full((1, tq), I16_MIN, I32))

    thr_hi = bisect16(hi_scr, n_sel)
    thr_hi16 = as_i16(thr_hi)
    n_hi_gt = count16(hi_scr, lambda v: v > thr_hi16)

    def keep_bucket(i, carry):
        rows = step_rows(i)
        lo_scr[rows, :] = jnp.where(hi_scr[rows, :] == thr_hi16, lo_scr[rows, :], jnp.int16(I16_MIN))
        return carry

    lax.fori_loop(0, n_steps, keep_bucket, 0)
    thr = thr_hi * 65536 + (bisect16(lo_scr, n_sel - n_hi_gt) - I16_MIN)
    n_gt = count(lambda key: key > thr)
    n_eq = count(lambda key: key == thr)
    need = n_sel - n_gt
    tie = jnp.max(jnp.where((n_eq > need) & (thr > NEG_INF_KEY), 1, 0)) > 0

    @pl.when(jnp.logical_not(tie))
    def _():
        def body(kbi, carry):
            off = key_block(kbi)
            sel = (key_scr[pl.ds(off, kb), :] >= thr) & (key_index(off) < lim_row)
            bias_scr[pl.ds(off, kb), :] = jnp.where(sel, 0.0, -MASK_BIAS)
            return carry
        lax.fori_loop(0, nkb, body, 0)

    @pl.when(tie)
    def _():
        tri = jnp.where(lax.broadcasted_iota(I32, (kb, kb), 0) > lax.broadcasted_iota(I32, (kb, kb), 1),
                        1.0, 0.0).astype(BF16)
        need_f = need.astype(F32)

        def body(kbi, seen):
            off = key_block(kbi)
            key = key_scr[pl.ds(off, kb), :]
            eq = jnp.where(key == thr, 1.0, 0.0)
            earlier = _dot(tri, eq.astype(BF16)) + seen
            sel = ((key > thr) | ((key == thr) & (earlier < need_f))) & (key_index(off) < lim_row)
            bias_scr[pl.ds(off, kb), :] = jnp.where(sel, 0.0, -MASK_BIAS)
            return seen + jnp.sum(eq, axis=0, keepdims=True)
        lax.fori_loop(0, nkb, body, jnp.zeros((1, tq), F32))

    n_kv = n_heads // group
    for h in range(n_heads):
        q2 = q_ref[0, :, (h // 2) * LANES:(h // 2 + 1) * LANES].astype(F32)
        if h % 2:
            q2 = pltpu.roll(q2, HEAD_DIM, 1)
        sl = jnp.zeros((1, LANES), F32)
        for i, piece in enumerate(_bf16_pieces(slopes[h] * LOG2E) * 2):
            sl = jnp.where(lane[:1] == POS_LANE0 + i, piece, sl)
        qt_scr[h // group, :, (h % group) * tq:(h % group + 1) * tq] = jnp.where(lo_half, q2, sl).T.astype(BF16)
    m_scr[...] = jnp.full(m_scr.shape, -jnp.inf, F32)
    acc_scr[...] = jnp.zeros(acc_scr.shape, F32)

    def attend(kbi, last):
        for part in range(kb // ATTEND_KEYS):
            off = pl.multiple_of(kbi * kb + part * ATTEND_KEYS, ATTEND_KEYS)
            cols = slice(part * ATTEND_KEYS, (part + 1) * ATTEND_KEYS)
            bias = bias_scr[pl.ds(off, ATTEND_KEYS), :]
            bias = jnp.concatenate([bias] * group, axis=1)
            if last:
                key_frame = off + lax.broadcasted_iota(I32, (ATTEND_KEYS, tq), 0)
                ahead = jnp.maximum(key_frame - qpos_row, 0).astype(F32)
            for g in range(n_kv):
                s = _dot(ka_ref[0, pl.ds(off, ATTEND_KEYS), g * LANES:(g + 1) * LANES], qt_scr[g]) + bias
                if last:
                    s = s - jnp.concatenate([(2.0 * slopes[g * group + j] * LOG2E) * ahead for j in range(group)], axis=1)
                m_old = m_scr[g]
                m_new = jnp.maximum(m_old, jnp.max(s, axis=0, keepdims=True))
                p = jnp.exp2((s - m_new).astype(BF16))
                acc_scr[g] = acc_scr[g] * jnp.exp2(m_old - m_new) + _dot(vt_ref[0, kbi, g * LANES:(g + 1) * LANES, cols], p)
                m_scr[g] = m_new

    def attend_body(kbi, carry):
        attend(kbi, False)
        return carry

    lax.fori_loop(0, nkb - 1, attend_body, 0)
    attend(nkb - 1, True)

    for g in range(n_kv):
        acc = acc_scr[g]
        o_t = acc[:HEAD_DIM] / acc[HEAD_DIM:HEAD_DIM + 1]
        for j in range(group):
            h = g * group + j
            ot_scr[h * HEAD_DIM:(h + 1) * HEAD_DIM, :] = o_t[:, j * tq:(j + 1) * tq].astype(BF16)
    y = lax.dot_general(ot_scr[...], wo_ref[...], (((0,), (0,)), ((), ())), preferred_element_type=F32)
    o_ref[0] = _layer_norm(alpha * x_ref[0] + y, g_ref[...], b_ref[...])


def _sparse_attn(x, q, qi, wit, k, vt, kd, wo, g, b, *, tq, kb, pos0, n_keys, alpha):
    bsz, s, d = x.shape
    n_t = s // tq
    l_pad = k.shape[1]
    n_heads = q.shape[2] // HEAD_DIM
    n_kv = k.shape[2] // LANES
    group = n_heads // n_kv
    assert kb % tq == 0 and pos0 % tq == 0 and l_pad % kb == 0 and vt.shape == (bsz, l_pad // kb, k.shape[2], kb)
    slopes = tuple(2.0 ** (-8.0 * (h + 1) / n_heads) for h in range(n_heads))
    qrow = lambda c: pl.BlockSpec((1, tq, c), lambda bi, t: (bi, t, 0))
    keys = lambda c: pl.BlockSpec((1, l_pad, c), lambda bi, t: (bi, 0, 0))
    const2 = lambda bi, t: (0, 0)
    sel_rows = -(-l_pad // (2 * kb)) * 2 * kb
    kern = functools.partial(_sparse_kernel, tq=tq, kb=kb, pos0=pos0, n_keys=n_keys, n_sel=min(TOPK_MAX, n_keys // 4),
                             n_heads=n_heads, group=group, slopes=slopes, alpha=alpha)
    return pl.pallas_call(
        kern,
        grid=(bsz, n_t),
        in_specs=[qrow(d), qrow(q.shape[2]), qrow(qi.shape[2]),
                  pl.BlockSpec((IDX_HEADS, tq), lambda bi, t: (0, bi * n_t + t)),
                  keys(k.shape[2]), pl.BlockSpec((1,) + vt.shape[1:], lambda bi, t: (bi, 0, 0, 0)), keys(kd.shape[2]),
                  _resident(wo.shape), pl.BlockSpec((1, d), const2), pl.BlockSpec((1, d), const2)],
        out_specs=qrow(d),
        out_shape=jax.ShapeDtypeStruct((bsz, s, d), F32),
        scratch_shapes=[pltpu.VMEM((sel_rows, tq), I32), pltpu.VMEM((sel_rows, tq), I16), pltpu.VMEM((sel_rows, tq), I16),
                        pltpu.VMEM((l_pad, tq), F32),
                        pltpu.VMEM((n_kv, LANES, group * tq), BF16), pltpu.VMEM((n_kv, LANES, group * tq), F32),
                        pltpu.VMEM((n_kv, 1, group * tq), F32), pltpu.VMEM((q.shape[2], tq), BF16)],
        compiler_params=_params(2),
        name="sparse_attn",
    )(x, q, qi, wit, k, vt, kd, wo, g, b)


def _pad_rows(a, n):
    return jnp.pad(a, ((0, 0), (0, n - a.shape[1]), (0, 0)))


def _mixer_a(xp, xs, cache_k, cache_v, w_in, rel_bias, w_out, g, b, alpha):
    bsz, s, d = xp.shape
    dbs, t, _ = xs.shape
    width = w_out.shape[0]
    w_in = w_in.astype(BF16)
    w_out = w_out.astype(BF16)
    bias_p, bias_s = _rel_bias_tables(rel_bias, t)
    keep = min(A_PAST, s)
    q, k, vt, kf, vf = _proj_a(xp, w_in, keep)
    yp = _attn_a_prompt(xp, q, k, vt, bias_p, w_out, g, b, alpha)
    qs, ks, _, ksf, vsf = _proj_a(xs.reshape(1, dbs * t, d), w_in, dbs * t)
    n_cache = cache_k.shape[1]
    kwin = _pad_rows(jnp.concatenate([cache_k.reshape(dbs, n_cache, width).astype(BF16), ks.reshape(dbs, t, width)], 1), A_WIN)
    vwin = _pad_rows(jnp.concatenate([cache_v.reshape(dbs, n_cache, width), vsf.reshape(dbs, t, width)], 1).astype(BF16), A_WIN)
    ys = _attn_a_sample(_pad_rows(xs, A_SUB), _pad_rows(qs.reshape(dbs, t, width), A_SUB), kwin, vwin.transpose(0, 2, 1),
                        bias_s, w_out, g, b, alpha)[:, :t]
    heads = width // HEAD_DIM
    return (yp, ys, kf.reshape(bsz, keep, heads, HEAD_DIM), vf.reshape(bsz, keep, heads, HEAD_DIM),
            ksf.reshape(dbs, t, heads, HEAD_DIM), vsf.reshape(dbs, t, heads, HEAD_DIM))


def _mixer_b(xp, xs, cache_k, cache_v, cache_idx, w_in, w_out, g, b, alpha):
    bsz, s, d = xp.shape
    dbs, t, _ = xs.shape
    nq = w_out.shape[0]
    nkv = cache_k.shape[2] * cache_k.shape[3]
    nqi = IDX_HEADS * HEAD_DIM
    kvh = nkv // HEAD_DIM
    past = cache_k.shape[1]
    w_pad = jnp.pad(w_in, ((0, 0), (0, nq + 2 * nkv + nqi + LANES - w_in.shape[1]))).astype(BF16)
    w_out = w_out.astype(BF16)
    kb = 512

    q, qi, k, kf, vf, kiwi, kd, wit, vt = _proj_b(xp.reshape(bsz * s, d), w_pad, nq, nkv, nqi, kb, 0, s)
    r3 = lambda a: a.reshape(bsz, s, a.shape[1])
    yp = _sparse_attn(xp, r3(q), r3(qi), wit, r3(k), vt.reshape(bsz, s // kb, 2 * nkv, kb), r3(kd), w_out, g, b,
                      tq=256, kb=kb, pos0=0, n_keys=s, alpha=alpha)

    qs, qis, ks, ksf, vsf, kiwis, kds, wits, _ = _proj_b(xs.reshape(dbs * t, d), w_pad, nq, nkv, nqi, dbs * t, past, t)
    tq_s = LANES
    n_keys = past + t
    l_pad = -(-n_keys // kb) * kb
    s3 = lambda a: a.reshape(dbs, t, a.shape[1])
    v_all = jnp.concatenate([cache_v, vsf.reshape(dbs, t, kvh, HEAD_DIM)], axis=1).astype(BF16)
    v_all = jnp.concatenate([v_all, jnp.ones_like(v_all)], axis=-1).reshape(dbs, n_keys, 2 * nkv)
    keys_vt = _pad_rows(v_all, l_pad).reshape(dbs, l_pad // kb, kb, 2 * nkv).transpose(0, 1, 3, 2)
    kd_cache = jnp.concatenate([cache_idx, cache_idx], axis=-1).astype(BF16)
    frame = jnp.arange(past, dtype=I32)[:, None]
    pos = _pos_lanes(frame, HEAD_DIM + jnp.arange(HEAD_DIM, dtype=I32)[None, :]).astype(BF16)
    pos = jnp.broadcast_to(pos[None, :, None, :], (dbs, past, kvh, HEAD_DIM))
    k_cache = jnp.concatenate([cache_k.astype(BF16), pos], axis=-1).reshape(dbs, past, 2 * nkv)
    keys_k = _pad_rows(jnp.concatenate([k_cache, s3(ks)], 1), l_pad)
    keys_kd = _pad_rows(jnp.concatenate([kd_cache, s3(kds)], 1), l_pad)
    wit_pad = jnp.pad(wits.reshape(IDX_HEADS, dbs, t), ((0, 0), (0, 0), (0, tq_s - t))).reshape(IDX_HEADS, dbs * tq_s)
    ys = _sparse_attn(_pad_rows(xs, tq_s), _pad_rows(s3(qs), tq_s), _pad_rows(s3(qis), tq_s), wit_pad,
                      keys_k, keys_vt, keys_kd, w_out, g, b,
                      tq=tq_s, kb=kb, pos0=past, n_keys=n_keys, alpha=alpha)[:, :t]
    return (yp, ys,
            kf.reshape(bsz, s, kvh, HEAD_DIM), vf.reshape(bsz, s, kvh, HEAD_DIM), kiwi[:, :HEAD_DIM].reshape(bsz, s, HEAD_DIM),
            ksf.reshape(dbs, t, kvh, HEAD_DIM), vsf.reshape(dbs, t, kvh, HEAD_DIM), kiwis[:, :HEAD_DIM].reshape(dbs, t, HEAD_DIM))


def kernel(x_prompt, x_sample, cache_a_k, cache_a_v, cache_b_k, cache_b_v, cache_b_idx, ln_g, ln_b, ffn_w_gate, ffn_w_up, ffn_w_down, a_w_in, a_rel_bias, a_w_out, b_w_in, b_w_out):
    depth = ln_g.shape[0]
    alpha = (2.0 * depth) ** 0.25
    bsz, s, d = x_prompt.shape
    dbs, t, _ = x_sample.shape
    xp, xs = x_prompt, x_sample
    a_out, b_out = [], []

    def ffn(x, layer, i, tm):
        shp = x.shape
        y = _ffn_block(x.reshape(-1, d), ffn_w_gate[layer, i].astype(BF16), ffn_w_up[layer, i].astype(BF16),
                       ffn_w_down[layer, i].astype(BF16), ln_g[layer, 2 * i][None], ln_b[layer, 2 * i][None], alpha, tm)
        return y.reshape(shp)

    for layer in range(depth):
        j = layer // 2
        xp = ffn(xp, layer, 0, 512)
        xs = ffn(xs, layer, 0, dbs * t)
        g, b = ln_g[layer, 1][None], ln_b[layer, 1][None]
        if layer % 2 == 0:
            xp, xs, *rest = _mixer_a(xp, xs, cache_a_k[j], cache_a_v[j], a_w_in[j], a_rel_bias[j], a_w_out[j], g, b, alpha)
            a_out.append(rest)
        else:
            xp, xs, *rest = _mixer_b(xp, xs, cache_b_k[j], cache_b_v[j], cache_b_idx[j], b_w_in[j], b_w_out[j], g, b, alpha)
            b_out.append(rest)
        xp = ffn(xp, layer, 1, 512)
        xs = ffn(xs, layer, 1, dbs * t)

    stack = lambda outs, i: jnp.stack([o[i] for o in outs], 0)
    return (xp, xs,
            stack(a_out, 0), stack(a_out, 1), stack(a_out, 2), stack(a_out, 3),
            stack(b_out, 0), stack(b_out, 1), stack(b_out, 2), stack(b_out, 3), stack(b_out, 4), stack(b_out, 5))
```

```python
import functools
import math
import struct

import jax
import jax.numpy as jnp
from jax import lax
from jax.experimental import pallas as pl
from jax.experimental.pallas import tpu as pltpu

F32 = jnp.float32
BF16 = jnp.bfloat16
I32 = jnp.int32
I16 = jnp.int16

CHUNK = 64
A_PAST = 512
REL_CLIP = 128
HEAD_DIM = 64
LANES = 128
IDX_HEADS = 8
TOPK_MAX = 256
LN_EPS = 1e-5
A_SUB = 4 * CHUNK
A_WIN = A_PAST + A_SUB
VMEM_LIMIT = 52 * 1024 * 1024

LOG2E = 1.4426950408889634
POS_LANE0 = HEAD_DIM
MASK_BIAS = 1e30
ONES_ROWS = 16
UNIT_HEADS = 2
COUNT_ROWS = 16

NT_DIMS = (((1,), (1,)), ((), ()))
INT_MIN = -2147483648
I16_MIN = -32768
COUNT16_ROWS = 16
COUNT16_CHAINS = 4
NEG_INF_KEY = -2139095041


def _params(n_grid, flags=None):
    return pltpu.CompilerParams(dimension_semantics=("arbitrary",) * n_grid,
                                vmem_limit_bytes=VMEM_LIMIT, flags=flags)


def _resident(shape):
    zeros = (0,) * len(shape)
    return pl.BlockSpec(shape, lambda *_: zeros, pipeline_mode=pl.Buffered(1))


def _layer_norm(z, g, b):
    mu = jnp.mean(z, axis=-1, keepdims=True)
    d = z - mu
    var = jnp.mean(d * d, axis=-1, keepdims=True)
    return d * lax.rsqrt(var + LN_EPS) * g + b


def _dot(a, b):
    return jnp.dot(a, b, preferred_element_type=F32)


def _dot_nt(a, b):
    return lax.dot_general(a, b, NT_DIMS, preferred_element_type=F32)


def _ffn_kernel(x_ref, wg_ref, wu_ref, wd_ref, g_ref, b_ref, o_ref, h_scr, *, n_chunks, alpha):
    x = x_ref[...]
    xb = x.astype(BF16)
    cf = wg_ref.shape[1] // n_chunks
    for c in range(n_chunks):
        sl = slice(c * cf, (c + 1) * cf)
        gate = _dot(xb, wg_ref[:, sl])
        up = _dot(xb, wu_ref[:, sl])
        h_scr[:, sl] = (gate * (1.0 / (1.0 + jnp.exp(-gate))) * up).astype(BF16)
    y = _dot(h_scr[...], wd_ref[...])
    o_ref[...] = _layer_norm(alpha * x + 0.5 * y, g_ref[...], b_ref[...])


def _ffn_block(x, wg, wu, wd, g, b, alpha, tm):
    m, d = x.shape
    f = wg.shape[1]
    const = lambda i: (0, 0)
    return pl.pallas_call(
        functools.partial(_ffn_kernel, n_chunks=4, alpha=alpha),
        grid=(m // tm,),
        in_specs=[pl.BlockSpec((tm, d), lambda i: (i, 0)),
                  _resident((d, f)), _resident((d, f)), _resident((f, d)),
                  pl.BlockSpec((1, d), const), pl.BlockSpec((1, d), const)],
        out_specs=pl.BlockSpec((tm, d), lambda i: (i, 0)),
        out_shape=jax.ShapeDtypeStruct((m, d), F32),
        scratch_shapes=[pltpu.VMEM((tm, f), BF16)],
        compiler_params=_params(1),
        name="ffn_ln",
    )(x, wg, wu, wd, g, b)


def _proj_a_kernel(x_ref, w_ref, q_ref, k_ref, vt_ref, kf_ref, vf_ref, *, width):
    h = _dot(x_ref[0].astype(BF16), w_ref[...])
    k = h[:, width:2 * width]
    v = h[:, 2 * width:]
    q_ref[0] = (h[:, :width] * (HEAD_DIM ** -0.5 * LOG2E)).astype(BF16)
    k_ref[0] = k.astype(BF16)
    vt_ref[0] = v.T.astype(BF16)

    @pl.when(pl.program_id(1) == pl.num_programs(1) - 1)
    def _():
        kf_ref[0] = k
        vf_ref[0] = v


def _proj_a(x, w, tm):
    bsz, s, d = x.shape
    width = w.shape[1] // 3
    row = pl.BlockSpec((1, tm, width), lambda b, t: (b, t, 0))
    last = pl.BlockSpec((1, tm, width), lambda b, t: (b, 0, 0))
    return pl.pallas_call(
        functools.partial(_proj_a_kernel, width=width),
        grid=(bsz, s // tm),
        in_specs=[pl.BlockSpec((1, tm, d), lambda b, t: (b, t, 0)), _resident(w.shape)],
        out_specs=[row, row, pl.BlockSpec((1, width, tm), lambda b, t: (b, 0, t)), last, last],
        out_shape=[jax.ShapeDtypeStruct((bsz, s, width), BF16)] * 2 + [jax.ShapeDtypeStruct((bsz, width, s), BF16)]
        + [jax.ShapeDtypeStruct((bsz, tm, width), F32)] * 2,
        compiler_params=_params(2),
        name="proj_a",
    )(x, w)


def _attn_a_kernel(*refs, n_sub, prompt, alpha):
    if prompt:
        (x_ref, q_ref, kp_ref, kc_ref, vtp_ref, vtc_ref, bias_ref, wo_ref, g_ref, b_ref,
         o_ref, kwin, vtwin, ot_scr) = refs
        kwin[0:A_PAST] = kp_ref[0]
        kwin[A_PAST:2 * A_PAST] = kc_ref[0]
        vtwin[:, 0:A_PAST] = vtp_ref[0]
        vtwin[:, A_PAST:2 * A_PAST] = vtc_ref[0]
        k_at = lambda r0, c0: kwin[r0:r0 + A_WIN, c0:c0 + LANES]
        vt_at = lambda r0, c0: vtwin[c0:c0 + LANES, r0:r0 + A_WIN]
    else:
        x_ref, q_ref, k_ref, vt_ref, bias_ref, wo_ref, g_ref, b_ref, o_ref, ot_scr = refs
        k_at = lambda r0, c0: k_ref[0, r0:r0 + A_WIN, c0:c0 + LANES]
        vt_at = lambda r0, c0: vt_ref[0, c0:c0 + LANES, r0:r0 + A_WIN]
    lane = lax.broadcasted_iota(I32, (A_SUB, LANES), 1)
    key_row = lax.broadcasted_iota(I32, (A_WIN, 2 * A_SUB), 0)
    out_row = lax.broadcasted_iota(I32, (LANES, A_SUB), 0)
    ones_rows = jnp.ones((ONES_ROWS, A_WIN), BF16)

    def attend(first_valid):
        for j in range(n_sub):
            r0 = j * A_SUB
            for p in range(q_ref.shape[2] // LANES):
                c0 = p * LANES
                q2 = q_ref[0, r0:r0 + A_SUB, c0:c0 + LANES].astype(F32)
                qt = jnp.concatenate([jnp.where(lane < HEAD_DIM, q2, 0.0).T, jnp.where(lane < HEAD_DIM, 0.0, q2).T], axis=1)
                s = _dot(k_at(r0, c0), qt.astype(BF16)) + bias_ref[p]
                if first_valid > r0:
                    s = jnp.where(key_row >= first_valid - r0, s, -jnp.inf)
                e = jnp.exp2((s - jnp.max(s, axis=0, keepdims=True)).astype(BF16))
                o = _dot(jnp.concatenate([vt_at(r0, c0), ones_rows], axis=0), e)
                o = o[:LANES] / o[LANES:LANES + 1]
                ot_scr[c0:c0 + LANES, r0:r0 + A_SUB] = jnp.where(out_row < HEAD_DIM, o[:, :A_SUB], o[:, A_SUB:]).astype(BF16)

    if prompt:
        pl.when(pl.program_id(1) == 0)(lambda: attend(A_PAST))
        pl.when(pl.program_id(1) > 0)(lambda: attend(0))
    else:
        attend(0)

    y = lax.dot_general(ot_scr[...], wo_ref[...], (((0,), (0,)), ((), ())), preferred_element_type=F32)
    o_ref[0] = _layer_norm(alpha * x_ref[0] + y, g_ref[...], b_ref[...])


def _attn_a_prompt(x, q, k, vt, bias, wo, g, b, alpha):
    bsz, s, d = x.shape
    width = q.shape[2]
    tq = A_PAST
    cur = lambda bi, t: (bi, t, 0)
    prev = lambda bi, t: (bi, jnp.maximum(t - 1, 0), 0)
    cur_t = lambda bi, t: (bi, 0, t)
    prev_t = lambda bi, t: (bi, 0, jnp.maximum(t - 1, 0))
    const2 = lambda bi, t: (0, 0)
    return pl.pallas_call(
        functools.partial(_attn_a_kernel, n_sub=tq // A_SUB, prompt=True, alpha=alpha),
        grid=(bsz, s // tq),
        in_specs=[pl.BlockSpec((1, tq, d), cur), pl.BlockSpec((1, tq, width), cur),
                  pl.BlockSpec((1, tq, width), prev), pl.BlockSpec((1, tq, width), cur),
                  pl.BlockSpec((1, width, tq), prev_t), pl.BlockSpec((1, width, tq), cur_t),
                  _resident(bias.shape), _resident(wo.shape),
                  pl.BlockSpec((1, d), const2), pl.BlockSpec((1, d), const2)],
        out_specs=pl.BlockSpec((1, tq, d), cur),
        out_shape=jax.ShapeDtypeStruct((bsz, s, d), F32),
        scratch_shapes=[pltpu.VMEM((2 * tq, width), BF16), pltpu.VMEM((width, 2 * tq), BF16),
                        pltpu.VMEM((width, tq), BF16)],
        compiler_params=_params(2),
        name="attn_a_prompt",
    )(x, q, k, k, vt, vt, bias, wo, g, b)


def _attn_a_sample(x, q, kwin, vtwin, bias, wo, g, b, alpha):
    bsz, _, d = x.shape
    width = q.shape[2]
    blk = lambda n, c: pl.BlockSpec((1, n, c), lambda bi: (bi, 0, 0))
    const2 = lambda bi: (0, 0)
    return pl.pallas_call(
        functools.partial(_attn_a_kernel, n_sub=1, prompt=False, alpha=alpha),
        grid=(bsz,),
        in_specs=[blk(A_SUB, d), blk(A_SUB, width), blk(A_WIN, width), blk(width, A_WIN),
                  _resident(bias.shape), _resident(wo.shape),
                  pl.BlockSpec((1, d), const2), pl.BlockSpec((1, d), const2)],
        out_specs=blk(A_SUB, d),
        out_shape=jax.ShapeDtypeStruct((bsz, A_SUB, d), F32),
        scratch_shapes=[pltpu.VMEM((width, A_SUB), BF16)],
        compiler_params=_params(1),
        name="attn_a_sample",
    )(x, q, kwin, vtwin, bias, wo, g, b)


def _rel_bias_tables(rel_bias, n_sample):
    r = jnp.arange(A_SUB)[:, None]
    c = jnp.arange(A_WIN)[None, :]
    period = A_SUB + A_WIN
    diff = jnp.arange(period)
    diff = jnp.where(diff < A_WIN, diff, diff - period)
    line = rel_bias[:, jnp.clip(A_PAST - diff, -REL_CLIP, REL_CLIP) + REL_CLIP].astype(F32) * LOG2E
    table = jnp.tile(line, (1, A_SUB))[:, :A_SUB * (period - 1)].reshape(-1, A_SUB, period - 1)[:, :, :A_WIN]
    lo = (r // CHUNK) * CHUNK
    band = (c >= lo) & (c < lo + A_PAST + CHUNK)
    prompt = jnp.where(band[None], table, -jnp.inf)
    live = (c < A_PAST + n_sample)
    sample = jnp.where(live[None], jnp.where((r < n_sample)[None], table, 0.0), -jnp.inf)

    def pair_layout(tab):
        h = tab.shape[0]
        return tab.transpose(0, 2, 1).reshape(h // 2, 2, A_WIN, A_SUB).transpose(0, 2, 1, 3).reshape(h // 2, A_WIN, 2 * A_SUB)

    return pair_layout(prompt), pair_layout(sample)


def _round_to_bf16(x):
    bits = struct.unpack("<I", struct.pack("<f", x))[0]
    bits = (bits + 0x7FFF + ((bits >> 16) & 1)) & 0xFFFF0000
    return struct.unpack("<f", struct.pack("<I", bits))[0]


def _bf16_pieces(x):
    p1 = _round_to_bf16(x)
    p2 = _round_to_bf16(x - p1)
    return (p1, p2, _round_to_bf16(x - p1 - p2))


def _pos_lanes(pos, lane):
    hi = (lax.shift_right_logical(pos, 6) * CHUNK).astype(F32)
    lo = (pos & (CHUNK - 1)).astype(F32)
    return jnp.where(lane < POS_LANE0 + 3, hi, jnp.where(lane < POS_LANE0 + 6, lo, 0.0))


def _proj_b_kernel(x_ref, w_ref, q_ref, qi_ref, ka_ref, kf_ref, vf_ref, kiwi_ref, kd_ref, wit_ref, vt_ref,
                   *, nq, nkv, nqi, pos0, period):
    h = _dot(x_ref[...].astype(BF16), w_ref[...])
    q_ref[...] = (h[:, :nq] * (HEAD_DIM ** -0.5 * LOG2E)).astype(BF16)
    k = h[:, nq:nq + nkv]
    v = h[:, nq + nkv:nq + 2 * nkv]
    kf_ref[...] = k
    vf_ref[...] = v
    o0 = nq + 2 * nkv
    qi_ref[...] = (h[:, o0:o0 + nqi] * (HEAD_DIM ** -0.5)).astype(BF16)
    tail = h[:, o0 + nqi:o0 + nqi + LANES]
    kiwi_ref[...] = tail
    lane = lax.broadcasted_iota(I32, tail.shape, 1)
    kd_ref[...] = jnp.where(lane < HEAD_DIM, tail, pltpu.roll(tail, HEAD_DIM, 1)).astype(BF16)
    wit_ref[...] = tail.T[HEAD_DIM:HEAD_DIM + IDX_HEADS, :] * (IDX_HEADS ** -0.5)
    tm = tail.shape[0]
    row = pl.program_id(0) * tm + lax.broadcasted_iota(I32, (tm, 1), 0)
    pos = _pos_lanes(pos0 + (row & (period - 1)), lane)
    for pair in range(nkv // LANES):
        for half in range(2):
            tile = slice((2 * pair + half) * LANES, (2 * pair + half + 1) * LANES)
            vp = v[:, pair * LANES:(pair + 1) * LANES]
            kp = k[:, pair * LANES:(pair + 1) * LANES]
            if half:
                vp, kp = pltpu.roll(vp, HEAD_DIM, 1), pltpu.roll(kp, HEAD_DIM, 1)
            vt_ref[0, tile, :] = jnp.where(lane < HEAD_DIM, vp, 1.0).T.astype(BF16)
            ka_ref[:, tile] = jnp.where(lane < HEAD_DIM, kp, pos).astype(BF16)


def _proj_b(x, w_pad, nq, nkv, nqi, tm, pos0, period):
    m, d = x.shape
    assert period & (period - 1) == 0
    rows = lambda c: pl.BlockSpec((tm, c), lambda i: (i, 0))
    outs = [(nq, BF16), (nqi, BF16), (2 * nkv, BF16), (nkv, F32), (nkv, F32), (LANES, F32), (LANES, BF16)]
    return pl.pallas_call(
        functools.partial(_proj_b_kernel, nq=nq, nkv=nkv, nqi=nqi, pos0=pos0, period=period),
        grid=(m // tm,),
        in_specs=[rows(d), _resident(w_pad.shape)],
        out_specs=[rows(c) for c, _ in outs] + [pl.BlockSpec((IDX_HEADS, tm), lambda i: (0, i)),
                                                 pl.BlockSpec((1, 2 * nkv, tm), lambda i: (i, 0, 0))],
        out_shape=[jax.ShapeDtypeStruct((m, c), dt) for c, dt in outs]
        + [jax.ShapeDtypeStruct((IDX_HEADS, m), F32), jax.ShapeDtypeStruct((m // tm, 2 * nkv, tm), BF16)],
        compiler_params=_params(1),
        name="proj_b",
    )(x, w_pad)


def _sparse_kernel(x_ref, q_ref, qi_ref, wit_ref, ka_ref, vt_ref, kd_ref, wo_ref, g_ref, b_ref, o_ref,
                   key_scr, hi_scr, lo_scr, bias_scr, qt_scr, s_scr, acc_scr, m_scr, ot_scr,
                   *, tq, kb, pos0, n_keys, n_sel, n_heads, group, slopes, alpha):
    t = pl.program_id(1)
    q0 = pos0 + t * tq
    qpos_row = q0 + lax.broadcasted_iota(I32, (1, tq), 1)
    lim_row = jnp.minimum((lax.shift_right_logical(qpos_row, 6) + 1) * CHUNK, n_keys)
    kmax = jnp.minimum((lax.shift_right_logical(q0 + tq - 1, 6) + 1) * CHUNK, n_keys)
    nkb = lax.shift_right_logical(kmax + kb - 1, int(math.log2(kb)))
    lane = lax.broadcasted_iota(I32, (tq, LANES), 1)
    lo_half = lane < HEAD_DIM

    def key_block(kbi):
        return pl.multiple_of(kbi * kb, kb)

    def key_index(off):
        return off + lax.broadcasted_iota(I32, (kb, tq), 0)

    qis = []
    for p in range(IDX_HEADS // 2):
        q2 = qi_ref[0, :, p * LANES:(p + 1) * LANES]
        qis.append(jnp.where(lo_half, q2, jnp.zeros_like(q2)))
        qis.append(jnp.where(lo_half, jnp.zeros_like(q2), q2))
    wis = wit_ref[...]
    qi_all = jnp.concatenate(qis, axis=0)

    def score_block(kbi, carry):
        off = key_block(kbi)
        dots = _dot_nt(kd_ref[0, pl.ds(off, kb), :], qi_all)
        acc = jnp.zeros((kb, tq), F32)
        for h in range(IDX_HEADS):
            acc = acc + jnp.maximum(dots[:, h * tq:(h + 1) * tq], 0.0) * wis[h:h + 1, :]
        acc = jnp.where(key_index(off) < lim_row, acc, -jnp.inf)
        bits = lax.bitcast_convert_type(acc, I32)
        key = bits ^ (lax.shift_right_arithmetic(bits, 31) & 0x7FFFFFFF)
        key_scr[pl.ds(off, kb), :] = key
        hi_scr[pl.ds(off, kb), :] = lax.shift_right_arithmetic(key, 16).astype(I16)
        lo_scr[pl.ds(off, kb), :] = ((key & 0xFFFF) + I16_MIN).astype(I16)
        return carry

    lax.fori_loop(0, nkb, score_block, 0)

    @pl.when(nkb % 2 == 1)
    def _():
        key_scr[pl.ds(key_block(nkb), kb), :] = jnp.full((kb, tq), INT_MIN, I32)
        hi_scr[pl.ds(key_block(nkb), kb), :] = jnp.full((kb, tq), I16_MIN, I16)
        lo_scr[pl.ds(key_block(nkb), kb), :] = jnp.full((kb, tq), I16_MIN, I16)

    n_steps = lax.shift_right_logical(nkb + 1, 1)

    def step_rows(i):
        return pl.ds(pl.multiple_of(i * 2 * kb, 2 * kb), 2 * kb)

    def count(pred):
        def body(i, part):
            hit = pred(key_scr[step_rows(i), :])
            return part + jnp.sum(jnp.where(hit, 1, 0).reshape(-1, COUNT_ROWS, tq), axis=0)
        part = lax.fori_loop(0, n_steps, body, jnp.zeros((COUNT_ROWS, tq), I32))
        return jnp.sum(part, axis=0, keepdims=True)

    def count16(ref, pred):
        def body(i, parts):
            ind = jnp.where(pred(ref[step_rows(i), :]), jnp.int16(1), jnp.int16(0))
            parts = list(parts)
            for j, r in enumerate(range(0, 2 * kb, COUNT16_ROWS)):
                parts[j % len(parts)] = parts[j % len(parts)] + ind[r:r + COUNT16_ROWS]
            return tuple(parts)
        parts = lax.fori_loop(0, n_steps, body, (jnp.zeros((COUNT16_ROWS, tq), I16),) * COUNT16_CHAINS)
        return jnp.sum(sum(p.astype(I32) for p in parts), axis=0, keepdims=True)

    def as_i16(row):
        return jnp.broadcast_to(row, (COUNT16_ROWS, tq)).astype(I16)[:1]

    def bisect16(ref, want):
        def body(i, thr):
            cand = thr + lax.shift_left(jnp.int32(1), 15 - i)
            cand16 = as_i16(cand)
            return jnp.where(count16(ref, lambda v: v >= cand16) >= want, cand, thr)
        return lax.fori_loop(0, 16, body, jnp.full((1, tq), I16_MIN, I32))

    thr_hi = bisect16(hi_scr, n_sel)
    thr_hi16 = as_i16(thr_hi)
    n_hi_gt = count16(hi_scr, lambda v: v > thr_hi16)

    def keep_bucket(i, carry):
        rows = step_rows(i)
        lo_scr[rows, :] = jnp.where(hi_scr[rows, :] == thr_hi16, lo_scr[rows, :], jnp.int16(I16_MIN))
        return carry

    lax.fori_loop(0, n_steps, keep_bucket, 0)
    thr = thr_hi * 65536 + (bisect16(lo_scr, n_sel - n_hi_gt) - I16_MIN)
    n_gt = count(lambda key: key > thr)
    n_eq = count(lambda key: key == thr)
    need = n_sel - n_gt
    tie = jnp.max(jnp.where((n_eq > need) & (thr > NEG_INF_KEY), 1, 0)) > 0

    @pl.when(jnp.logical_not(tie))
    def _():
        def body(kbi, carry):
            off = key_block(kbi)
            sel = (key_scr[pl.ds(off, kb), :] >= thr) & (key_index(off) < lim_row)
            bias_scr[pl.ds(off, kb), :] = jnp.where(sel, 0.0, -MASK_BIAS)
            return carry
        lax.fori_loop(0, nkb, body, 0)

    @pl.when(tie)
    def _():
        tri = jnp.where(lax.broadcasted_iota(I32, (kb, kb), 0) > lax.broadcasted_iota(I32, (kb, kb), 1),
                        1.0, 0.0).astype(BF16)
        need_f = need.astype(F32)

        def body(kbi, seen):
            off = key_block(kbi)
            key = key_scr[pl.ds(off, kb), :]
            eq = jnp.where(key == thr, 1.0, 0.0)
            earlier = _dot(tri, eq.astype(BF16)) + seen
            sel = ((key > thr) | ((key == thr) & (earlier < need_f))) & (key_index(off) < lim_row)
            bias_scr[pl.ds(off, kb), :] = jnp.where(sel, 0.0, -MASK_BIAS)
            return seen + jnp.sum(eq, axis=0, keepdims=True)
        lax.fori_loop(0, nkb, body, jnp.zeros((1, tq), F32))

    n_units = n_heads // UNIT_HEADS
    kv_lanes = lambda u: slice((u * UNIT_HEADS // group) * LANES, (u * UNIT_HEADS // group + 1) * LANES)
    for h in range(n_heads):
        q2 = q_ref[0, :, (h // 2) * LANES:(h // 2 + 1) * LANES].astype(F32)
        if h % 2:
            q2 = pltpu.roll(q2, HEAD_DIM, 1)
        sl = jnp.zeros((1, LANES), F32)
        for i, piece in enumerate(_bf16_pieces(slopes[h] * LOG2E) * 2):
            sl = jnp.where(lane[:1] == POS_LANE0 + i, piece, sl)
        j = h % UNIT_HEADS
        qt_scr[h // UNIT_HEADS, :, j * tq:(j + 1) * tq] = jnp.where(lo_half, q2, sl).T.astype(BF16)
    m_scr[...] = jnp.full(m_scr.shape, -jnp.inf, F32)
    acc_scr[...] = jnp.zeros(acc_scr.shape, F32)

    def scores(kbi, u, last):
        off = key_block(kbi)
        bias = jnp.concatenate([bias_scr[pl.ds(off, kb), :]] * UNIT_HEADS, axis=1)
        s = _dot(ka_ref[0, pl.ds(off, kb), kv_lanes(u)], qt_scr[u]) + bias
        if last:
            ahead = jnp.maximum(key_index(off) - qpos_row, 0).astype(F32)
            s = s - jnp.concatenate([(2.0 * slopes[u * UNIT_HEADS + j] * LOG2E) * ahead for j in range(UNIT_HEADS)], axis=1)
        s_scr[u] = s

    def absorb(kbi, u):
        s = s_scr[u]
        m_old = m_scr[u]
        m_new = jnp.maximum(m_old, jnp.max(s, axis=0, keepdims=True))
        p = jnp.exp2((s - m_new).astype(BF16))
        acc_scr[u] = acc_scr[u] * jnp.exp2(m_old - m_new) + _dot(vt_ref[0, kbi, kv_lanes(u), :], p)
        m_scr[u] = m_new

    def step(kbi, last):
        for u in range(n_units):
            absorb(kbi - 1, u)
            scores(kbi, u, last)

    def first_scores(last):
        for u in range(n_units):
            scores(0, u, last)

    def step_body(kbi, carry):
        step(kbi, False)
        return carry

    pl.when(nkb == 1)(lambda: first_scores(True))
    pl.when(nkb > 1)(lambda: first_scores(False))
    lax.fori_loop(1, nkb - 1, step_body, 0)
    pl.when(nkb > 1)(lambda: step(nkb - 1, True))
    for u in range(n_units):
        absorb(nkb - 1, u)

    for u in range(n_units):
        acc = acc_scr[u]
        o_t = acc[:HEAD_DIM] / acc[HEAD_DIM:HEAD_DIM + 1]
        for j in range(UNIT_HEADS):
            h = u * UNIT_HEADS + j
            ot_scr[h * HEAD_DIM:(h + 1) * HEAD_DIM, :] = o_t[:, j * tq:(j + 1) * tq].astype(BF16)
    y = lax.dot_general(ot_scr[...], wo_ref[...], (((0,), (0,)), ((), ())), preferred_element_type=F32)
    o_ref[0] = _layer_norm(alpha * x_ref[0] + y, g_ref[...], b_ref[...])


def _sparse_attn(x, q, qi, wit, k, vt, kd, wo, g, b, *, tq, kb, pos0, n_keys, alpha):
    bsz, s, d = x.shape
    n_t = s // tq
    l_pad = k.shape[1]
    n_heads = q.shape[2] // HEAD_DIM
    n_kv = k.shape[2] // LANES
    group = n_heads // n_kv
    assert kb % tq == 0 and pos0 % tq == 0 and l_pad % kb == 0 and vt.shape == (bsz, l_pad // kb, k.shape[2], kb)
    slopes = tuple(2.0 ** (-8.0 * (h + 1) / n_heads) for h in range(n_heads))
    qrow = lambda c: pl.BlockSpec((1, tq, c), lambda bi, t: (bi, t, 0))
    keys = lambda c: pl.BlockSpec((1, l_pad, c), lambda bi, t: (bi, 0, 0), pipeline_mode=pl.Buffered(1))
    const2 = lambda bi, t: (0, 0)
    sel_rows = -(-l_pad // (2 * kb)) * 2 * kb
    assert group % UNIT_HEADS == 0
    n_units, unit_w = n_heads // UNIT_HEADS, UNIT_HEADS * tq
    kern = functools.partial(_sparse_kernel, tq=tq, kb=kb, pos0=pos0, n_keys=n_keys, n_sel=min(TOPK_MAX, n_keys // 4),
                             n_heads=n_heads, group=group, slopes=slopes, alpha=alpha)
    return pl.pallas_call(
        kern,
        grid=(bsz, n_t),
        in_specs=[qrow(d), qrow(q.shape[2]), qrow(qi.shape[2]),
                  pl.BlockSpec((IDX_HEADS, tq), lambda bi, t: (0, bi * n_t + t)),
                  keys(k.shape[2]),
                  pl.BlockSpec((1,) + vt.shape[1:], lambda bi, t: (bi, 0, 0, 0), pipeline_mode=pl.Buffered(1)),
                  keys(kd.shape[2]),
                  _resident(wo.shape), pl.BlockSpec((1, d), const2), pl.BlockSpec((1, d), const2)],
        out_specs=qrow(d),
        out_shape=jax.ShapeDtypeStruct((bsz, s, d), F32),
        scratch_shapes=[pltpu.VMEM((sel_rows, tq), I32), pltpu.VMEM((sel_rows, tq), I16), pltpu.VMEM((sel_rows, tq), I16),
                        pltpu.VMEM((l_pad, tq), F32),
                        pltpu.VMEM((n_units, LANES, unit_w), BF16), pltpu.VMEM((n_units, kb, unit_w), F32),
                        pltpu.VMEM((n_units, LANES, unit_w), F32),
                        pltpu.VMEM((n_units, 1, unit_w), F32), pltpu.VMEM((q.shape[2], tq), BF16)],
        compiler_params=_params(2),
        name="sparse_attn",
    )(x, q, qi, wit, k, vt, kd, wo, g, b)


def _pad_rows(a, n):
    return jnp.pad(a, ((0, 0), (0, n - a.shape[1]), (0, 0)))


def _mixer_a(xp, xs, cache_k, cache_v, w_in, rel_bias, w_out, g, b, alpha):
    bsz, s, d = xp.shape
    dbs, t, _ = xs.shape
    width = w_out.shape[0]
    w_in = w_in.astype(BF16)
    w_out = w_out.astype(BF16)
    bias_p, bias_s = _rel_bias_tables(rel_bias, t)
    keep = min(A_PAST, s)
    q, k, vt, kf, vf = _proj_a(xp, w_in, keep)
    yp = _attn_a_prompt(xp, q, k, vt, bias_p, w_out, g, b, alpha)
    qs, ks, _, ksf, vsf = _proj_a(xs.reshape(1, dbs * t, d), w_in, dbs * t)
    n_cache = cache_k.shape[1]
    kwin = _pad_rows(jnp.concatenate([cache_k.reshape(dbs, n_cache, width).astype(BF16), ks.reshape(dbs, t, width)], 1), A_WIN)
    vwin = _pad_rows(jnp.concatenate([cache_v.reshape(dbs, n_cache, width), vsf.reshape(dbs, t, width)], 1).astype(BF16), A_WIN)
    ys = _attn_a_sample(_pad_rows(xs, A_SUB), _pad_rows(qs.reshape(dbs, t, width), A_SUB), kwin, vwin.transpose(0, 2, 1),
                        bias_s, w_out, g, b, alpha)[:, :t]
    heads = width // HEAD_DIM
    return (yp, ys, kf.reshape(bsz, keep, heads, HEAD_DIM), vf.reshape(bsz, keep, heads, HEAD_DIM),
            ksf.reshape(dbs, t, heads, HEAD_DIM), vsf.reshape(dbs, t, heads, HEAD_DIM))


def _mixer_b(xp, xs, cache_k, cache_v, cache_idx, w_in, w_out, g, b, alpha):
    bsz, s, d = xp.shape
    dbs, t, _ = xs.shape
    nq = w_out.shape[0]
    nkv = cache_k.shape[2] * cache_k.shape[3]
    nqi = IDX_HEADS * HEAD_DIM
    kvh = nkv // HEAD_DIM
    past = cache_k.shape[1]
    w_pad = jnp.pad(w_in, ((0, 0), (0, nq + 2 * nkv + nqi + LANES - w_in.shape[1]))).astype(BF16)
    w_out = w_out.astype(BF16)
    kb = 512

    q, qi, k, kf, vf, kiwi, kd, wit, vt = _proj_b(xp.reshape(bsz * s, d), w_pad, nq, nkv, nqi, kb, 0, s)
    r3 = lambda a: a.reshape(bsz, s, a.shape[1])
    yp = _sparse_attn(xp, r3(q), r3(qi), wit, r3(k), vt.reshape(bsz, s // kb, 2 * nkv, kb), r3(kd), w_out, g, b,
                      tq=256, kb=kb, pos0=0, n_keys=s, alpha=alpha)

    qs, qis, ks, ksf, vsf, kiwis, kds, wits, _ = _proj_b(xs.reshape(dbs * t, d), w_pad, nq, nkv, nqi, dbs * t, past, t)
    tq_s = LANES
    n_keys = past + t
    l_pad = -(-n_keys // kb) * kb
    s3 = lambda a: a.reshape(dbs, t, a.shape[1])
    v_all = jnp.concatenate([cache_v, vsf.reshape(dbs, t, kvh, HEAD_DIM)], axis=1).astype(BF16)
    v_all = jnp.concatenate([v_all, jnp.ones_like(v_all)], axis=-1).reshape(dbs, n_keys, 2 * nkv)
    keys_vt = _pad_rows(v_all, l_pad).reshape(dbs, l_pad // kb, kb, 2 * nkv).transpose(0, 1, 3, 2)
    kd_cache = jnp.concatenate([cache_idx, cache_idx], axis=-1).astype(BF16)
    frame = jnp.arange(past, dtype=I32)[:, None]
    pos = _pos_lanes(frame, HEAD_DIM + jnp.arange(HEAD_DIM, dtype=I32)[None, :]).astype(BF16)
    pos = jnp.broadcast_to(pos[None, :, None, :], (dbs, past, kvh, HEAD_DIM))
    k_cache = jnp.concatenate([cache_k.astype(BF16), pos], axis=-1).reshape(dbs, past, 2 * nkv)
    keys_k = _pad_rows(jnp.concatenate([k_cache, s3(ks)], 1), l_pad)
    keys_kd = _pad_rows(jnp.concatenate([kd_cache, s3(kds)], 1), l_pad)
    wit_pad = jnp.pad(wits.reshape(IDX_HEADS, dbs, t), ((0, 0), (0, 0), (0, tq_s - t))).reshape(IDX_HEADS, dbs * tq_s)
    ys = _sparse_attn(_pad_rows(xs, tq_s), _pad_rows(s3(qs), tq_s), _pad_rows(s3(qis), tq_s), wit_pad,
                      keys_k, keys_vt, keys_kd, w_out, g, b,
                      tq=tq_s, kb=kb, pos0=past, n_keys=n_keys, alpha=alpha)[:, :t]
    return (yp, ys,
            kf.reshape(bsz, s, kvh, HEAD_DIM), vf.reshape(bsz, s, kvh, HEAD_DIM), kiwi[:, :HEAD_DIM].reshape(bsz, s, HEAD_DIM),
            ksf.reshape(dbs, t, kvh, HEAD_DIM), vsf.reshape(dbs, t, kvh, HEAD_DIM), kiwis[:, :HEAD_DIM].reshape(dbs, t, HEAD_DIM))


def kernel(x_prompt, x_sample, cache_a_k, cache_a_v, cache_b_k, cache_b_v, cache_b_idx, ln_g, ln_b, ffn_w_gate, ffn_w_up, ffn_w_down, a_w_in, a_rel_bias, a_w_out, b_w_in, b_w_out):
    depth = ln_g.shape[0]
    alpha = (2.0 * depth) ** 0.25
    bsz, s, d = x_prompt.shape
    dbs, t, _ = x_sample.shape
    xp, xs = x_prompt, x_sample
    a_out, b_out = [], []

    def ffn(x, layer, i, tm):
        shp = x.shape
        y = _ffn_block(x.reshape(-1, d), ffn_w_gate[layer, i].astype(BF16), ffn_w_up[layer, i].astype(BF16),
                       ffn_w_down[layer, i].astype(BF16), ln_g[layer, 2 * i][None], ln_b[layer, 2 * i][None], alpha, tm)
        return y.reshape(shp)

    for layer in range(depth):
        j = layer // 2
        xp = ffn(xp, layer, 0, 512)
        xs = ffn(xs, layer, 0, dbs * t)
        g, b = ln_g[layer, 1][None], ln_b[layer, 1][None]
        if layer % 2 == 0:
            xp, xs, *rest = _mixer_a(xp, xs, cache_a_k[j], cache_a_v[j], a_w_in[j], a_rel_bias[j], a_w_out[j], g, b, alpha)
            a_out.append(rest)
        else:
            xp, xs, *rest = _mixer_b(xp, xs, cache_b_k[j], cache_b_v[j], cache_b_idx[j], b_w_in[j], b_w_out[j], g, b, alpha)
            b_out.append(rest)
        xp = ffn(xp, layer, 1, 512)
        xs = ffn(xs, layer, 1, dbs * t)

    stack = lambda outs, i: jnp.stack([o[i] for o in outs], 0)
    return (xp, xs,
            stack(a_out, 0), stack(a_out, 1), stack(a_out, 2), stack(a_out, 3),
            stack(b_out, 0), stack(b_out, 1), stack(b_out, 2), stack(b_out, 3), stack(b_out, 4), stack(b_out, 5))
```

```python
import functools
import math
import struct

import jax
import jax.numpy as jnp
from jax import lax
from jax.experimental import pallas as pl
from jax.experimental.pallas import tpu as pltpu

F32 = jnp.float32
BF16 = jnp.bfloat16
I32 = jnp.int32
I16 = jnp.int16

CHUNK = 64
A_PAST = 512
REL_CLIP = 128
HEAD_DIM = 64
LANES = 128
IDX_HEADS = 8
TOPK_MAX = 256
LN_EPS = 1e-5
A_SUB = 4 * CHUNK
A_WIN = A_PAST + A_SUB
VMEM_LIMIT = 52 * 1024 * 1024

LOG2E = 1.4426950408889634
POS_LANE0 = HEAD_DIM
MASK_BIAS = 1e30
ONES_ROWS = 16
UNIT_HEADS = 2
COUNT_ROWS = 16

NT_DIMS = (((1,), (1,)), ((), ()))
INT_MIN = -2147483648
I16_MIN = -32768
COUNT16_ROWS = 16
COUNT16_CHAINS = 4
NEG_INF_KEY = -2139095041


def _params(n_grid, flags=None):
    return pltpu.CompilerParams(dimension_semantics=("arbitrary",) * n_grid,
                                vmem_limit_bytes=VMEM_LIMIT, flags=flags)


def _resident(shape):
    zeros = (0,) * len(shape)
    return pl.BlockSpec(shape, lambda *_: zeros, pipeline_mode=pl.Buffered(1))


def _layer_norm(z, g, b):
    mu = jnp.mean(z, axis=-1, keepdims=True)
    d = z - mu
    var = jnp.mean(d * d, axis=-1, keepdims=True)
    return d * lax.rsqrt(var + LN_EPS) * g + b


def _dot(a, b):
    return jnp.dot(a, b, preferred_element_type=F32)


def _dot_nt(a, b):
    return lax.dot_general(a, b, NT_DIMS, preferred_element_type=F32)


def _ffn_kernel(x_ref, wg_ref, wu_ref, wd_ref, g_ref, b_ref, o_ref, h_scr, *, n_chunks, alpha):
    x = x_ref[...]
    xb = x.astype(BF16)
    cf = wg_ref.shape[1] // n_chunks
    for c in range(n_chunks):
        sl = slice(c * cf, (c + 1) * cf)
        gate = _dot(xb, wg_ref[:, sl])
        up = _dot(xb, wu_ref[:, sl])
        h_scr[:, sl] = (gate * (1.0 / (1.0 + jnp.exp(-gate))) * up).astype(BF16)
    y = _dot(h_scr[...], wd_ref[...])
    o_ref[...] = _layer_norm(alpha * x + 0.5 * y, g_ref[...], b_ref[...])


def _ffn_block(x, wg, wu, wd, g, b, alpha, tm):
    m, d = x.shape
    f = wg.shape[1]
    const = lambda i: (0, 0)
    return pl.pallas_call(
        functools.partial(_ffn_kernel, n_chunks=4, alpha=alpha),
        grid=(m // tm,),
        in_specs=[pl.BlockSpec((tm, d), lambda i: (i, 0)),
                  _resident((d, f)), _resident((d, f)), _resident((f, d)),
                  pl.BlockSpec((1, d), const), pl.BlockSpec((1, d), const)],
        out_specs=pl.BlockSpec((tm, d), lambda i: (i, 0)),
        out_shape=jax.ShapeDtypeStruct((m, d), F32),
        scratch_shapes=[pltpu.VMEM((tm, f), BF16)],
        compiler_params=_params(1),
        name="ffn_ln",
    )(x, wg, wu, wd, g, b)


def _proj_a_kernel(x_ref, w_ref, q_ref, k_ref, vt_ref, kf_ref, vf_ref, *, width):
    h = _dot(x_ref[0].astype(BF16), w_ref[...])
    k = h[:, width:2 * width]
    v = h[:, 2 * width:]
    q_ref[0] = (h[:, :width] * (HEAD_DIM ** -0.5 * LOG2E)).astype(BF16)
    k_ref[0] = k.astype(BF16)
    vt_ref[0] = v.T.astype(BF16)

    @pl.when(pl.program_id(1) == pl.num_programs(1) - 1)
    def _():
        kf_ref[0] = k
        vf_ref[0] = v


def _proj_a(x, w, tm):
    bsz, s, d = x.shape
    width = w.shape[1] // 3
    row = pl.BlockSpec((1, tm, width), lambda b, t: (b, t, 0))
    last = pl.BlockSpec((1, tm, width), lambda b, t: (b, 0, 0))
    return pl.pallas_call(
        functools.partial(_proj_a_kernel, width=width),
        grid=(bsz, s // tm),
        in_specs=[pl.BlockSpec((1, tm, d), lambda b, t: (b, t, 0)), _resident(w.shape)],
        out_specs=[row, row, pl.BlockSpec((1, width, tm), lambda b, t: (b, 0, t)), last, last],
        out_shape=[jax.ShapeDtypeStruct((bsz, s, width), BF16)] * 2 + [jax.ShapeDtypeStruct((bsz, width, s), BF16)]
        + [jax.ShapeDtypeStruct((bsz, tm, width), F32)] * 2,
        compiler_params=_params(2),
        name="proj_a",
    )(x, w)


def _attn_a_kernel(*refs, n_sub, prompt, alpha):
    if prompt:
        (x_ref, q_ref, kp_ref, kc_ref, vtp_ref, vtc_ref, bias_ref, wo_ref, g_ref, b_ref,
         o_ref, kwin, q_scr, s_scr, ot_scr, vtwin) = refs
        vtwin[:, 0:A_PAST] = vtp_ref[0]
        vtwin[:, A_PAST:2 * A_PAST] = vtc_ref[0]
        vt_at = lambda rows, r0: vtwin[rows, r0:r0 + A_WIN]
    else:
        x_ref, q_ref, k_ref, vt_ref, bias_ref, wo_ref, g_ref, b_ref, o_ref, kwin, q_scr, s_scr, ot_scr = refs
        vt_at = lambda rows, r0: vt_ref[0, rows, r0:r0 + A_WIN]
    n_pairs = q_ref.shape[2] // LANES
    for p in range(n_pairs):
        cols = slice(p * LANES, (p + 1) * LANES)
        q_scr[p] = q_ref[0, :, cols]
        if prompt:
            kwin[p, 0:A_PAST] = kp_ref[0, :, cols]
            kwin[p, A_PAST:2 * A_PAST] = kc_ref[0, :, cols]
        else:
            kwin[p] = k_ref[0, :, cols]
    lane = lax.broadcasted_iota(I32, (A_SUB, LANES), 1)
    key_row = lax.broadcasted_iota(I32, (A_WIN, 2 * A_SUB), 0)
    out_row = lax.broadcasted_iota(I32, (LANES, A_SUB), 0)
    ones_rows = jnp.ones((ONES_ROWS, A_WIN), BF16)

    def scores(p, j, side, first_valid):
        r0 = j * A_SUB
        q2 = q_scr[p, r0:r0 + A_SUB, :].astype(F32)
        qt = jnp.concatenate([jnp.where(lane < HEAD_DIM, q2, 0.0).T, jnp.where(lane < HEAD_DIM, 0.0, q2).T], axis=1)
        s = _dot(kwin[p, r0:r0 + A_WIN, :], qt.astype(BF16)) + bias_ref[p]
        if first_valid > r0:
            s = jnp.where(key_row >= first_valid - r0, s, -jnp.inf)
        s_scr[side, j] = s

    def absorb(p, j, side):
        r0 = j * A_SUB
        s = s_scr[side, j]
        e = jnp.exp2((s - jnp.max(s, axis=0, keepdims=True)).astype(BF16))
        rows = pl.ds(p * LANES, LANES)
        o = _dot(jnp.concatenate([vt_at(rows, r0), ones_rows], axis=0), e)
        o = o[:LANES] / o[LANES:LANES + 1]
        ot_scr[rows, r0:r0 + A_SUB] = jnp.where(out_row < HEAD_DIM, o[:, :A_SUB], o[:, A_SUB:]).astype(BF16)

    def overlap(p_scores, p_absorb, side, first_valid):
        for j in range(n_sub):
            scores(p_scores, j, side, first_valid)
        for j in range(n_sub):
            absorb(p_absorb, j, 1 - side)

    def attend(first_valid):
        assert n_pairs % 2 == 0
        for j in range(n_sub):
            scores(0, j, 0, first_valid)

        def two_pairs(i, carry):
            overlap(2 * i + 1, 2 * i, 1, first_valid)
            overlap(2 * i + 2, 2 * i + 1, 0, first_valid)
            return carry

        for i in range(n_pairs // 2 - 1):
            two_pairs(i, 0)
        overlap(n_pairs - 1, n_pairs - 2, 1, first_valid)
        for j in range(n_sub):
            absorb(n_pairs - 1, j, 1)

    if prompt:
        pl.when(pl.program_id(1) == 0)(lambda: attend(A_PAST))
        pl.when(pl.program_id(1) > 0)(lambda: attend(0))
    else:
        attend(0)

    y = lax.dot_general(ot_scr[...], wo_ref[...], (((0,), (0,)), ((), ())), preferred_element_type=F32)
    o_ref[0] = _layer_norm(alpha * x_ref[0] + y, g_ref[...], b_ref[...])


def _attn_a_scratch(width, n_keys, n_queries):
    pairs = width // LANES
    return [pltpu.VMEM((pairs, n_keys, LANES), BF16), pltpu.VMEM((pairs, n_queries, LANES), BF16),
            pltpu.VMEM((2, n_queries // A_SUB, A_WIN, 2 * A_SUB), F32), pltpu.VMEM((width, n_queries), BF16)]


def _attn_a_prompt(x, q, k, vt, bias, wo, g, b, alpha):
    bsz, s, d = x.shape
    width = q.shape[2]
    tq = A_PAST
    cur = lambda bi, t: (bi, t, 0)
    prev = lambda bi, t: (bi, jnp.maximum(t - 1, 0), 0)
    cur_t = lambda bi, t: (bi, 0, t)
    prev_t = lambda bi, t: (bi, 0, jnp.maximum(t - 1, 0))
    const2 = lambda bi, t: (0, 0)
    return pl.pallas_call(
        functools.partial(_attn_a_kernel, n_sub=tq // A_SUB, prompt=True, alpha=alpha),
        grid=(bsz, s // tq),
        in_specs=[pl.BlockSpec((1, tq, d), cur), pl.BlockSpec((1, tq, width), cur),
                  pl.BlockSpec((1, tq, width), prev), pl.BlockSpec((1, tq, width), cur),
                  pl.BlockSpec((1, width, tq), prev_t), pl.BlockSpec((1, width, tq), cur_t),
                  _resident(bias.shape), _resident(wo.shape),
                  pl.BlockSpec((1, d), const2), pl.BlockSpec((1, d), const2)],
        out_specs=pl.BlockSpec((1, tq, d), cur),
        out_shape=jax.ShapeDtypeStruct((bsz, s, d), F32),
        scratch_shapes=_attn_a_scratch(width, 2 * tq, tq) + [pltpu.VMEM((width, 2 * tq), BF16)],
        compiler_params=_params(2),
        name="attn_a_prompt",
    )(x, q, k, k, vt, vt, bias, wo, g, b)


def _attn_a_sample(x, q, kwin, vtwin, bias, wo, g, b, alpha):
    bsz, _, d = x.shape
    width = q.shape[2]
    blk = lambda n, c: pl.BlockSpec((1, n, c), lambda bi: (bi, 0, 0))
    const2 = lambda bi: (0, 0)
    return pl.pallas_call(
        functools.partial(_attn_a_kernel, n_sub=1, prompt=False, alpha=alpha),
        grid=(bsz,),
        in_specs=[blk(A_SUB, d), blk(A_SUB, width), blk(A_WIN, width), blk(width, A_WIN),
                  _resident(bias.shape), _resident(wo.shape),
                  pl.BlockSpec((1, d), const2), pl.BlockSpec((1, d), const2)],
        out_specs=blk(A_SUB, d),
        out_shape=jax.ShapeDtypeStruct((bsz, A_SUB, d), F32),
        scratch_shapes=_attn_a_scratch(width, A_WIN, A_SUB),
        compiler_params=_params(1),
        name="attn_a_sample",
    )(x, q, kwin, vtwin, bias, wo, g, b)


def _rel_bias_tables(rel_bias, n_sample):
    r = jnp.arange(A_SUB)[:, None]
    c = jnp.arange(A_WIN)[None, :]
    period = A_SUB + A_WIN
    diff = jnp.arange(period)
    diff = jnp.where(diff < A_WIN, diff, diff - period)
    line = rel_bias[:, jnp.clip(A_PAST - diff, -REL_CLIP, REL_CLIP) + REL_CLIP].astype(F32) * LOG2E
    table = jnp.tile(line, (1, A_SUB))[:, :A_SUB * (period - 1)].reshape(-1, A_SUB, period - 1)[:, :, :A_WIN]
    lo = (r // CHUNK) * CHUNK
    band = (c >= lo) & (c < lo + A_PAST + CHUNK)
    prompt = jnp.where(band[None], table, -jnp.inf)
    live = (c < A_PAST + n_sample)
    sample = jnp.where(live[None], jnp.where((r < n_sample)[None], table, 0.0), -jnp.inf)

    def pair_layout(tab):
        h = tab.shape[0]
        return tab.transpose(0, 2, 1).reshape(h // 2, 2, A_WIN, A_SUB).transpose(0, 2, 1, 3).reshape(h // 2, A_WIN, 2 * A_SUB)

    return pair_layout(prompt), pair_layout(sample)


def _round_to_bf16(x):
    bits = struct.unpack("<I", struct.pack("<f", x))[0]
    bits = (bits + 0x7FFF + ((bits >> 16) & 1)) & 0xFFFF0000
    return struct.unpack("<f", struct.pack("<I", bits))[0]


def _bf16_pieces(x):
    p1 = _round_to_bf16(x)
    p2 = _round_to_bf16(x - p1)
    return (p1, p2, _round_to_bf16(x - p1 - p2))


def _pos_lanes(pos, lane):
    hi = (lax.shift_right_logical(pos, 6) * CHUNK).astype(F32)
    lo = (pos & (CHUNK - 1)).astype(F32)
    return jnp.where(lane < POS_LANE0 + 3, hi, jnp.where(lane < POS_LANE0 + 6, lo, 0.0))


def _proj_b_kernel(x_ref, w_ref, q_ref, qi_ref, ka_ref, kf_ref, vf_ref, kiwi_ref, kd_ref, wit_ref, vt_ref,
                   *, nq, nkv, nqi, pos0, period):
    h = _dot(x_ref[...].astype(BF16), w_ref[...])
    q_ref[...] = (h[:, :nq] * (HEAD_DIM ** -0.5 * LOG2E)).astype(BF16)
    k = h[:, nq:nq + nkv]
    v = h[:, nq + nkv:nq + 2 * nkv]
    kf_ref[...] = k
    vf_ref[...] = v
    o0 = nq + 2 * nkv
    qi_ref[...] = (h[:, o0:o0 + nqi] * (HEAD_DIM ** -0.5)).astype(BF16)
    tail = h[:, o0 + nqi:o0 + nqi + LANES]
    kiwi_ref[...] = tail
    lane = lax.broadcasted_iota(I32, tail.shape, 1)
    kd_ref[...] = jnp.where(lane < HEAD_DIM, tail, pltpu.roll(tail, HEAD_DIM, 1)).astype(BF16)
    wit_ref[...] = tail.T[HEAD_DIM:HEAD_DIM + IDX_HEADS, :] * (IDX_HEADS ** -0.5)
    tm = tail.shape[0]
    row = pl.program_id(0) * tm + lax.broadcasted_iota(I32, (tm, 1), 0)
    pos = _pos_lanes(pos0 + (row & (period - 1)), lane)
    for pair in range(nkv // LANES):
        for half in range(2):
            tile = slice((2 * pair + half) * LANES, (2 * pair + half + 1) * LANES)
            vp = v[:, pair * LANES:(pair + 1) * LANES]
            kp = k[:, pair * LANES:(pair + 1) * LANES]
            if half:
                vp, kp = pltpu.roll(vp, HEAD_DIM, 1), pltpu.roll(kp, HEAD_DIM, 1)
            vt_ref[0, tile, :] = jnp.where(lane < HEAD_DIM, vp, 1.0).T.astype(BF16)
            ka_ref[:, tile] = jnp.where(lane < HEAD_DIM, kp, pos).astype(BF16)


def _proj_b(x, w_pad, nq, nkv, nqi, tm, pos0, period):
    m, d = x.shape
    assert period & (period - 1) == 0
    rows = lambda c: pl.BlockSpec((tm, c), lambda i: (i, 0))
    outs = [(nq, BF16), (nqi, BF16), (2 * nkv, BF16), (nkv, F32), (nkv, F32), (LANES, F32), (LANES, BF16)]
    return pl.pallas_call(
        functools.partial(_proj_b_kernel, nq=nq, nkv=nkv, nqi=nqi, pos0=pos0, period=period),
        grid=(m // tm,),
        in_specs=[rows(d), _resident(w_pad.shape)],
        out_specs=[rows(c) for c, _ in outs] + [pl.BlockSpec((IDX_HEADS, tm), lambda i: (0, i)),
                                                 pl.BlockSpec((1, 2 * nkv, tm), lambda i: (i, 0, 0))],
        out_shape=[jax.ShapeDtypeStruct((m, c), dt) for c, dt in outs]
        + [jax.ShapeDtypeStruct((IDX_HEADS, m), F32), jax.ShapeDtypeStruct((m // tm, 2 * nkv, tm), BF16)],
        compiler_params=_params(1),
        name="proj_b",
    )(x, w_pad)


def _sparse_kernel(x_ref, q_ref, qi_ref, wit_ref, ka_ref, vt_ref, kd_ref, wo_ref, g_ref, b_ref, o_ref,
                   key_scr, hi_scr, lo_scr, bias_scr, qt_scr, s_scr, acc_scr, m_scr, ot_scr,
                   *, tq, kb, pos0, n_keys, n_sel, n_heads, group, slopes, alpha):
    t = pl.program_id(1)
    q0 = pos0 + t * tq
    qpos_row = q0 + lax.broadcasted_iota(I32, (1, tq), 1)
    lim_row = jnp.minimum((lax.shift_right_logical(qpos_row, 6) + 1) * CHUNK, n_keys)
    kmax = jnp.minimum((lax.shift_right_logical(q0 + tq - 1, 6) + 1) * CHUNK, n_keys)
    nkb = lax.shift_right_logical(kmax + kb - 1, int(math.log2(kb)))
    lane = lax.broadcasted_iota(I32, (tq, LANES), 1)
    lo_half = lane < HEAD_DIM

    def key_block(kbi):
        return pl.multiple_of(kbi * kb, kb)

    def key_index(off):
        return off + lax.broadcasted_iota(I32, (kb, tq), 0)

    qis = []
    for p in range(IDX_HEADS // 2):
        q2 = qi_ref[0, :, p * LANES:(p + 1) * LANES]
        qis.append(jnp.where(lo_half, q2, jnp.zeros_like(q2)))
        qis.append(jnp.where(lo_half, jnp.zeros_like(q2), q2))
    wis = wit_ref[...]
    qi_all = jnp.concatenate(qis, axis=0)

    def score_block(kbi, carry):
        off = key_block(kbi)
        dots = _dot_nt(kd_ref[0, pl.ds(off, kb), :], qi_all)
        acc = jnp.zeros((kb, tq), F32)
        for h in range(IDX_HEADS):
            acc = acc + jnp.maximum(dots[:, h * tq:(h + 1) * tq], 0.0) * wis[h:h + 1, :]
        acc = jnp.where(key_index(off) < lim_row, acc, -jnp.inf)
        bits = lax.bitcast_convert_type(acc, I32)
        key = bits ^ (lax.shift_right_arithmetic(bits, 31) & 0x7FFFFFFF)
        key_scr[pl.ds(off, kb), :] = key
        hi_scr[pl.ds(off, kb), :] = lax.shift_right_arithmetic(key, 16).astype(I16)
        lo_scr[pl.ds(off, kb), :] = ((key & 0xFFFF) + I16_MIN).astype(I16)
        return carry

    lax.fori_loop(0, nkb, score_block, 0)

    @pl.when(nkb % 2 == 1)
    def _():
        key_scr[pl.ds(key_block(nkb), kb), :] = jnp.full((kb, tq), INT_MIN, I32)
        hi_scr[pl.ds(key_block(nkb), kb), :] = jnp.full((kb, tq), I16_MIN, I16)
        lo_scr[pl.ds(key_block(nkb), kb), :] = jnp.full((kb, tq), I16_MIN, I16)

    n_steps = lax.shift_right_logical(nkb + 1, 1)

    def step_rows(i):
        return pl.ds(pl.multiple_of(i * 2 * kb, 2 * kb), 2 * kb)

    def count(pred):
        def body(i, part):
            hit = pred(key_scr[step_rows(i), :])
            return part + jnp.sum(jnp.where(hit, 1, 0).reshape(-1, COUNT_ROWS, tq), axis=0)
        part = lax.fori_loop(0, n_steps, body, jnp.zeros((COUNT_ROWS, tq), I32))
        return jnp.sum(part, axis=0, keepdims=True)

    def count16(ref, pred):
        def body(i, parts):
            ind = jnp.where(pred(ref[pl.ds(key_block(i), kb), :]), jnp.int16(1), jnp.int16(0))
            parts = list(parts)
            for j, r in enumerate(range(0, kb, COUNT16_ROWS)):
                parts[j % len(parts)] = parts[j % len(parts)] + ind[r:r + COUNT16_ROWS]
            return tuple(parts)
        parts = lax.fori_loop(0, 2 * n_steps, body, (jnp.zeros((COUNT16_ROWS, tq), I16),) * COUNT16_CHAINS)
        return jnp.sum(sum(p.astype(I32) for p in parts), axis=0, keepdims=True)

    def as_i16(row):
        return jnp.broadcast_to(row, (COUNT16_ROWS, tq)).astype(I16)[:1]

    def bisect16(ref, want):
        def body(i, thr):
            cand = thr + lax.shift_left(jnp.int32(1), 15 - i)
            cand16 = as_i16(cand)
            return jnp.where(count16(ref, lambda v: v >= cand16) >= want, cand, thr)
        return lax.fori_loop(0, 16, body, jnp.full((1, tq), I16_MIN, I32))

    thr_hi = bisect16(hi_scr, n_sel)
    thr_hi16 = as_i16(thr_hi)
    n_hi_gt = count16(hi_scr, lambda v: v > thr_hi16)

    def keep_bucket(i, carry):
        rows = step_rows(i)
        lo_scr[rows, :] = jnp.where(hi_scr[rows, :] == thr_hi16, lo_scr[rows, :], jnp.int16(I16_MIN))
        return carry

    lax.fori_loop(0, n_steps, keep_bucket, 0)
    thr = thr_hi * 65536 + (bisect16(lo_scr, n_sel - n_hi_gt) - I16_MIN)
    n_gt = count(lambda key: key > thr)
    n_eq = count(lambda key: key == thr)
    need = n_sel - n_gt
    tie = jnp.max(jnp.where((n_eq > need) & (thr > NEG_INF_KEY), 1, 0)) > 0

    @pl.when(jnp.logical_not(tie))
    def _():
        def body(kbi, carry):
            off = key_block(kbi)
            sel = (key_scr[pl.ds(off, kb), :] >= thr) & (key_index(off) < lim_row)
            bias_scr[pl.ds(off, kb), :] = jnp.where(sel, 0.0, -MASK_BIAS)
            return carry
        lax.fori_loop(0, nkb, body, 0)

    @pl.when(tie)
    def _():
        tri = jnp.where(lax.broadcasted_iota(I32, (kb, kb), 0) > lax.broadcasted_iota(I32, (kb, kb), 1),
                        1.0, 0.0).astype(BF16)
        need_f = need.astype(F32)

        def body(kbi, seen):
            off = key_block(kbi)
            key = key_scr[pl.ds(off, kb), :]
            eq = jnp.where(key == thr, 1.0, 0.0)
            earlier = _dot(tri, eq.astype(BF16)) + seen
            sel = ((key > thr) | ((key == thr) & (earlier < need_f))) & (key_index(off) < lim_row)
            bias_scr[pl.ds(off, kb), :] = jnp.where(sel, 0.0, -MASK_BIAS)
            return seen + jnp.sum(eq, axis=0, keepdims=True)
        lax.fori_loop(0, nkb, body, jnp.zeros((1, tq), F32))

    n_units = n_heads // UNIT_HEADS
    kv_lanes = lambda u: slice((u * UNIT_HEADS // group) * LANES, (u * UNIT_HEADS // group + 1) * LANES)
    for h in range(n_heads):
        q2 = q_ref[0, :, (h // 2) * LANES:(h // 2 + 1) * LANES].astype(F32)
        if h % 2:
            q2 = pltpu.roll(q2, HEAD_DIM, 1)
        sl = jnp.zeros((1, LANES), F32)
        for i, piece in enumerate(_bf16_pieces(slopes[h] * LOG2E) * 2):
            sl = jnp.where(lane[:1] == POS_LANE0 + i, piece, sl)
        j = h % UNIT_HEADS
        qt_scr[h // UNIT_HEADS, :, j * tq:(j + 1) * tq] = jnp.where(lo_half, q2, sl).T.astype(BF16)
    m_scr[...] = jnp.full(m_scr.shape, -jnp.inf, F32)
    acc_scr[...] = jnp.zeros(acc_scr.shape, F32)

    def scores(kbi, u, last):
        off = key_block(kbi)
        bias = jnp.concatenate([bias_scr[pl.ds(off, kb), :]] * UNIT_HEADS, axis=1)
        s = _dot(ka_ref[0, pl.ds(off, kb), kv_lanes(u)], qt_scr[u]) + bias
        if last:
            ahead = jnp.maximum(key_index(off) - qpos_row, 0).astype(F32)
            s = s - jnp.concatenate([(2.0 * slopes[u * UNIT_HEADS + j] * LOG2E) * ahead for j in range(UNIT_HEADS)], axis=1)
        s_scr[u] = s

    def absorb(kbi, u):
        s = s_scr[u]
        m_old = m_scr[u]
        m_new = jnp.maximum(m_old, jnp.max(s, axis=0, keepdims=True))
        p = jnp.exp2((s - m_new).astype(BF16))
        acc_scr[u] = acc_scr[u] * jnp.exp2(m_old - m_new) + _dot(vt_ref[0, kbi, kv_lanes(u), :], p)
        m_scr[u] = m_new

    def step(kbi, last):
        for u in range(n_units):
            absorb(kbi - 1, u)
            scores(kbi, u, last)

    def first_scores(last):
        for u in range(n_units):
            scores(0, u, last)

    def step_body(kbi, carry):
        step(kbi, False)
        return carry

    pl.when(nkb == 1)(lambda: first_scores(True))
    pl.when(nkb > 1)(lambda: first_scores(False))
    lax.fori_loop(1, nkb - 1, step_body, 0)
    pl.when(nkb > 1)(lambda: step(nkb - 1, True))
    for u in range(n_units):
        absorb(nkb - 1, u)

    for u in range(n_units):
        acc = acc_scr[u]
        o_t = acc[:HEAD_DIM] / acc[HEAD_DIM:HEAD_DIM + 1]
        for j in range(UNIT_HEADS):
            h = u * UNIT_HEADS + j
            ot_scr[h * HEAD_DIM:(h + 1) * HEAD_DIM, :] = o_t[:, j * tq:(j + 1) * tq].astype(BF16)
    y = lax.dot_general(ot_scr[...], wo_ref[...], (((0,), (0,)), ((), ())), preferred_element_type=F32)
    o_ref[0] = _layer_norm(alpha * x_ref[0] + y, g_ref[...], b_ref[...])


def _sparse_attn(x, q, qi, wit, k, vt, kd, wo, g, b, *, tq, kb, pos0, n_keys, alpha):
    bsz, s, d = x.shape
    n_t = s // tq
    l_pad = k.shape[1]
    n_heads = q.shape[2] // HEAD_DIM
    n_kv = k.shape[2] // LANES
    group = n_heads // n_kv
    assert kb % tq == 0 and pos0 % tq == 0 and l_pad % kb == 0 and vt.shape == (bsz, l_pad // kb, k.shape[2], kb)
    slopes = tuple(2.0 ** (-8.0 * (h + 1) / n_heads) for h in range(n_heads))
    qrow = lambda c: pl.BlockSpec((1, tq, c), lambda bi, t: (bi, t, 0))
    keys = lambda c: pl.BlockSpec((1, l_pad, c), lambda bi, t: (bi, 0, 0), pipeline_mode=pl.Buffered(1))
    const2 = lambda bi, t: (0, 0)
    sel_rows = -(-l_pad // (2 * kb)) * 2 * kb
    assert group % UNIT_HEADS == 0
    n_units, unit_w = n_heads // UNIT_HEADS, UNIT_HEADS * tq
    kern = functools.partial(_sparse_kernel, tq=tq, kb=kb, pos0=pos0, n_keys=n_keys, n_sel=min(TOPK_MAX, n_keys // 4),
                             n_heads=n_heads, group=group, slopes=slopes, alpha=alpha)
    return pl.pallas_call(
        kern,
        grid=(bsz, n_t),
        in_specs=[qrow(d), qrow(q.shape[2]), qrow(qi.shape[2]),
                  pl.BlockSpec((IDX_HEADS, tq), lambda bi, t: (0, bi * n_t + t)),
                  keys(k.shape[2]),
                  pl.BlockSpec((1,) + vt.shape[1:], lambda bi, t: (bi, 0, 0, 0), pipeline_mode=pl.Buffered(1)),
                  keys(kd.shape[2]),
                  _resident(wo.shape), pl.BlockSpec((1, d), const2), pl.BlockSpec((1, d), const2)],
        out_specs=qrow(d),
        out_shape=jax.ShapeDtypeStruct((bsz, s, d), F32),
        scratch_shapes=[pltpu.VMEM((sel_rows, tq), I32), pltpu.VMEM((sel_rows, tq), I16), pltpu.VMEM((sel_rows, tq), I16),
                        pltpu.VMEM((l_pad, tq), F32),
                        pltpu.VMEM((n_units, LANES, unit_w), BF16), pltpu.VMEM((n_units, kb, unit_w), F32),
                        pltpu.VMEM((n_units, LANES, unit_w), F32),
                        pltpu.VMEM((n_units, 1, unit_w), F32), pltpu.VMEM((q.shape[2], tq), BF16)],
        compiler_params=_params(2),
        name="sparse_attn",
    )(x, q, qi, wit, k, vt, kd, wo, g, b)


def _pad_rows(a, n):
    return jnp.pad(a, ((0, 0), (0, n - a.shape[1]), (0, 0)))


def _mixer_a(xp, xs, cache_k, cache_v, w_in, rel_bias, w_out, g, b, alpha):
    bsz, s, d = xp.shape
    dbs, t, _ = xs.shape
    width = w_out.shape[0]
    w_in = w_in.astype(BF16)
    w_out = w_out.astype(BF16)
    bias_p, bias_s = _rel_bias_tables(rel_bias, t)
    keep = min(A_PAST, s)
    q, k, vt, kf, vf = _proj_a(xp, w_in, keep)
    yp = _attn_a_prompt(xp, q, k, vt, bias_p, w_out, g, b, alpha)
    qs, ks, _, ksf, vsf = _proj_a(xs.reshape(1, dbs * t, d), w_in, dbs * t)
    n_cache = cache_k.shape[1]
    kwin = _pad_rows(jnp.concatenate([cache_k.reshape(dbs, n_cache, width).astype(BF16), ks.reshape(dbs, t, width)], 1), A_WIN)
    vwin = _pad_rows(jnp.concatenate([cache_v.reshape(dbs, n_cache, width), vsf.reshape(dbs, t, width)], 1).astype(BF16), A_WIN)
    ys = _attn_a_sample(_pad_rows(xs, A_SUB), _pad_rows(qs.reshape(dbs, t, width), A_SUB), kwin, vwin.transpose(0, 2, 1),
                        bias_s, w_out, g, b, alpha)[:, :t]
    heads = width // HEAD_DIM
    return (yp, ys, kf.reshape(bsz, keep, heads, HEAD_DIM), vf.reshape(bsz, keep, heads, HEAD_DIM),
            ksf.reshape(dbs, t, heads, HEAD_DIM), vsf.reshape(dbs, t, heads, HEAD_DIM))


def _mixer_b(xp, xs, cache_k, cache_v, cache_idx, w_in, w_out, g, b, alpha):
    bsz, s, d = xp.shape
    dbs, t, _ = xs.shape
    nq = w_out.shape[0]
    nkv = cache_k.shape[2] * cache_k.shape[3]
    nqi = IDX_HEADS * HEAD_DIM
    kvh = nkv // HEAD_DIM
    past = cache_k.shape[1]
    w_pad = jnp.pad(w_in, ((0, 0), (0, nq + 2 * nkv + nqi + LANES - w_in.shape[1]))).astype(BF16)
    w_out = w_out.astype(BF16)
    kb = 512

    q, qi, k, kf, vf, kiwi, kd, wit, vt = _proj_b(xp.reshape(bsz * s, d), w_pad, nq, nkv, nqi, kb, 0, s)
    r3 = lambda a: a.reshape(bsz, s, a.shape[1])
    yp = _sparse_attn(xp, r3(q), r3(qi), wit, r3(k), vt.reshape(bsz, s // kb, 2 * nkv, kb), r3(kd), w_out, g, b,
                      tq=256, kb=kb, pos0=0, n_keys=s, alpha=alpha)

    qs, qis, ks, ksf, vsf, kiwis, kds, wits, _ = _proj_b(xs.reshape(dbs * t, d), w_pad, nq, nkv, nqi, dbs * t, past, t)
    tq_s = LANES
    n_keys = past + t
    l_pad = -(-n_keys // kb) * kb
    s3 = lambda a: a.reshape(dbs, t, a.shape[1])
    v_all = jnp.concatenate([cache_v, vsf.reshape(dbs, t, kvh, HEAD_DIM)], axis=1).astype(BF16)
    v_all = jnp.concatenate([v_all, jnp.ones_like(v_all)], axis=-1).reshape(dbs, n_keys, 2 * nkv)
    keys_vt = _pad_rows(v_all, l_pad).reshape(dbs, l_pad // kb, kb, 2 * nkv).transpose(0, 1, 3, 2)
    kd_cache = jnp.concatenate([cache_idx, cache_idx], axis=-1).astype(BF16)
    frame = jnp.arange(past, dtype=I32)[:, None]
    pos = _pos_lanes(frame, HEAD_DIM + jnp.arange(HEAD_DIM, dtype=I32)[None, :]).astype(BF16)
    pos = jnp.broadcast_to(pos[None, :, None, :], (dbs, past, kvh, HEAD_DIM))
    k_cache = jnp.concatenate([cache_k.astype(BF16), pos], axis=-1).reshape(dbs, past, 2 * nkv)
    keys_k = _pad_rows(jnp.concatenate([k_cache, s3(ks)], 1), l_pad)
    keys_kd = _pad_rows(jnp.concatenate([kd_cache, s3(kds)], 1), l_pad)
    wit_pad = jnp.pad(wits.reshape(IDX_HEADS, dbs, t), ((0, 0), (0, 0), (0, tq_s - t))).reshape(IDX_HEADS, dbs * tq_s)
    ys = _sparse_attn(_pad_rows(xs, tq_s), _pad_rows(s3(qs), tq_s), _pad_rows(s3(qis), tq_s), wit_pad,
                      keys_k, keys_vt, keys_kd, w_out, g, b,
                      tq=tq_s, kb=kb, pos0=past, n_keys=n_keys, alpha=alpha)[:, :t]
    return (yp, ys,
            kf.reshape(bsz, s, kvh, HEAD_DIM), vf.reshape(bsz, s, kvh, HEAD_DIM), kiwi[:, :HEAD_DIM].reshape(bsz, s, HEAD_DIM),
            ksf.reshape(dbs, t, kvh, HEAD_DIM), vsf.reshape(dbs, t, kvh, HEAD_DIM), kiwis[:, :HEAD_DIM].reshape(dbs, t, HEAD_DIM))


def kernel(x_prompt, x_sample, cache_a_k, cache_a_v, cache_b_k, cache_b_v, cache_b_idx, ln_g, ln_b, ffn_w_gate, ffn_w_up, ffn_w_down, a_w_in, a_rel_bias, a_w_out, b_w_in, b_w_out):
    depth = ln_g.shape[0]
    alpha = (2.0 * depth) ** 0.25
    bsz, s, d = x_prompt.shape
    dbs, t, _ = x_sample.shape
    xp, xs = x_prompt, x_sample
    a_out, b_out = [], []

    def ffn(x, layer, i, tm):
        shp = x.shape
        y = _ffn_block(x.reshape(-1, d), ffn_w_gate[layer, i].astype(BF16), ffn_w_up[layer, i].astype(BF16),
                       ffn_w_down[layer, i].astype(BF16), ln_g[layer, 2 * i][None], ln_b[layer, 2 * i][None], alpha, tm)
        return y.reshape(shp)

    for layer in range(depth):
        j = layer // 2
        xp = ffn(xp, layer, 0, 512)
        xs = ffn(xs, layer, 0, dbs * t)
        g, b = ln_g[layer, 1][None], ln_b[layer, 1][None]
        if layer % 2 == 0:
            xp, xs, *rest = _mixer_a(xp, xs, cache_a_k[j], cache_a_v[j], a_w_in[j], a_rel_bias[j], a_w_out[j], g, b, alpha)
            a_out.append(rest)
        else:
            xp, xs, *rest = _mixer_b(xp, xs, cache_b_k[j], cache_b_v[j], cache_b_idx[j], b_w_in[j], b_w_out[j], g, b, alpha)
            b_out.append(rest)
        xp = ffn(xp, layer, 1, 512)
        xs = ffn(xs, layer, 1, dbs * t)

    stack = lambda outs, i: jnp.stack([o[i] for o in outs], 0)
    return (xp, xs,
            stack(a_out, 0), stack(a_out, 1), stack(a_out, 2), stack(a_out, 3),
            stack(b_out, 0), stack(b_out, 1), stack(b_out, 2), stack(b_out, 3), stack(b_out, 4), stack(b_out, 5))
```

```python
import functools
import math
import struct

import jax
import jax.numpy as jnp
from jax import lax
from jax.experimental import pallas as pl
from jax.experimental.pallas import tpu as pltpu

F32 = jnp.float32
BF16 = jnp.bfloat16
I32 = jnp.int32
I16 = jnp.int16

CHUNK = 64
A_PAST = 512
REL_CLIP = 128
HEAD_DIM = 64
LANES = 128
IDX_HEADS = 8
TOPK_MAX = 256
LN_EPS = 1e-5
A_SUB = 4 * CHUNK
A_WIN = A_PAST + A_SUB
VMEM_LIMIT = 52 * 1024 * 1024

LOG2E = 1.4426950408889634
POS_LANE0 = HEAD_DIM
MASK_BIAS = 1e30
ONES_ROWS = 16
FFN_CHUNK = 768
UNIT_LANES = 512
COUNT_ROWS = 16

NT_DIMS = (((1,), (1,)), ((), ()))
INT_MIN = -2147483648
I16_MIN = -32768
COUNT16_ROWS = 16
COUNT16_CHAINS = 4
NEG_INF_KEY = -2139095041


def _params(n_grid, flags=None):
    return pltpu.CompilerParams(dimension_semantics=("arbitrary",) * n_grid,
                                vmem_limit_bytes=VMEM_LIMIT, flags=flags)


def _resident(shape):
    zeros = (0,) * len(shape)
    return pl.BlockSpec(shape, lambda *_: zeros, pipeline_mode=pl.Buffered(1))


def _layer_norm(z, g, b):
    mu = jnp.mean(z, axis=-1, keepdims=True)
    d = z - mu
    var = jnp.mean(d * d, axis=-1, keepdims=True)
    return d * lax.rsqrt(var + LN_EPS) * g + b


def _dot(a, b):
    return jnp.dot(a, b, preferred_element_type=F32)


def _dot_nt(a, b):
    return lax.dot_general(a, b, NT_DIMS, preferred_element_type=F32)


def _ffn_kernel(x_ref, wg_ref, wu_ref, wd_ref, g_ref, b_ref, o_ref, h_scr, *, alpha):
    x = x_ref[...]
    xb = x.astype(BF16)
    f = wg_ref.shape[1]
    for c0 in range(0, f, FFN_CHUNK):
        sl = slice(c0, min(c0 + FFN_CHUNK, f))
        gate = _dot(xb, wg_ref[:, sl])
        up = _dot(xb, wu_ref[:, sl])
        h_scr[:, sl] = (gate * (1.0 / (1.0 + jnp.exp(-gate))) * up).astype(BF16)
    y = _dot(h_scr[...], wd_ref[...])
    o_ref[...] = _layer_norm(alpha * x + 0.5 * y, g_ref[...], b_ref[...])


def _ffn_block(x, wg, wu, wd, g, b, alpha, tm):
    m, d = x.shape
    f = wg.shape[1]
    const = lambda i: (0, 0)
    return pl.pallas_call(
        functools.partial(_ffn_kernel, alpha=alpha),
        grid=(m // tm,),
        in_specs=[pl.BlockSpec((tm, d), lambda i: (i, 0)),
                  _resident((d, f)), _resident((d, f)), _resident((f, d)),
                  pl.BlockSpec((1, d), const), pl.BlockSpec((1, d), const)],
        out_specs=pl.BlockSpec((tm, d), lambda i: (i, 0)),
        out_shape=jax.ShapeDtypeStruct((m, d), F32),
        scratch_shapes=[pltpu.VMEM((tm, f), BF16)],
        compiler_params=_params(1),
        name="ffn_ln",
    )(x, wg, wu, wd, g, b)


def _proj_a_kernel(x_ref, w_ref, q_ref, k_ref, vt_ref, kf_ref, vf_ref, *, width):
    h = _dot(x_ref[0].astype(BF16), w_ref[...])
    k = h[:, width:2 * width]
    v = h[:, 2 * width:]
    q_ref[0] = (h[:, :width] * (HEAD_DIM ** -0.5 * LOG2E)).astype(BF16)
    k_ref[0] = k.astype(BF16)
    vt_ref[0] = v.T.astype(BF16)

    @pl.when(pl.program_id(1) == pl.num_programs(1) - 1)
    def _():
        kf_ref[0] = k
        vf_ref[0] = v


def _proj_a(x, w, tm):
    bsz, s, d = x.shape
    width = w.shape[1] // 3
    row = pl.BlockSpec((1, tm, width), lambda b, t: (b, t, 0))
    last = pl.BlockSpec((1, tm, width), lambda b, t: (b, 0, 0))
    return pl.pallas_call(
        functools.partial(_proj_a_kernel, width=width),
        grid=(bsz, s // tm),
        in_specs=[pl.BlockSpec((1, tm, d), lambda b, t: (b, t, 0)), _resident(w.shape)],
        out_specs=[row, row, pl.BlockSpec((1, width, tm), lambda b, t: (b, 0, t)), last, last],
        out_shape=[jax.ShapeDtypeStruct((bsz, s, width), BF16)] * 2 + [jax.ShapeDtypeStruct((bsz, width, s), BF16)]
        + [jax.ShapeDtypeStruct((bsz, tm, width), F32)] * 2,
        compiler_params=_params(2),
        name="proj_a",
    )(x, w)


def _attn_a_kernel(*refs, n_sub, prompt, alpha):
    if prompt:
        (x_ref, q_ref, kp_ref, kc_ref, vtp_ref, vtc_ref, bias_ref, wo_ref, g_ref, b_ref,
         o_ref, kwin, q_scr, s_scr, ot_scr, vtwin) = refs
        vtwin[:, 0:A_PAST] = vtp_ref[0]
        vtwin[:, A_PAST:2 * A_PAST] = vtc_ref[0]
        vt_at = lambda rows, r0: vtwin[rows, r0:r0 + A_WIN]
    else:
        x_ref, q_ref, k_ref, vt_ref, bias_ref, wo_ref, g_ref, b_ref, o_ref, kwin, q_scr, s_scr, ot_scr = refs
        vt_at = lambda rows, r0: vt_ref[0, rows, r0:r0 + A_WIN]
    n_pairs = q_ref.shape[2] // LANES
    for p in range(n_pairs):
        cols = slice(p * LANES, (p + 1) * LANES)
        q_scr[p] = q_ref[0, :, cols]
        if prompt:
            kwin[p, 0:A_PAST] = kp_ref[0, :, cols]
            kwin[p, A_PAST:2 * A_PAST] = kc_ref[0, :, cols]
        else:
            kwin[p] = k_ref[0, :, cols]
    lane = lax.broadcasted_iota(I32, (A_SUB, LANES), 1)
    key_row = lax.broadcasted_iota(I32, (A_WIN, 2 * A_SUB), 0)
    out_row = lax.broadcasted_iota(I32, (LANES, A_SUB), 0)
    ones_rows = jnp.ones((ONES_ROWS, A_WIN), BF16)

    def scores(p, j, side, first_valid):
        r0 = j * A_SUB
        q2 = q_scr[p, r0:r0 + A_SUB, :].astype(F32)
        qt = jnp.concatenate([jnp.where(lane < HEAD_DIM, q2, 0.0).T, jnp.where(lane < HEAD_DIM, 0.0, q2).T], axis=1)
        s = _dot(kwin[p, r0:r0 + A_WIN, :], qt.astype(BF16)) + bias_ref[p]
        if first_valid > r0:
            s = jnp.where(key_row >= first_valid - r0, s, -jnp.inf)
        s_scr[side, j] = s

    def absorb(p, j, side):
        r0 = j * A_SUB
        s = s_scr[side, j]
        e = jnp.exp2((s - jnp.max(s, axis=0, keepdims=True)).astype(BF16))
        rows = pl.ds(p * LANES, LANES)
        o = _dot(jnp.concatenate([vt_at(rows, r0), ones_rows], axis=0), e)
        o = o[:LANES] / o[LANES:LANES + 1]
        ot_scr[rows, r0:r0 + A_SUB] = jnp.where(out_row < HEAD_DIM, o[:, :A_SUB], o[:, A_SUB:]).astype(BF16)

    def overlap(p_scores, p_absorb, side, first_valid):
        for j in range(n_sub):
            scores(p_scores, j, side, first_valid)
        for j in range(n_sub):
            absorb(p_absorb, j, 1 - side)

    def attend(first_valid):
        assert n_pairs % 2 == 0
        for j in range(n_sub):
            scores(0, j, 0, first_valid)

        def two_pairs(i, carry):
            overlap(2 * i + 1, 2 * i, 1, first_valid)
            overlap(2 * i + 2, 2 * i + 1, 0, first_valid)
            return carry

        for i in range(n_pairs // 2 - 1):
            two_pairs(i, 0)
        overlap(n_pairs - 1, n_pairs - 2, 1, first_valid)
        for j in range(n_sub):
            absorb(n_pairs - 1, j, 1)

    if prompt:
        pl.when(pl.program_id(1) == 0)(lambda: attend(A_PAST))
        pl.when(pl.program_id(1) > 0)(lambda: attend(0))
    else:
        attend(0)

    y = lax.dot_general(ot_scr[...], wo_ref[...], (((0,), (0,)), ((), ())), preferred_element_type=F32)
    o_ref[0] = _layer_norm(alpha * x_ref[0] + y, g_ref[...], b_ref[...])


def _attn_a_scratch(width, n_keys, n_queries):
    pairs = width // LANES
    return [pltpu.VMEM((pairs, n_keys, LANES), BF16), pltpu.VMEM((pairs, n_queries, LANES), BF16),
            pltpu.VMEM((2, n_queries // A_SUB, A_WIN, 2 * A_SUB), F32), pltpu.VMEM((width, n_queries), BF16)]


def _attn_a_prompt(x, q, k, vt, bias, wo, g, b, alpha):
    bsz, s, d = x.shape
    width = q.shape[2]
    tq = A_PAST
    cur = lambda bi, t: (bi, t, 0)
    prev = lambda bi, t: (bi, jnp.maximum(t - 1, 0), 0)
    cur_t = lambda bi, t: (bi, 0, t)
    prev_t = lambda bi, t: (bi, 0, jnp.maximum(t - 1, 0))
    const2 = lambda bi, t: (0, 0)
    return pl.pallas_call(
        functools.partial(_attn_a_kernel, n_sub=tq // A_SUB, prompt=True, alpha=alpha),
        grid=(bsz, s // tq),
        in_specs=[pl.BlockSpec((1, tq, d), cur), pl.BlockSpec((1, tq, width), cur),
                  pl.BlockSpec((1, tq, width), prev), pl.BlockSpec((1, tq, width), cur),
                  pl.BlockSpec((1, width, tq), prev_t), pl.BlockSpec((1, width, tq), cur_t),
                  _resident(bias.shape), _resident(wo.shape),
                  pl.BlockSpec((1, d), const2), pl.BlockSpec((1, d), const2)],
        out_specs=pl.BlockSpec((1, tq, d), cur),
        out_shape=jax.ShapeDtypeStruct((bsz, s, d), F32),
        scratch_shapes=_attn_a_scratch(width, 2 * tq, tq) + [pltpu.VMEM((width, 2 * tq), BF16)],
        compiler_params=_params(2),
        name="attn_a_prompt",
    )(x, q, k, k, vt, vt, bias, wo, g, b)


def _attn_a_sample(x, q, kwin, vtwin, bias, wo, g, b, alpha):
    bsz, _, d = x.shape
    width = q.shape[2]
    blk = lambda n, c: pl.BlockSpec((1, n, c), lambda bi: (bi, 0, 0))
    const2 = lambda bi: (0, 0)
    return pl.pallas_call(
        functools.partial(_attn_a_kernel, n_sub=1, prompt=False, alpha=alpha),
        grid=(bsz,),
        in_specs=[blk(A_SUB, d), blk(A_SUB, width), blk(A_WIN, width), blk(width, A_WIN),
                  _resident(bias.shape), _resident(wo.shape),
                  pl.BlockSpec((1, d), const2), pl.BlockSpec((1, d), const2)],
        out_specs=blk(A_SUB, d),
        out_shape=jax.ShapeDtypeStruct((bsz, A_SUB, d), F32),
        scratch_shapes=_attn_a_scratch(width, A_WIN, A_SUB),
        compiler_params=_params(1),
        name="attn_a_sample",
    )(x, q, kwin, vtwin, bias, wo, g, b)


def _rel_bias_tables(rel_bias, n_sample):
    r = jnp.arange(A_SUB)[:, None]
    c = jnp.arange(A_WIN)[None, :]
    period = A_SUB + A_WIN
    diff = jnp.arange(period)
    diff = jnp.where(diff < A_WIN, diff, diff - period)
    line = rel_bias[:, jnp.clip(A_PAST - diff, -REL_CLIP, REL_CLIP) + REL_CLIP].astype(F32) * LOG2E
    table = jnp.tile(line, (1, A_SUB))[:, :A_SUB * (period - 1)].reshape(-1, A_SUB, period - 1)[:, :, :A_WIN]
    lo = (r // CHUNK) * CHUNK
    band = (c >= lo) & (c < lo + A_PAST + CHUNK)
    prompt = jnp.where(band[None], table, -jnp.inf)
    live = (c < A_PAST + n_sample)
    sample = jnp.where(live[None], jnp.where((r < n_sample)[None], table, 0.0), -jnp.inf)

    def pair_layout(tab):
        h = tab.shape[0]
        return tab.transpose(0, 2, 1).reshape(h // 2, 2, A_WIN, A_SUB).transpose(0, 2, 1, 3).reshape(h // 2, A_WIN, 2 * A_SUB)

    return pair_layout(prompt), pair_layout(sample)


def _round_to_bf16(x):
    bits = struct.unpack("<I", struct.pack("<f", x))[0]
    bits = (bits + 0x7FFF + ((bits >> 16) & 1)) & 0xFFFF0000
    return struct.unpack("<f", struct.pack("<I", bits))[0]


def _bf16_pieces(x):
    p1 = _round_to_bf16(x)
    p2 = _round_to_bf16(x - p1)
    return (p1, p2, _round_to_bf16(x - p1 - p2))


def _pos_lanes(pos, lane):
    hi = (lax.shift_right_logical(pos, 6) * CHUNK).astype(F32)
    lo = (pos & (CHUNK - 1)).astype(F32)
    return jnp.where(lane < POS_LANE0 + 3, hi, jnp.where(lane < POS_LANE0 + 6, lo, 0.0))


def _proj_b_kernel(x_ref, w_ref, q_ref, qi_ref, ka_ref, kf_ref, vf_ref, kiwi_ref, kd_ref, wit_ref, vt_ref,
                   *, nq, nkv, nqi, pos0, period):
    h = _dot(x_ref[...].astype(BF16), w_ref[...])
    q_ref[...] = (h[:, :nq] * (HEAD_DIM ** -0.5 * LOG2E)).astype(BF16)
    k = h[:, nq:nq + nkv]
    v = h[:, nq + nkv:nq + 2 * nkv]
    kf_ref[...] = k
    vf_ref[...] = v
    o0 = nq + 2 * nkv
    qi_ref[...] = (h[:, o0:o0 + nqi] * (HEAD_DIM ** -0.5)).astype(BF16)
    tail = h[:, o0 + nqi:o0 + nqi + LANES]
    kiwi_ref[...] = tail
    lane = lax.broadcasted_iota(I32, tail.shape, 1)
    kd_ref[...] = jnp.where(lane < HEAD_DIM, tail, pltpu.roll(tail, HEAD_DIM, 1)).astype(BF16)
    wit_ref[...] = tail.T[HEAD_DIM:HEAD_DIM + IDX_HEADS, :] * (IDX_HEADS ** -0.5)
    tm = tail.shape[0]
    row = pl.program_id(0) * tm + lax.broadcasted_iota(I32, (tm, 1), 0)
    pos = _pos_lanes(pos0 + (row & (period - 1)), lane)
    for pair in range(nkv // LANES):
        for half in range(2):
            tile = slice((2 * pair + half) * LANES, (2 * pair + half + 1) * LANES)
            vp = v[:, pair * LANES:(pair + 1) * LANES]
            kp = k[:, pair * LANES:(pair + 1) * LANES]
            if half:
                vp, kp = pltpu.roll(vp, HEAD_DIM, 1), pltpu.roll(kp, HEAD_DIM, 1)
            vt_ref[0, tile, :] = jnp.where(lane < HEAD_DIM, vp, 1.0).T.astype(BF16)
            ka_ref[:, tile] = jnp.where(lane < HEAD_DIM, kp, pos).astype(BF16)


def _proj_b(x, w_pad, nq, nkv, nqi, tm, pos0, period):
    m, d = x.shape
    assert period & (period - 1) == 0
    rows = lambda c: pl.BlockSpec((tm, c), lambda i: (i, 0))
    outs = [(nq, BF16), (nqi, BF16), (2 * nkv, BF16), (nkv, F32), (nkv, F32), (LANES, F32), (LANES, BF16)]
    return pl.pallas_call(
        functools.partial(_proj_b_kernel, nq=nq, nkv=nkv, nqi=nqi, pos0=pos0, period=period),
        grid=(m // tm,),
        in_specs=[rows(d), _resident(w_pad.shape)],
        out_specs=[rows(c) for c, _ in outs] + [pl.BlockSpec((IDX_HEADS, tm), lambda i: (0, i)),
                                                 pl.BlockSpec((1, 2 * nkv, tm), lambda i: (i, 0, 0))],
        out_shape=[jax.ShapeDtypeStruct((m, c), dt) for c, dt in outs]
        + [jax.ShapeDtypeStruct((IDX_HEADS, m), F32), jax.ShapeDtypeStruct((m // tm, 2 * nkv, tm), BF16)],
        compiler_params=_params(1),
        name="proj_b",
    )(x, w_pad)


def _sparse_kernel(x_ref, q_ref, qi_ref, wit_ref, ka_ref, vt_ref, kd_ref, wo_ref, g_ref, b_ref, o_ref,
                   key_scr, hi_scr, lo_scr, bias_scr, qt_scr, s_scr, acc_scr, m_scr, ot_scr,
                   *, tq, kb, pos0, n_keys, n_sel, n_heads, group, slopes, alpha):
    t = pl.program_id(1)
    q0 = pos0 + t * tq
    qpos_row = q0 + lax.broadcasted_iota(I32, (1, tq), 1)
    lim_row = jnp.minimum((lax.shift_right_logical(qpos_row, 6) + 1) * CHUNK, n_keys)
    kmax = jnp.minimum((lax.shift_right_logical(q0 + tq - 1, 6) + 1) * CHUNK, n_keys)
    nkb = lax.shift_right_logical(kmax + kb - 1, int(math.log2(kb)))
    lane = lax.broadcasted_iota(I32, (tq, LANES), 1)
    lo_half = lane < HEAD_DIM

    def key_block(kbi):
        return pl.multiple_of(kbi * kb, kb)

    def key_index(off):
        return off + lax.broadcasted_iota(I32, (kb, tq), 0)

    qis = []
    for p in range(IDX_HEADS // 2):
        q2 = qi_ref[0, :, p * LANES:(p + 1) * LANES]
        qis.append(jnp.where(lo_half, q2, jnp.zeros_like(q2)))
        qis.append(jnp.where(lo_half, jnp.zeros_like(q2), q2))
    wis = wit_ref[...]
    qi_all = jnp.concatenate(qis, axis=0)

    def score_block(kbi, carry):
        off = key_block(kbi)
        dots = _dot_nt(kd_ref[0, pl.ds(off, kb), :], qi_all)
        acc = jnp.zeros((kb, tq), F32)
        for h in range(IDX_HEADS):
            acc = acc + jnp.maximum(dots[:, h * tq:(h + 1) * tq], 0.0) * wis[h:h + 1, :]
        acc = jnp.where(key_index(off) < lim_row, acc, -jnp.inf)
        bits = lax.bitcast_convert_type(acc, I32)
        key = bits ^ (lax.shift_right_arithmetic(bits, 31) & 0x7FFFFFFF)
        key_scr[pl.ds(off, kb), :] = key
        hi_scr[pl.ds(off, kb), :] = lax.shift_right_arithmetic(key, 16).astype(I16)
        lo_scr[pl.ds(off, kb), :] = ((key & 0xFFFF) + I16_MIN).astype(I16)
        return carry

    lax.fori_loop(0, nkb, score_block, 0)

    @pl.when(nkb % 2 == 1)
    def _():
        key_scr[pl.ds(key_block(nkb), kb), :] = jnp.full((kb, tq), INT_MIN, I32)
        hi_scr[pl.ds(key_block(nkb), kb), :] = jnp.full((kb, tq), I16_MIN, I16)
        lo_scr[pl.ds(key_block(nkb), kb), :] = jnp.full((kb, tq), I16_MIN, I16)

    n_steps = lax.shift_right_logical(nkb + 1, 1)

    def step_rows(i):
        return pl.ds(pl.multiple_of(i * 2 * kb, 2 * kb), 2 * kb)

    def count(pred):
        def body(i, part):
            hit = pred(key_scr[step_rows(i), :])
            return part + jnp.sum(jnp.where(hit, 1, 0).reshape(-1, COUNT_ROWS, tq), axis=0)
        part = lax.fori_loop(0, n_steps, body, jnp.zeros((COUNT_ROWS, tq), I32))
        return jnp.sum(part, axis=0, keepdims=True)

    def count16(ref, pred):
        def body(i, parts):
            ind = jnp.where(pred(ref[pl.ds(key_block(i), kb), :]), jnp.int16(1), jnp.int16(0))
            parts = list(parts)
            for j, r in enumerate(range(0, kb, COUNT16_ROWS)):
                parts[j % len(parts)] = parts[j % len(parts)] + ind[r:r + COUNT16_ROWS]
            return tuple(parts)
        parts = lax.fori_loop(0, 2 * n_steps, body, (jnp.zeros((COUNT16_ROWS, tq), I16),) * COUNT16_CHAINS)
        return jnp.sum(sum(p.astype(I32) for p in parts), axis=0, keepdims=True)

    def as_i16(row):
        return jnp.broadcast_to(row, (COUNT16_ROWS, tq)).astype(I16)[:1]

    def bisect16(ref, want):
        def body(i, thr):
            cand = thr + lax.shift_left(jnp.int32(1), 15 - i)
            cand16 = as_i16(cand)
            return jnp.where(count16(ref, lambda v: v >= cand16) >= want, cand, thr)
        return lax.fori_loop(0, 16, body, jnp.full((1, tq), I16_MIN, I32))

    thr_hi = bisect16(hi_scr, n_sel)
    thr_hi16 = as_i16(thr_hi)
    n_hi_gt = count16(hi_scr, lambda v: v > thr_hi16)

    def keep_bucket(i, carry):
        rows = step_rows(i)
        lo_scr[rows, :] = jnp.where(hi_scr[rows, :] == thr_hi16, lo_scr[rows, :], jnp.int16(I16_MIN))
        return carry

    lax.fori_loop(0, n_steps, keep_bucket, 0)
    thr = thr_hi * 65536 + (bisect16(lo_scr, n_sel - n_hi_gt) - I16_MIN)
    n_gt = count(lambda key: key > thr)
    n_eq = count(lambda key: key == thr)
    need = n_sel - n_gt
    tie = jnp.max(jnp.where((n_eq > need) & (thr > NEG_INF_KEY), 1, 0)) > 0

    @pl.when(jnp.logical_not(tie))
    def _():
        def body(kbi, carry):
            off = key_block(kbi)
            sel = (key_scr[pl.ds(off, kb), :] >= thr) & (key_index(off) < lim_row)
            bias_scr[pl.ds(off, kb), :] = jnp.where(sel, 0.0, -MASK_BIAS)
            return carry
        lax.fori_loop(0, nkb, body, 0)

    @pl.when(tie)
    def _():
        tri = jnp.where(lax.broadcasted_iota(I32, (kb, kb), 0) > lax.broadcasted_iota(I32, (kb, kb), 1),
                        1.0, 0.0).astype(BF16)
        need_f = need.astype(F32)

        def body(kbi, seen):
            off = key_block(kbi)
            key = key_scr[pl.ds(off, kb), :]
            eq = jnp.where(key == thr, 1.0, 0.0)
            earlier = _dot(tri, eq.astype(BF16)) + seen
            sel = ((key > thr) | ((key == thr) & (earlier < need_f))) & (key_index(off) < lim_row)
            bias_scr[pl.ds(off, kb), :] = jnp.where(sel, 0.0, -MASK_BIAS)
            return seen + jnp.sum(eq, axis=0, keepdims=True)
        lax.fori_loop(0, nkb, body, jnp.zeros((1, tq), F32))

    unit_heads = qt_scr.shape[2] // tq
    n_units = n_heads // unit_heads
    kv_lanes = lambda u: slice((u * unit_heads // group) * LANES, (u * unit_heads // group + 1) * LANES)
    for h in range(n_heads):
        q2 = q_ref[0, :, (h // 2) * LANES:(h // 2 + 1) * LANES].astype(F32)
        if h % 2:
            q2 = pltpu.roll(q2, HEAD_DIM, 1)
        sl = jnp.zeros((1, LANES), F32)
        for i, piece in enumerate(_bf16_pieces(slopes[h] * LOG2E) * 2):
            sl = jnp.where(lane[:1] == POS_LANE0 + i, piece, sl)
        j = h % unit_heads
        qt_scr[h // unit_heads, :, j * tq:(j + 1) * tq] = jnp.where(lo_half, q2, sl).T.astype(BF16)
    m_scr[...] = jnp.full(m_scr.shape, -jnp.inf, F32)
    acc_scr[...] = jnp.zeros(acc_scr.shape, F32)

    def scores(kbi, u, last):
        off = key_block(kbi)
        bias = jnp.concatenate([bias_scr[pl.ds(off, kb), :]] * unit_heads, axis=1)
        s = _dot(ka_ref[0, pl.ds(off, kb), kv_lanes(u)], qt_scr[u]) + bias
        if last:
            ahead = jnp.maximum(key_index(off) - qpos_row, 0).astype(F32)
            s = s - jnp.concatenate([(2.0 * slopes[u * unit_heads + j] * LOG2E) * ahead for j in range(unit_heads)], axis=1)
        s_scr[u] = s

    def absorb(kbi, u):
        s = s_scr[u]
        m_old = m_scr[u]
        m_new = jnp.maximum(m_old, jnp.max(s, axis=0, keepdims=True))
        p = jnp.exp2((s - m_new).astype(BF16))
        acc_scr[u] = acc_scr[u] * jnp.exp2(m_old - m_new) + _dot(vt_ref[0, kbi, kv_lanes(u), :], p)
        m_scr[u] = m_new

    def step(kbi, last):
        for u in range(n_units):
            absorb(kbi - 1, u)
            scores(kbi, u, last)

    def first_scores(last):
        for u in range(n_units):
            scores(0, u, last)

    def step_body(kbi, carry):
        step(kbi, False)
        return carry

    pl.when(nkb == 1)(lambda: first_scores(True))
    pl.when(nkb > 1)(lambda: first_scores(False))
    lax.fori_loop(1, nkb - 1, step_body, 0)
    pl.when(nkb > 1)(lambda: step(nkb - 1, True))
    for u in range(n_units):
        absorb(nkb - 1, u)

    for u in range(n_units):
        acc = acc_scr[u]
        o_t = acc[:HEAD_DIM] / acc[HEAD_DIM:HEAD_DIM + 1]
        for j in range(unit_heads):
            h = u * unit_heads + j
            ot_scr[h * HEAD_DIM:(h + 1) * HEAD_DIM, :] = o_t[:, j * tq:(j + 1) * tq].astype(BF16)
    y = lax.dot_general(ot_scr[...], wo_ref[...], (((0,), (0,)), ((), ())), preferred_element_type=F32)
    o_ref[0] = _layer_norm(alpha * x_ref[0] + y, g_ref[...], b_ref[...])


def _sparse_attn(x, q, qi, wit, k, vt, kd, wo, g, b, *, tq, kb, pos0, n_keys, alpha):
    bsz, s, d = x.shape
    n_t = s // tq
    l_pad = k.shape[1]
    n_heads = q.shape[2] // HEAD_DIM
    n_kv = k.shape[2] // LANES
    group = n_heads // n_kv
    assert kb % tq == 0 and pos0 % tq == 0 and l_pad % kb == 0 and vt.shape == (bsz, l_pad // kb, k.shape[2], kb)
    slopes = tuple(2.0 ** (-8.0 * (h + 1) / n_heads) for h in range(n_heads))
    qrow = lambda c: pl.BlockSpec((1, tq, c), lambda bi, t: (bi, t, 0))
    keys = lambda c: pl.BlockSpec((1, l_pad, c), lambda bi, t: (bi, 0, 0), pipeline_mode=pl.Buffered(1))
    const2 = lambda bi, t: (0, 0)
    sel_rows = -(-l_pad // (2 * kb)) * 2 * kb
    unit_heads = min(max(UNIT_LANES // tq, 1), group)
    assert group % unit_heads == 0
    n_units, unit_w = n_heads // unit_heads, unit_heads * tq
    kern = functools.partial(_sparse_kernel, tq=tq, kb=kb, pos0=pos0, n_keys=n_keys, n_sel=min(TOPK_MAX, n_keys // 4),
                             n_heads=n_heads, group=group, slopes=slopes, alpha=alpha)
    return pl.pallas_call(
        kern,
        grid=(bsz, n_t),
        in_specs=[qrow(d), qrow(q.shape[2]), qrow(qi.shape[2]),
                  pl.BlockSpec((IDX_HEADS, tq), lambda bi, t: (0, bi * n_t + t)),
                  keys(k.shape[2]),
                  pl.BlockSpec((1,) + vt.shape[1:], lambda bi, t: (bi, 0, 0, 0), pipeline_mode=pl.Buffered(1)),
                  keys(kd.shape[2]),
                  _resident(wo.shape), pl.BlockSpec((1, d), const2), pl.BlockSpec((1, d), const2)],
        out_specs=qrow(d),
        out_shape=jax.ShapeDtypeStruct((bsz, s, d), F32),
        scratch_shapes=[pltpu.VMEM((sel_rows, tq), I32), pltpu.VMEM((sel_rows, tq), I16), pltpu.VMEM((sel_rows, tq), I16),
                        pltpu.VMEM((l_pad, tq), F32),
                        pltpu.VMEM((n_units, LANES, unit_w), BF16), pltpu.VMEM((n_units, kb, unit_w), F32),
                        pltpu.VMEM((n_units, LANES, unit_w), F32),
                        pltpu.VMEM((n_units, 1, unit_w), F32), pltpu.VMEM((q.shape[2], tq), BF16)],
        compiler_params=_params(2),
        name="sparse_attn",
    )(x, q, qi, wit, k, vt, kd, wo, g, b)


def _pad_rows(a, n):
    return jnp.pad(a, ((0, 0), (0, n - a.shape[1]), (0, 0)))


def _mixer_a(xp, xs, cache_k, cache_v, w_in, rel_bias, w_out, g, b, alpha):
    bsz, s, d = xp.shape
    dbs, t, _ = xs.shape
    width = w_out.shape[0]
    w_in = w_in.astype(BF16)
    w_out = w_out.astype(BF16)
    bias_p, bias_s = _rel_bias_tables(rel_bias, t)
    keep = min(A_PAST, s)
    q, k, vt, kf, vf = _proj_a(xp, w_in, keep)
    yp = _attn_a_prompt(xp, q, k, vt, bias_p, w_out, g, b, alpha)
    qs, ks, _, ksf, vsf = _proj_a(xs.reshape(1, dbs * t, d), w_in, dbs * t)
    n_cache = cache_k.shape[1]
    kwin = _pad_rows(jnp.concatenate([cache_k.reshape(dbs, n_cache, width).astype(BF16), ks.reshape(dbs, t, width)], 1), A_WIN)
    vwin = _pad_rows(jnp.concatenate([cache_v.reshape(dbs, n_cache, width), vsf.reshape(dbs, t, width)], 1).astype(BF16), A_WIN)
    ys = _attn_a_sample(_pad_rows(xs, A_SUB), _pad_rows(qs.reshape(dbs, t, width), A_SUB), kwin, vwin.transpose(0, 2, 1),
                        bias_s, w_out, g, b, alpha)[:, :t]
    heads = width // HEAD_DIM
    return (yp, ys, kf.reshape(bsz, keep, heads, HEAD_DIM), vf.reshape(bsz, keep, heads, HEAD_DIM),
            ksf.reshape(dbs, t, heads, HEAD_DIM), vsf.reshape(dbs, t, heads, HEAD_DIM))


def _mixer_b(xp, xs, cache_k, cache_v, cache_idx, w_in, w_out, g, b, alpha):
    bsz, s, d = xp.shape
    dbs, t, _ = xs.shape
    nq = w_out.shape[0]
    nkv = cache_k.shape[2] * cache_k.shape[3]
    nqi = IDX_HEADS * HEAD_DIM
    kvh = nkv // HEAD_DIM
    past = cache_k.shape[1]
    w_pad = jnp.pad(w_in, ((0, 0), (0, nq + 2 * nkv + nqi + LANES - w_in.shape[1]))).astype(BF16)
    w_out = w_out.astype(BF16)
    kb = 512

    q, qi, k, kf, vf, kiwi, kd, wit, vt = _proj_b(xp.reshape(bsz * s, d), w_pad, nq, nkv, nqi, kb, 0, s)
    r3 = lambda a: a.reshape(bsz, s, a.shape[1])
    yp = _sparse_attn(xp, r3(q), r3(qi), wit, r3(k), vt.reshape(bsz, s // kb, 2 * nkv, kb), r3(kd), w_out, g, b,
                      tq=256, kb=kb, pos0=0, n_keys=s, alpha=alpha)

    qs, qis, ks, ksf, vsf, kiwis, kds, wits, _ = _proj_b(xs.reshape(dbs * t, d), w_pad, nq, nkv, nqi, dbs * t, past, t)
    tq_s = LANES
    n_keys = past + t
    l_pad = -(-n_keys // kb) * kb
    s3 = lambda a: a.reshape(dbs, t, a.shape[1])
    v_all = jnp.concatenate([cache_v, vsf.reshape(dbs, t, kvh, HEAD_DIM)], axis=1).astype(BF16)
    v_all = jnp.concatenate([v_all, jnp.ones_like(v_all)], axis=-1).reshape(dbs, n_keys, 2 * nkv)
    keys_vt = _pad_rows(v_all, l_pad).reshape(dbs, l_pad // kb, kb, 2 * nkv).transpose(0, 1, 3, 2)
    kd_cache = jnp.concatenate([cache_idx, cache_idx], axis=-1).astype(BF16)
    frame = jnp.arange(past, dtype=I32)[:, None]
    pos = _pos_lanes(frame, HEAD_DIM + jnp.arange(HEAD_DIM, dtype=I32)[None, :]).astype(BF16)
    pos = jnp.broadcast_to(pos[None, :, None, :], (dbs, past, kvh, HEAD_DIM))
    k_cache = jnp.concatenate([cache_k.astype(BF16), pos], axis=-1).reshape(dbs, past, 2 * nkv)
    keys_k = _pad_rows(jnp.concatenate([k_cache, s3(ks)], 1), l_pad)
    keys_kd = _pad_rows(jnp.concatenate([kd_cache, s3(kds)], 1), l_pad)
    wit_pad = jnp.pad(wits.reshape(IDX_HEADS, dbs, t), ((0, 0), (0, 0), (0, tq_s - t))).reshape(IDX_HEADS, dbs * tq_s)
    ys = _sparse_attn(_pad_rows(xs, tq_s), _pad_rows(s3(qs), tq_s), _pad_rows(s3(qis), tq_s), wit_pad,
                      keys_k, keys_vt, keys_kd, w_out, g, b,
                      tq=tq_s, kb=kb, pos0=past, n_keys=n_keys, alpha=alpha)[:, :t]
    return (yp, ys,
            kf.reshape(bsz, s, kvh, HEAD_DIM), vf.reshape(bsz, s, kvh, HEAD_DIM), kiwi[:, :HEAD_DIM].reshape(bsz, s, HEAD_DIM),
            ksf.reshape(dbs, t, kvh, HEAD_DIM), vsf.reshape(dbs, t, kvh, HEAD_DIM), kiwis[:, :HEAD_DIM].reshape(dbs, t, HEAD_DIM))


def kernel(x_prompt, x_sample, cache_a_k, cache_a_v, cache_b_k, cache_b_v, cache_b_idx, ln_g, ln_b, ffn_w_gate, ffn_w_up, ffn_w_down, a_w_in, a_rel_bias, a_w_out, b_w_in, b_w_out):
    depth = ln_g.shape[0]
    alpha = (2.0 * depth) ** 0.25
    bsz, s, d = x_prompt.shape
    dbs, t, _ = x_sample.shape
    xp, xs = x_prompt, x_sample
    a_out, b_out = [], []

    def ffn(x, layer, i, tm):
        shp = x.shape
        y = _ffn_block(x.reshape(-1, d), ffn_w_gate[layer, i].astype(BF16), ffn_w_up[layer, i].astype(BF16),
                       ffn_w_down[layer, i].astype(BF16), ln_g[layer, 2 * i][None], ln_b[layer, 2 * i][None], alpha, tm)
        return y.reshape(shp)

    for layer in range(depth):
        j = layer // 2
        xp = ffn(xp, layer, 0, 512)
        xs = ffn(xs, layer, 0, dbs * t)
        g, b = ln_g[layer, 1][None], ln_b[layer, 1][None]
        if layer % 2 == 0:
            xp, xs, *rest = _mixer_a(xp, xs, cache_a_k[j], cache_a_v[j], a_w_in[j], a_rel_bias[j], a_w_out[j], g, b, alpha)
            a_out.append(rest)
        else:
            xp, xs, *rest = _mixer_b(xp, xs, cache_b_k[j], cache_b_v[j], cache_b_idx[j], b_w_in[j], b_w_out[j], g, b, alpha)
            b_out.append(rest)
        xp = ffn(xp, layer, 1, 512)
        xs = ffn(xs, layer, 1, dbs * t)

    stack = lambda outs, i: jnp.stack([o[i] for o in outs], 0)
    return (xp, xs,
            stack(a_out, 0), stack(a_out, 1), stack(a_out, 2), stack(a_out, 3),
            stack(b_out, 0), stack(b_out, 1), stack(b_out, 2), stack(b_out, 3), stack(b_out, 4), stack(b_out, 5))
```

```python
import functools
import math
import struct

import jax
import jax.numpy as jnp
from jax import lax
from jax.experimental import pallas as pl
from jax.experimental.pallas import tpu as pltpu

F32 = jnp.float32
BF16 = jnp.bfloat16
I32 = jnp.int32
I16 = jnp.int16

CHUNK = 64
A_PAST = 512
REL_CLIP = 128
HEAD_DIM = 64
LANES = 128
IDX_HEADS = 8
TOPK_MAX = 256
LN_EPS = 1e-5
A_SUB = 4 * CHUNK
A_WIN = A_PAST + A_SUB
VMEM_LIMIT = 52 * 1024 * 1024

LOG2E = 1.4426950408889634
POS_LANE0 = HEAD_DIM
MASK_BIAS = 1e30
ONES_ROWS = 16
FFN_CHUNK = 768
UNIT_LANES = 512
COUNT_ROWS = 16

NT_DIMS = (((1,), (1,)), ((), ()))
INT_MIN = -2147483648
I16_MIN = -32768
COUNT16_ROWS = 16
COUNT16_CHAINS = 4
NEG_INF_KEY = -2139095041


def _params(n_grid, flags=None):
    return pltpu.CompilerParams(dimension_semantics=("arbitrary",) * n_grid,
                                vmem_limit_bytes=VMEM_LIMIT, flags=flags)


def _resident(shape):
    zeros = (0,) * len(shape)
    return pl.BlockSpec(shape, lambda *_: zeros, pipeline_mode=pl.Buffered(1))


def _layer_norm(z, g, b):
    mu = jnp.mean(z, axis=-1, keepdims=True)
    d = z - mu
    var = jnp.mean(d * d, axis=-1, keepdims=True)
    return d * lax.rsqrt(var + LN_EPS) * g + b


def _dot(a, b):
    return jnp.dot(a, b, preferred_element_type=F32)


def _dot_nt(a, b):
    return lax.dot_general(a, b, NT_DIMS, preferred_element_type=F32)


def _ffn_kernel(x_ref, wg_ref, wu_ref, wd_ref, g_ref, b_ref, o_ref, h_scr, *, alpha):
    x = x_ref[...]
    xb = x.astype(BF16)
    f = wg_ref.shape[1]
    for c0 in range(0, f, FFN_CHUNK):
        sl = slice(c0, min(c0 + FFN_CHUNK, f))
        gate = _dot(xb, wg_ref[:, sl])
        up = _dot(xb, wu_ref[:, sl])
        h_scr[:, sl] = (gate * (1.0 / (1.0 + jnp.exp(-gate))) * up).astype(BF16)
    y = _dot(h_scr[...], wd_ref[...])
    o_ref[...] = _layer_norm(alpha * x + 0.5 * y, g_ref[...], b_ref[...])


def _ffn_block(x, wg, wu, wd, g, b, alpha, tm):
    m, d = x.shape
    f = wg.shape[1]
    const = lambda i: (0, 0)
    return pl.pallas_call(
        functools.partial(_ffn_kernel, alpha=alpha),
        grid=(m // tm,),
        in_specs=[pl.BlockSpec((tm, d), lambda i: (i, 0)),
                  _resident((d, f)), _resident((d, f)), _resident((f, d)),
                  pl.BlockSpec((1, d), const), pl.BlockSpec((1, d), const)],
        out_specs=pl.BlockSpec((tm, d), lambda i: (i, 0)),
        out_shape=jax.ShapeDtypeStruct((m, d), F32),
        scratch_shapes=[pltpu.VMEM((tm, f), BF16)],
        compiler_params=_params(1),
        name="ffn_ln",
    )(x, wg, wu, wd, g, b)


def _proj_a_kernel(x_ref, w_ref, q_ref, k_ref, vt_ref, kf_ref, vf_ref, *, width):
    h = _dot(x_ref[0].astype(BF16), w_ref[...])
    k = h[:, width:2 * width]
    v = h[:, 2 * width:]
    q_ref[0] = (h[:, :width] * (HEAD_DIM ** -0.5 * LOG2E)).astype(BF16)
    k_ref[0] = k.astype(BF16)
    vt_ref[0] = v.T.astype(BF16)

    @pl.when(pl.program_id(1) == pl.num_programs(1) - 1)
    def _():
        kf_ref[0] = k
        vf_ref[0] = v


def _proj_a(x, w, tm):
    bsz, s, d = x.shape
    width = w.shape[1] // 3
    row = pl.BlockSpec((1, tm, width), lambda b, t: (b, t, 0))
    last = pl.BlockSpec((1, tm, width), lambda b, t: (b, 0, 0))
    return pl.pallas_call(
        functools.partial(_proj_a_kernel, width=width),
        grid=(bsz, s // tm),
        in_specs=[pl.BlockSpec((1, tm, d), lambda b, t: (b, t, 0)), _resident(w.shape)],
        out_specs=[row, row, pl.BlockSpec((1, width, tm), lambda b, t: (b, 0, t)), last, last],
        out_shape=[jax.ShapeDtypeStruct((bsz, s, width), BF16)] * 2 + [jax.ShapeDtypeStruct((bsz, width, s), BF16)]
        + [jax.ShapeDtypeStruct((bsz, tm, width), F32)] * 2,
        compiler_params=_params(2),
        name="proj_a",
    )(x, w)


def _attn_a_kernel(*refs, n_sub, prompt, alpha):
    if prompt:
        (x_ref, q_ref, kp_ref, kc_ref, vtp_ref, vtc_ref, bias_ref, wo_ref, g_ref, b_ref,
         o_ref, kwin, q_scr, s_scr, ot_scr, vtwin) = refs
        vtwin[:, 0:A_PAST] = vtp_ref[0]
        vtwin[:, A_PAST:2 * A_PAST] = vtc_ref[0]
        vt_at = lambda rows, r0: vtwin[rows, r0:r0 + A_WIN]
    else:
        x_ref, q_ref, k_ref, vt_ref, bias_ref, wo_ref, g_ref, b_ref, o_ref, kwin, q_scr, s_scr, ot_scr = refs
        vt_at = lambda rows, r0: vt_ref[0, rows, r0:r0 + A_WIN]
    n_pairs = q_ref.shape[2] // LANES
    for p in range(n_pairs):
        cols = slice(p * LANES, (p + 1) * LANES)
        q_scr[p] = q_ref[0, :, cols]
        if prompt:
            kwin[p, 0:A_PAST] = kp_ref[0, :, cols]
            kwin[p, A_PAST:2 * A_PAST] = kc_ref[0, :, cols]
        else:
            kwin[p] = k_ref[0, :, cols]
    lane = lax.broadcasted_iota(I32, (A_SUB, LANES), 1)
    key_row = lax.broadcasted_iota(I32, (A_WIN, 2 * A_SUB), 0)
    out_row = lax.broadcasted_iota(I32, (LANES, A_SUB), 0)
    ones_rows = jnp.ones((ONES_ROWS, A_WIN), BF16)

    def scores(p, j, side, first_valid):
        r0 = j * A_SUB
        q2 = q_scr[p, r0:r0 + A_SUB, :].astype(F32)
        qt = jnp.concatenate([jnp.where(lane < HEAD_DIM, q2, 0.0).T, jnp.where(lane < HEAD_DIM, 0.0, q2).T], axis=1)
        s = _dot(kwin[p, r0:r0 + A_WIN, :], qt.astype(BF16)) + bias_ref[p]
        if first_valid > r0:
            s = jnp.where(key_row >= first_valid - r0, s, -jnp.inf)
        s_scr[side, j] = s

    def absorb(p, j, side):
        r0 = j * A_SUB
        s = s_scr[side, j]
        e = jnp.exp2((s - jnp.max(s, axis=0, keepdims=True)).astype(BF16))
        rows = pl.ds(p * LANES, LANES)
        o = _dot(jnp.concatenate([vt_at(rows, r0), ones_rows], axis=0), e)
        o = o[:LANES] / o[LANES:LANES + 1]
        ot_scr[rows, r0:r0 + A_SUB] = jnp.where(out_row < HEAD_DIM, o[:, :A_SUB], o[:, A_SUB:]).astype(BF16)

    def overlap(p_scores, p_absorb, side, first_valid):
        for j in range(n_sub):
            scores(p_scores, j, side, first_valid)
        for j in range(n_sub):
            absorb(p_absorb, j, 1 - side)

    def attend(first_valid):
        assert n_pairs % 2 == 0
        for j in range(n_sub):
            scores(0, j, 0, first_valid)

        def two_pairs(i, carry):
            overlap(2 * i + 1, 2 * i, 1, first_valid)
            overlap(2 * i + 2, 2 * i + 1, 0, first_valid)
            return carry

        for i in range(n_pairs // 2 - 1):
            two_pairs(i, 0)
        overlap(n_pairs - 1, n_pairs - 2, 1, first_valid)
        for j in range(n_sub):
            absorb(n_pairs - 1, j, 1)

    if prompt:
        pl.when(pl.program_id(1) == 0)(lambda: attend(A_PAST))
        pl.when(pl.program_id(1) > 0)(lambda: attend(0))
    else:
        attend(0)

    y = lax.dot_general(ot_scr[...], wo_ref[...], (((0,), (0,)), ((), ())), preferred_element_type=F32)
    o_ref[0] = _layer_norm(alpha * x_ref[0] + y, g_ref[...], b_ref[...])


def _attn_a_scratch(width, n_keys, n_queries):
    pairs = width // LANES
    return [pltpu.VMEM((pairs, n_keys, LANES), BF16), pltpu.VMEM((pairs, n_queries, LANES), BF16),
            pltpu.VMEM((2, n_queries // A_SUB, A_WIN, 2 * A_SUB), F32), pltpu.VMEM((width, n_queries), BF16)]


def _attn_a_prompt(x, q, k, vt, bias, wo, g, b, alpha):
    bsz, s, d = x.shape
    width = q.shape[2]
    tq = A_PAST
    cur = lambda bi, t: (bi, t, 0)
    prev = lambda bi, t: (bi, jnp.maximum(t - 1, 0), 0)
    cur_t = lambda bi, t: (bi, 0, t)
    prev_t = lambda bi, t: (bi, 0, jnp.maximum(t - 1, 0))
    const2 = lambda bi, t: (0, 0)
    return pl.pallas_call(
        functools.partial(_attn_a_kernel, n_sub=tq // A_SUB, prompt=True, alpha=alpha),
        grid=(bsz, s // tq),
        in_specs=[pl.BlockSpec((1, tq, d), cur), pl.BlockSpec((1, tq, width), cur),
                  pl.BlockSpec((1, tq, width), prev), pl.BlockSpec((1, tq, width), cur),
                  pl.BlockSpec((1, width, tq), prev_t), pl.BlockSpec((1, width, tq), cur_t),
                  _resident(bias.shape), _resident(wo.shape),
                  pl.BlockSpec((1, d), const2), pl.BlockSpec((1, d), const2)],
        out_specs=pl.BlockSpec((1, tq, d), cur),
        out_shape=jax.ShapeDtypeStruct((bsz, s, d), F32),
        scratch_shapes=_attn_a_scratch(width, 2 * tq, tq) + [pltpu.VMEM((width, 2 * tq), BF16)],
        compiler_params=_params(2),
        name="attn_a_prompt",
    )(x, q, k, k, vt, vt, bias, wo, g, b)


def _attn_a_sample(x, q, kwin, vtwin, bias, wo, g, b, alpha):
    bsz, _, d = x.shape
    width = q.shape[2]
    blk = lambda n, c: pl.BlockSpec((1, n, c), lambda bi: (bi, 0, 0))
    const2 = lambda bi: (0, 0)
    return pl.pallas_call(
        functools.partial(_attn_a_kernel, n_sub=1, prompt=False, alpha=alpha),
        grid=(bsz,),
        in_specs=[blk(A_SUB, d), blk(A_SUB, width), blk(A_WIN, width), blk(width, A_WIN),
                  _resident(bias.shape), _resident(wo.shape),
                  pl.BlockSpec((1, d), const2), pl.BlockSpec((1, d), const2)],
        out_specs=blk(A_SUB, d),
        out_shape=jax.ShapeDtypeStruct((bsz, A_SUB, d), F32),
        scratch_shapes=_attn_a_scratch(width, A_WIN, A_SUB),
        compiler_params=_params(1),
        name="attn_a_sample",
    )(x, q, kwin, vtwin, bias, wo, g, b)


def _rel_bias_tables(rel_bias, n_sample):
    r = jnp.arange(A_SUB)[:, None]
    c = jnp.arange(A_WIN)[None, :]
    period = A_SUB + A_WIN
    diff = jnp.arange(period)
    diff = jnp.where(diff < A_WIN, diff, diff - period)
    line = rel_bias[:, jnp.clip(A_PAST - diff, -REL_CLIP, REL_CLIP) + REL_CLIP].astype(F32) * LOG2E
    table = jnp.tile(line, (1, A_SUB))[:, :A_SUB * (period - 1)].reshape(-1, A_SUB, period - 1)[:, :, :A_WIN]
    lo = (r // CHUNK) * CHUNK
    band = (c >= lo) & (c < lo + A_PAST + CHUNK)
    prompt = jnp.where(band[None], table, -jnp.inf)
    live = (c < A_PAST + n_sample)
    sample = jnp.where(live[None], jnp.where((r < n_sample)[None], table, 0.0), -jnp.inf)

    def pair_layout(tab):
        h = tab.shape[0]
        return tab.transpose(0, 2, 1).reshape(h // 2, 2, A_WIN, A_SUB).transpose(0, 2, 1, 3).reshape(h // 2, A_WIN, 2 * A_SUB)

    return pair_layout(prompt), pair_layout(sample)


def _round_to_bf16(x):
    bits = struct.unpack("<I", struct.pack("<f", x))[0]
    bits = (bits + 0x7FFF + ((bits >> 16) & 1)) & 0xFFFF0000
    return struct.unpack("<f", struct.pack("<I", bits))[0]


def _bf16_pieces(x):
    p1 = _round_to_bf16(x)
    p2 = _round_to_bf16(x - p1)
    return (p1, p2, _round_to_bf16(x - p1 - p2))


def _pos_lanes(pos, lane):
    hi = (lax.shift_right_logical(pos, 6) * CHUNK).astype(F32)
    lo = (pos & (CHUNK - 1)).astype(F32)
    return jnp.where(lane < POS_LANE0 + 3, hi, jnp.where(lane < POS_LANE0 + 6, lo, 0.0))


def _proj_b_kernel(x_ref, w_ref, q_ref, qi_ref, ka_ref, kf_ref, vf_ref, ki_ref, kd_ref, wit_ref, vt_ref,
                   *, nq, nkv, nqi, pos0, period):
    h = _dot(x_ref[...].astype(BF16), w_ref[...])
    q_ref[...] = (h[:, :nq] * (HEAD_DIM ** -0.5 * LOG2E)).astype(BF16)
    k = h[:, nq:nq + nkv]
    v = h[:, nq + nkv:nq + 2 * nkv]
    kf_ref[...] = k
    vf_ref[...] = v
    o0 = nq + 2 * nkv
    qi_ref[...] = (h[:, o0:o0 + nqi] * (HEAD_DIM ** -0.5)).astype(BF16)
    tail = h[:, o0 + nqi:o0 + nqi + LANES]
    ki_ref[...] = tail[:, :HEAD_DIM]
    lane = lax.broadcasted_iota(I32, tail.shape, 1)
    kd_ref[...] = jnp.where(lane < HEAD_DIM, tail, pltpu.roll(tail, HEAD_DIM, 1)).astype(BF16)
    wit_ref[...] = tail.T[HEAD_DIM:HEAD_DIM + IDX_HEADS, :] * (IDX_HEADS ** -0.5)
    tm = tail.shape[0]
    row = pl.program_id(0) * tm + lax.broadcasted_iota(I32, (tm, 1), 0)
    pos = _pos_lanes(pos0 + (row & (period - 1)), lane)
    for pair in range(nkv // LANES):
        for half in range(2):
            tile = slice((2 * pair + half) * LANES, (2 * pair + half + 1) * LANES)
            vp = v[:, pair * LANES:(pair + 1) * LANES]
            kp = k[:, pair * LANES:(pair + 1) * LANES]
            if half:
                vp, kp = pltpu.roll(vp, HEAD_DIM, 1), pltpu.roll(kp, HEAD_DIM, 1)
            vt_ref[0, tile, :] = jnp.where(lane < HEAD_DIM, vp, 1.0).T.astype(BF16)
            ka_ref[:, tile] = jnp.where(lane < HEAD_DIM, kp, pos).astype(BF16)


def _proj_b(x, w_pad, nq, nkv, nqi, tm, pos0, period):
    m, d = x.shape
    assert period & (period - 1) == 0
    rows = lambda c: pl.BlockSpec((tm, c), lambda i: (i, 0))
    outs = [(nq, BF16), (nqi, BF16), (2 * nkv, BF16), (nkv, F32), (nkv, F32), (HEAD_DIM, F32), (LANES, BF16)]
    return pl.pallas_call(
        functools.partial(_proj_b_kernel, nq=nq, nkv=nkv, nqi=nqi, pos0=pos0, period=period),
        grid=(m // tm,),
        in_specs=[rows(d), _resident(w_pad.shape)],
        out_specs=[rows(c) for c, _ in outs] + [pl.BlockSpec((IDX_HEADS, tm), lambda i: (0, i)),
                                                 pl.BlockSpec((1, 2 * nkv, tm), lambda i: (i, 0, 0))],
        out_shape=[jax.ShapeDtypeStruct((m, c), dt) for c, dt in outs]
        + [jax.ShapeDtypeStruct((IDX_HEADS, m), F32), jax.ShapeDtypeStruct((m // tm, 2 * nkv, tm), BF16)],
        compiler_params=_params(1),
        name="proj_b",
    )(x, w_pad)


def _sparse_kernel(x_ref, q_ref, qi_ref, wit_ref, ka_ref, vt_ref, kd_ref, wo_ref, g_ref, b_ref, o_ref,
                   key_scr, hi_scr, lo_scr, bias_scr, qt_scr, s_scr, acc_scr, m_scr, ot_scr,
                   *, tq, kb, pos0, n_keys, n_sel, n_heads, group, slopes, alpha):
    t = pl.program_id(1)
    q0 = pos0 + t * tq
    qpos_row = q0 + lax.broadcasted_iota(I32, (1, tq), 1)
    lim_row = jnp.minimum((lax.shift_right_logical(qpos_row, 6) + 1) * CHUNK, n_keys)
    kmax = jnp.minimum((lax.shift_right_logical(q0 + tq - 1, 6) + 1) * CHUNK, n_keys)
    nkb = lax.shift_right_logical(kmax + kb - 1, int(math.log2(kb)))
    lane = lax.broadcasted_iota(I32, (tq, LANES), 1)
    lo_half = lane < HEAD_DIM

    def key_block(kbi):
        return pl.multiple_of(kbi * kb, kb)

    def key_index(off):
        return off + lax.broadcasted_iota(I32, (kb, tq), 0)

    qis = []
    for p in range(IDX_HEADS // 2):
        q2 = qi_ref[0, :, p * LANES:(p + 1) * LANES]
        qis.append(jnp.where(lo_half, q2, jnp.zeros_like(q2)))
        qis.append(jnp.where(lo_half, jnp.zeros_like(q2), q2))
    wis = wit_ref[...]
    qi_all = jnp.concatenate(qis, axis=0)

    def score_block(kbi, carry):
        off = key_block(kbi)
        dots = _dot_nt(kd_ref[0, pl.ds(off, kb), :], qi_all)
        acc = jnp.zeros((kb, tq), F32)
        for h in range(IDX_HEADS):
            acc = acc + jnp.maximum(dots[:, h * tq:(h + 1) * tq], 0.0) * wis[h:h + 1, :]
        acc = jnp.where(key_index(off) < lim_row, acc, -jnp.inf)
        bits = lax.bitcast_convert_type(acc, I32)
        key = bits ^ (lax.shift_right_arithmetic(bits, 31) & 0x7FFFFFFF)
        key_scr[pl.ds(off, kb), :] = key
        hi_scr[pl.ds(off, kb), :] = lax.shift_right_arithmetic(key, 16).astype(I16)
        lo_scr[pl.ds(off, kb), :] = ((key & 0xFFFF) + I16_MIN).astype(I16)
        return carry

    lax.fori_loop(0, nkb, score_block, 0)

    @pl.when(nkb % 2 == 1)
    def _():
        key_scr[pl.ds(key_block(nkb), kb), :] = jnp.full((kb, tq), INT_MIN, I32)
        hi_scr[pl.ds(key_block(nkb), kb), :] = jnp.full((kb, tq), I16_MIN, I16)
        lo_scr[pl.ds(key_block(nkb), kb), :] = jnp.full((kb, tq), I16_MIN, I16)

    n_steps = lax.shift_right_logical(nkb + 1, 1)

    def step_rows(i):
        return pl.ds(pl.multiple_of(i * 2 * kb, 2 * kb), 2 * kb)

    def count(pred):
        def body(i, part):
            hit = pred(key_scr[step_rows(i), :])
            return part + jnp.sum(jnp.where(hit, 1, 0).reshape(-1, COUNT_ROWS, tq), axis=0)
        part = lax.fori_loop(0, n_steps, body, jnp.zeros((COUNT_ROWS, tq), I32))
        return jnp.sum(part, axis=0, keepdims=True)

    def count16(ref, pred):
        def body(i, parts):
            ind = jnp.where(pred(ref[pl.ds(key_block(i), kb), :]), jnp.int16(1), jnp.int16(0))
            parts = list(parts)
            for j, r in enumerate(range(0, kb, COUNT16_ROWS)):
                parts[j % len(parts)] = parts[j % len(parts)] + ind[r:r + COUNT16_ROWS]
            return tuple(parts)
        parts = lax.fori_loop(0, 2 * n_steps, body, (jnp.zeros((COUNT16_ROWS, tq), I16),) * COUNT16_CHAINS)
        return jnp.sum(sum(p.astype(I32) for p in parts), axis=0, keepdims=True)

    def as_i16(row):
        return jnp.broadcast_to(row, (COUNT16_ROWS, tq)).astype(I16)[:1]

    def bisect16(ref, want):
        def body(i, thr):
            cand = thr + lax.shift_left(jnp.int32(1), 15 - i)
            cand16 = as_i16(cand)
            return jnp.where(count16(ref, lambda v: v >= cand16) >= want, cand, thr)
        return lax.fori_loop(0, 16, body, jnp.full((1, tq), I16_MIN, I32))

    thr_hi = bisect16(hi_scr, n_sel)
    thr_hi16 = as_i16(thr_hi)
    n_hi_gt = count16(hi_scr, lambda v: v > thr_hi16)

    def keep_bucket(i, carry):
        rows = step_rows(i)
        lo_scr[rows, :] = jnp.where(hi_scr[rows, :] == thr_hi16, lo_scr[rows, :], jnp.int16(I16_MIN))
        return carry

    lax.fori_loop(0, n_steps, keep_bucket, 0)
    thr = thr_hi * 65536 + (bisect16(lo_scr, n_sel - n_hi_gt) - I16_MIN)
    n_gt = count(lambda key: key > thr)
    n_eq = count(lambda key: key == thr)
    need = n_sel - n_gt
    tie = jnp.max(jnp.where((n_eq > need) & (thr > NEG_INF_KEY), 1, 0)) > 0

    @pl.when(jnp.logical_not(tie))
    def _():
        def body(kbi, carry):
            off = key_block(kbi)
            sel = (key_scr[pl.ds(off, kb), :] >= thr) & (key_index(off) < lim_row)
            bias_scr[pl.ds(off, kb), :] = jnp.where(sel, 0.0, -MASK_BIAS)
            return carry
        lax.fori_loop(0, nkb, body, 0)

    @pl.when(tie)
    def _():
        tri = jnp.where(lax.broadcasted_iota(I32, (kb, kb), 0) > lax.broadcasted_iota(I32, (kb, kb), 1),
                        1.0, 0.0).astype(BF16)
        need_f = need.astype(F32)

        def body(kbi, seen):
            off = key_block(kbi)
            key = key_scr[pl.ds(off, kb), :]
            eq = jnp.where(key == thr, 1.0, 0.0)
            earlier = _dot(tri, eq.astype(BF16)) + seen
            sel = ((key > thr) | ((key == thr) & (earlier < need_f))) & (key_index(off) < lim_row)
            bias_scr[pl.ds(off, kb), :] = jnp.where(sel, 0.0, -MASK_BIAS)
            return seen + jnp.sum(eq, axis=0, keepdims=True)
        lax.fori_loop(0, nkb, body, jnp.zeros((1, tq), F32))

    unit_heads = qt_scr.shape[2] // tq
    n_units = n_heads // unit_heads
    kv_lanes = lambda u: slice((u * unit_heads // group) * LANES, (u * unit_heads // group + 1) * LANES)
    for h in range(n_heads):
        q2 = q_ref[0, :, (h // 2) * LANES:(h // 2 + 1) * LANES].astype(F32)
        if h % 2:
            q2 = pltpu.roll(q2, HEAD_DIM, 1)
        sl = jnp.zeros((1, LANES), F32)
        for i, piece in enumerate(_bf16_pieces(slopes[h] * LOG2E) * 2):
            sl = jnp.where(lane[:1] == POS_LANE0 + i, piece, sl)
        j = h % unit_heads
        qt_scr[h // unit_heads, :, j * tq:(j + 1) * tq] = jnp.where(lo_half, q2, sl).T.astype(BF16)
    m_scr[...] = jnp.full(m_scr.shape, -jnp.inf, F32)
    acc_scr[...] = jnp.zeros(acc_scr.shape, F32)

    def scores(kbi, u, last):
        off = key_block(kbi)
        bias = jnp.concatenate([bias_scr[pl.ds(off, kb), :]] * unit_heads, axis=1)
        s = _dot(ka_ref[0, pl.ds(off, kb), kv_lanes(u)], qt_scr[u]) + bias
        if last:
            ahead = jnp.maximum(key_index(off) - qpos_row, 0).astype(F32)
            s = s - jnp.concatenate([(2.0 * slopes[u * unit_heads + j] * LOG2E) * ahead for j in range(unit_heads)], axis=1)
        s_scr[u] = s

    def absorb(kbi, u):
        s = s_scr[u]
        m_old = m_scr[u]
        m_new = jnp.maximum(m_old, jnp.max(s, axis=0, keepdims=True))
        p = jnp.exp2((s - m_new).astype(BF16))
        acc_scr[u] = acc_scr[u] * jnp.exp2(m_old - m_new) + _dot(vt_ref[0, kbi, kv_lanes(u), :], p)
        m_scr[u] = m_new

    def step(kbi, last):
        for u in range(n_units):
            absorb(kbi - 1, u)
            scores(kbi, u, last)

    def first_scores(last):
        for u in range(n_units):
            scores(0, u, last)

    def step_body(kbi, carry):
        step(kbi, False)
        return carry

    pl.when(nkb == 1)(lambda: first_scores(True))
    pl.when(nkb > 1)(lambda: first_scores(False))
    lax.fori_loop(1, nkb - 1, step_body, 0)
    pl.when(nkb > 1)(lambda: step(nkb - 1, True))
    for u in range(n_units):
        absorb(nkb - 1, u)

    for u in range(n_units):
        acc = acc_scr[u]
        o_t = acc[:HEAD_DIM] / acc[HEAD_DIM:HEAD_DIM + 1]
        for j in range(unit_heads):
            h = u * unit_heads + j
            ot_scr[h * HEAD_DIM:(h + 1) * HEAD_DIM, :] = o_t[:, j * tq:(j + 1) * tq].astype(BF16)
    y = lax.dot_general(ot_scr[...], wo_ref[...], (((0,), (0,)), ((), ())), preferred_element_type=F32)
    o_ref[0] = _layer_norm(alpha * x_ref[0] + y, g_ref[...], b_ref[...])


def _sparse_attn(x, q, qi, wit, k, vt, kd, wo, g, b, *, tq, kb, pos0, n_keys, alpha):
    bsz, s, d = x.shape
    n_t = s // tq
    l_pad = k.shape[1]
    n_heads = q.shape[2] // HEAD_DIM
    n_kv = k.shape[2] // LANES
    group = n_heads // n_kv
    assert kb % tq == 0 and pos0 % tq == 0 and l_pad % kb == 0 and vt.shape == (bsz, l_pad // kb, k.shape[2], kb)
    slopes = tuple(2.0 ** (-8.0 * (h + 1) / n_heads) for h in range(n_heads))
    qrow = lambda c: pl.BlockSpec((1, tq, c), lambda bi, t: (bi, t, 0))
    keys = lambda c: pl.BlockSpec((1, l_pad, c), lambda bi, t: (bi, 0, 0), pipeline_mode=pl.Buffered(1))
    const2 = lambda bi, t: (0, 0)
    sel_rows = -(-l_pad // (2 * kb)) * 2 * kb
    unit_heads = min(max(UNIT_LANES // tq, 1), group)
    assert group % unit_heads == 0
    n_units, unit_w = n_heads // unit_heads, unit_heads * tq
    kern = functools.partial(_sparse_kernel, tq=tq, kb=kb, pos0=pos0, n_keys=n_keys, n_sel=min(TOPK_MAX, n_keys // 4),
                             n_heads=n_heads, group=group, slopes=slopes, alpha=alpha)
    return pl.pallas_call(
        kern,
        grid=(bsz, n_t),
        in_specs=[qrow(d), qrow(q.shape[2]), qrow(qi.shape[2]),
                  pl.BlockSpec((IDX_HEADS, tq), lambda bi, t: (0, bi * n_t + t)),
                  keys(k.shape[2]),
                  pl.BlockSpec((1,) + vt.shape[1:], lambda bi, t: (bi, 0, 0, 0), pipeline_mode=pl.Buffered(1)),
                  keys(kd.shape[2]),
                  _resident(wo.shape), pl.BlockSpec((1, d), const2), pl.BlockSpec((1, d), const2)],
        out_specs=qrow(d),
        out_shape=jax.ShapeDtypeStruct((bsz, s, d), F32),
        scratch_shapes=[pltpu.VMEM((sel_rows, tq), I32), pltpu.VMEM((sel_rows, tq), I16), pltpu.VMEM((sel_rows, tq), I16),
                        pltpu.VMEM((l_pad, tq), F32),
                        pltpu.VMEM((n_units, LANES, unit_w), BF16), pltpu.VMEM((n_units, kb, unit_w), F32),
                        pltpu.VMEM((n_units, LANES, unit_w), F32),
                        pltpu.VMEM((n_units, 1, unit_w), F32), pltpu.VMEM((q.shape[2], tq), BF16)],
        compiler_params=_params(2),
        name="sparse_attn",
    )(x, q, qi, wit, k, vt, kd, wo, g, b)


def _pad_rows(a, n):
    return jnp.pad(a, ((0, 0), (0, n - a.shape[1]), (0, 0)))


def _mixer_a(xp, xs, cache_k, cache_v, w_in, rel_bias, w_out, g, b, alpha):
    bsz, s, d = xp.shape
    dbs, t, _ = xs.shape
    width = w_out.shape[0]
    w_in = w_in.astype(BF16)
    w_out = w_out.astype(BF16)
    bias_p, bias_s = _rel_bias_tables(rel_bias, t)
    keep = min(A_PAST, s)
    q, k, vt, kf, vf = _proj_a(xp, w_in, keep)
    yp = _attn_a_prompt(xp, q, k, vt, bias_p, w_out, g, b, alpha)
    qs, ks, _, ksf, vsf = _proj_a(xs.reshape(1, dbs * t, d), w_in, dbs * t)
    n_cache = cache_k.shape[1]
    kwin = _pad_rows(jnp.concatenate([cache_k.reshape(dbs, n_cache, width).astype(BF16), ks.reshape(dbs, t, width)], 1), A_WIN)
    vwin = _pad_rows(jnp.concatenate([cache_v.reshape(dbs, n_cache, width), vsf.reshape(dbs, t, width)], 1).astype(BF16), A_WIN)
    ys = _attn_a_sample(_pad_rows(xs, A_SUB), _pad_rows(qs.reshape(dbs, t, width), A_SUB), kwin, vwin.transpose(0, 2, 1),
                        bias_s, w_out, g, b, alpha)[:, :t]
    heads = width // HEAD_DIM
    return (yp, ys, kf.reshape(bsz, keep, heads, HEAD_DIM), vf.reshape(bsz, keep, heads, HEAD_DIM),
            ksf.reshape(dbs, t, heads, HEAD_DIM), vsf.reshape(dbs, t, heads, HEAD_DIM))


def _mixer_b(xp, xs, cache_k, cache_v, cache_idx, w_in, w_out, g, b, alpha):
    bsz, s, d = xp.shape
    dbs, t, _ = xs.shape
    nq = w_out.shape[0]
    nkv = cache_k.shape[2] * cache_k.shape[3]
    nqi = IDX_HEADS * HEAD_DIM
    kvh = nkv // HEAD_DIM
    past = cache_k.shape[1]
    w_pad = jnp.pad(w_in, ((0, 0), (0, nq + 2 * nkv + nqi + LANES - w_in.shape[1]))).astype(BF16)
    w_out = w_out.astype(BF16)
    kb = 512

    q, qi, k, kf, vf, kif, kd, wit, vt = _proj_b(xp.reshape(bsz * s, d), w_pad, nq, nkv, nqi, kb, 0, s)
    r3 = lambda a: a.reshape(bsz, s, a.shape[1])
    yp = _sparse_attn(xp, r3(q), r3(qi), wit, r3(k), vt.reshape(bsz, s // kb, 2 * nkv, kb), r3(kd), w_out, g, b,
                      tq=256, kb=kb, pos0=0, n_keys=s, alpha=alpha)

    qs, qis, ks, ksf, vsf, kisf, kds, wits, _ = _proj_b(xs.reshape(dbs * t, d), w_pad, nq, nkv, nqi, dbs * t, past, t)
    tq_s = LANES
    n_keys = past + t
    l_pad = -(-n_keys // kb) * kb
    s3 = lambda a: a.reshape(dbs, t, a.shape[1])
    v_all = jnp.concatenate([cache_v, vsf.reshape(dbs, t, kvh, HEAD_DIM)], axis=1).astype(BF16)
    v_all = jnp.concatenate([v_all, jnp.ones_like(v_all)], axis=-1).reshape(dbs, n_keys, 2 * nkv)
    keys_vt = _pad_rows(v_all, l_pad).reshape(dbs, l_pad // kb, kb, 2 * nkv).transpose(0, 1, 3, 2)
    kd_cache = jnp.concatenate([cache_idx, cache_idx], axis=-1).astype(BF16)
    frame = jnp.arange(past, dtype=I32)[:, None]
    pos = _pos_lanes(frame, HEAD_DIM + jnp.arange(HEAD_DIM, dtype=I32)[None, :]).astype(BF16)
    pos = jnp.broadcast_to(pos[None, :, None, :], (dbs, past, kvh, HEAD_DIM))
    k_cache = jnp.concatenate([cache_k.astype(BF16), pos], axis=-1).reshape(dbs, past, 2 * nkv)
    keys_k = _pad_rows(jnp.concatenate([k_cache, s3(ks)], 1), l_pad)
    keys_kd = _pad_rows(jnp.concatenate([kd_cache, s3(kds)], 1), l_pad)
    wit_pad = jnp.pad(wits.reshape(IDX_HEADS, dbs, t), ((0, 0), (0, 0), (0, tq_s - t)), mode="edge").reshape(IDX_HEADS, dbs * tq_s)
    qis_pad = jnp.pad(s3(qis), ((0, 0), (0, tq_s - t), (0, 0)), mode="edge")
    ys = _sparse_attn(_pad_rows(xs, tq_s), _pad_rows(s3(qs), tq_s), qis_pad, wit_pad,
                      keys_k, keys_vt, keys_kd, w_out, g, b,
                      tq=tq_s, kb=kb, pos0=past, n_keys=n_keys, alpha=alpha)[:, :t]
    return (yp, ys,
            kf.reshape(bsz, s, kvh, HEAD_DIM), vf.reshape(bsz, s, kvh, HEAD_DIM), kif.reshape(bsz, s, HEAD_DIM),
            ksf.reshape(dbs, t, kvh, HEAD_DIM), vsf.reshape(dbs, t, kvh, HEAD_DIM), kisf.reshape(dbs, t, HEAD_DIM))


def kernel(x_prompt, x_sample, cache_a_k, cache_a_v, cache_b_k, cache_b_v, cache_b_idx, ln_g, ln_b, ffn_w_gate, ffn_w_up, ffn_w_down, a_w_in, a_rel_bias, a_w_out, b_w_in, b_w_out):
    depth = ln_g.shape[0]
    alpha = (2.0 * depth) ** 0.25
    bsz, s, d = x_prompt.shape
    dbs, t, _ = x_sample.shape
    xp, xs = x_prompt, x_sample
    a_out, b_out = [], []

    wg_all, wu_all, wd_all = ffn_w_gate.astype(BF16), ffn_w_up.astype(BF16), ffn_w_down.astype(BF16)

    def ffn(x, layer, i, tm):
        shp = x.shape
        y = _ffn_block(x.reshape(-1, d), wg_all[layer, i], wu_all[layer, i], wd_all[layer, i],
                       ln_g[layer, 2 * i][None], ln_b[layer, 2 * i][None], alpha, tm)
        return y.reshape(shp)

    for layer in range(depth):
        j = layer // 2
        xp = ffn(xp, layer, 0, 512)
        xs = ffn(xs, layer, 0, dbs * t)
        g, b = ln_g[layer, 1][None], ln_b[layer, 1][None]
        if layer % 2 == 0:
            xp, xs, *rest = _mixer_a(xp, xs, cache_a_k[j], cache_a_v[j], a_w_in[j], a_rel_bias[j], a_w_out[j], g, b, alpha)
            a_out.append(rest)
        else:
            xp, xs, *rest = _mixer_b(xp, xs, cache_b_k[j], cache_b_v[j], cache_b_idx[j], b_w_in[j], b_w_out[j], g, b, alpha)
            b_out.append(rest)
        xp = ffn(xp, layer, 1, 512)
        xs = ffn(xs, layer, 1, dbs * t)

    stack = lambda outs, i: jnp.stack([o[i] for o in outs], 0)
    return (xp, xs,
            stack(a_out, 0), stack(a_out, 1), stack(a_out, 2), stack(a_out, 3),
            stack(b_out, 0), stack(b_out, 1), stack(b_out, 2), stack(b_out, 3), stack(b_out, 4), stack(b_out, 5))
```

```python
import functools
import math
import struct

import jax
import jax.numpy as jnp
from jax import lax
from jax.experimental import pallas as pl
from jax.experimental.pallas import tpu as pltpu

F32 = jnp.float32
BF16 = jnp.bfloat16
I32 = jnp.int32
I16 = jnp.int16

CHUNK = 64
A_PAST = 512
REL_CLIP = 128
HEAD_DIM = 64
LANES = 128
IDX_HEADS = 8
TOPK_MAX = 256
LN_EPS = 1e-5
A_SUB = 4 * CHUNK
A_WIN = A_PAST + A_SUB
VMEM_LIMIT = 52 * 1024 * 1024

LOG2E = 1.4426950408889634
POS_LANE0 = HEAD_DIM
MASK_BIAS = 1e30
ONES_ROWS = 16
FFN_CHUNK = 768
UNIT_LANES = 512
COUNT_ROWS = 16

NT_DIMS = (((1,), (1,)), ((), ()))
INT_MIN = -2147483648
I16_MIN = -32768
COUNT16_ROWS = 16
COUNT16_CHAINS = 4
NEG_INF_KEY = -2139095041


def _params(n_grid, flags=None):
    return pltpu.CompilerParams(dimension_semantics=("arbitrary",) * n_grid,
                                vmem_limit_bytes=VMEM_LIMIT, flags=flags)


def _resident(shape):
    zeros = (0,) * len(shape)
    return pl.BlockSpec(shape, lambda *_: zeros, pipeline_mode=pl.Buffered(1))


def _layer_norm(z, g, b):
    mu = jnp.mean(z, axis=-1, keepdims=True)
    d = z - mu
    var = jnp.mean(d * d, axis=-1, keepdims=True)
    return d * lax.rsqrt(var + LN_EPS) * g + b


def _dot(a, b):
    return jnp.dot(a, b, preferred_element_type=F32)


def _dot_nt(a, b):
    return lax.dot_general(a, b, NT_DIMS, preferred_element_type=F32)


def _ffn_kernel(x_ref, wg_ref, wu_ref, wd_ref, g_ref, b_ref, o_ref, h_scr, *, alpha):
    x = x_ref[...]
    xb = x.astype(BF16)
    f = wg_ref.shape[1]
    for c0 in range(0, f, FFN_CHUNK):
        sl = slice(c0, min(c0 + FFN_CHUNK, f))
        gate = _dot(xb, wg_ref[:, sl])
        up = _dot(xb, wu_ref[:, sl])
        h_scr[:, sl] = (gate * (1.0 / (1.0 + jnp.exp(-gate))) * up).astype(BF16)
    y = _dot(h_scr[...], wd_ref[...])
    o_ref[...] = _layer_norm(alpha * x + 0.5 * y, g_ref[...], b_ref[...])


def _ffn_block(x, wg, wu, wd, g, b, alpha, tm):
    m, d = x.shape
    f = wg.shape[1]
    const = lambda i: (0, 0)
    return pl.pallas_call(
        functools.partial(_ffn_kernel, alpha=alpha),
        grid=(m // tm,),
        in_specs=[pl.BlockSpec((tm, d), lambda i: (i, 0)),
                  _resident((d, f)), _resident((d, f)), _resident((f, d)),
                  pl.BlockSpec((1, d), const), pl.BlockSpec((1, d), const)],
        out_specs=pl.BlockSpec((tm, d), lambda i: (i, 0)),
        out_shape=jax.ShapeDtypeStruct((m, d), F32),
        scratch_shapes=[pltpu.VMEM((tm, f), BF16)],
        compiler_params=_params(1),
        name="ffn_ln",
    )(x, wg, wu, wd, g, b)


def _proj_a_kernel(x_ref, w_ref, q_ref, k_ref, vt_ref, kf_ref, vf_ref, *, width):
    h = _dot(x_ref[0].astype(BF16), w_ref[...])
    k = h[:, width:2 * width]
    v = h[:, 2 * width:]
    q_ref[0] = (h[:, :width] * (HEAD_DIM ** -0.5 * LOG2E)).astype(BF16)
    k_ref[0] = k.astype(BF16)
    vt_ref[0] = v.T.astype(BF16)

    @pl.when(pl.program_id(1) == pl.num_programs(1) - 1)
    def _():
        kf_ref[0] = k
        vf_ref[0] = v


def _proj_a(x, w, tm):
    bsz, s, d = x.shape
    width = w.shape[1] // 3
    row = pl.BlockSpec((1, tm, width), lambda b, t: (b, t, 0))
    last = pl.BlockSpec((1, tm, width), lambda b, t: (b, 0, 0))
    return pl.pallas_call(
        functools.partial(_proj_a_kernel, width=width),
        grid=(bsz, s // tm),
        in_specs=[pl.BlockSpec((1, tm, d), lambda b, t: (b, t, 0)), _resident(w.shape)],
        out_specs=[row, row, pl.BlockSpec((1, width, tm), lambda b, t: (b, 0, t)), last, last],
        out_shape=[jax.ShapeDtypeStruct((bsz, s, width), BF16)] * 2 + [jax.ShapeDtypeStruct((bsz, width, s), BF16)]
        + [jax.ShapeDtypeStruct((bsz, tm, width), F32)] * 2,
        compiler_params=_params(2),
        name="proj_a",
    )(x, w)


def _attn_a_kernel(*refs, n_sub, prompt, alpha):
    if prompt:
        (x_ref, q_ref, kp_ref, kc_ref, vtp_ref, vtc_ref, bias_ref, wo_ref, g_ref, b_ref,
         o_ref, kwin, q_scr, s_scr, ot_scr, vtwin) = refs
        vtwin[:, 0:A_PAST] = vtp_ref[0]
        vtwin[:, A_PAST:2 * A_PAST] = vtc_ref[0]
        vt_at = lambda rows, r0: vtwin[rows, r0:r0 + A_WIN]
    else:
        x_ref, q_ref, k_ref, vt_ref, bias_ref, wo_ref, g_ref, b_ref, o_ref, kwin, q_scr, s_scr, ot_scr = refs
        vt_at = lambda rows, r0: vt_ref[0, rows, r0:r0 + A_WIN]
    n_pairs = q_ref.shape[2] // LANES
    for p in range(n_pairs):
        cols = slice(p * LANES, (p + 1) * LANES)
        q_scr[p] = q_ref[0, :, cols]
        if prompt:
            kwin[p, 0:A_PAST] = kp_ref[0, :, cols]
            kwin[p, A_PAST:2 * A_PAST] = kc_ref[0, :, cols]
        else:
            kwin[p] = k_ref[0, :, cols]
    lane = lax.broadcasted_iota(I32, (A_SUB, LANES), 1)
    key_row = lax.broadcasted_iota(I32, (A_WIN, 2 * A_SUB), 0)
    out_row = lax.broadcasted_iota(I32, (LANES, A_SUB), 0)
    ones_rows = jnp.ones((ONES_ROWS, A_WIN), BF16)

    def scores(p, j, side, first_valid):
        r0 = j * A_SUB
        q2 = q_scr[p, r0:r0 + A_SUB, :].astype(F32)
        qt = jnp.concatenate([jnp.where(lane < HEAD_DIM, q2, 0.0).T, jnp.where(lane < HEAD_DIM, 0.0, q2).T], axis=1)
        s = _dot(kwin[p, r0:r0 + A_WIN, :], qt.astype(BF16)) + bias_ref[p]
        if first_valid > r0:
            s = jnp.where(key_row >= first_valid - r0, s, -jnp.inf)
        s_scr[side, j] = s

    def absorb(p, j, side):
        r0 = j * A_SUB
        s = s_scr[side, j]
        e = jnp.exp2((s - jnp.max(s, axis=0, keepdims=True)).astype(BF16))
        rows = pl.ds(p * LANES, LANES)
        o = _dot(jnp.concatenate([vt_at(rows, r0), ones_rows], axis=0), e)
        o = o[:LANES] / o[LANES:LANES + 1]
        ot_scr[rows, r0:r0 + A_SUB] = jnp.where(out_row < HEAD_DIM, o[:, :A_SUB], o[:, A_SUB:]).astype(BF16)

    def overlap(p_scores, p_absorb, side, first_valid):
        for j in range(n_sub):
            scores(p_scores, j, side, first_valid)
        for j in range(n_sub):
            absorb(p_absorb, j, 1 - side)

    def attend(first_valid):
        assert n_pairs % 2 == 0
        for j in range(n_sub):
            scores(0, j, 0, first_valid)

        def two_pairs(i, carry):
            overlap(2 * i + 1, 2 * i, 1, first_valid)
            overlap(2 * i + 2, 2 * i + 1, 0, first_valid)
            return carry

        for i in range(n_pairs // 2 - 1):
            two_pairs(i, 0)
        overlap(n_pairs - 1, n_pairs - 2, 1, first_valid)
        for j in range(n_sub):
            absorb(n_pairs - 1, j, 1)

    if prompt:
        pl.when(pl.program_id(1) == 0)(lambda: attend(A_PAST))
        pl.when(pl.program_id(1) > 0)(lambda: attend(0))
    else:
        attend(0)

    y = lax.dot_general(ot_scr[...], wo_ref[...], (((0,), (0,)), ((), ())), preferred_element_type=F32)
    o_ref[0] = _layer_norm(alpha * x_ref[0] + y, g_ref[...], b_ref[...])


def _attn_a_scratch(width, n_keys, n_queries):
    pairs = width // LANES
    return [pltpu.VMEM((pairs, n_keys, LANES), BF16), pltpu.VMEM((pairs, n_queries, LANES), BF16),
            pltpu.VMEM((2, n_queries // A_SUB, A_WIN, 2 * A_SUB), F32), pltpu.VMEM((width, n_queries), BF16)]


def _attn_a_prompt(x, q, k, vt, bias, wo, g, b, alpha):
    bsz, s, d = x.shape
    width = q.shape[2]
    tq = A_PAST
    cur = lambda bi, t: (bi, t, 0)
    prev = lambda bi, t: (bi, jnp.maximum(t - 1, 0), 0)
    cur_t = lambda bi, t: (bi, 0, t)
    prev_t = lambda bi, t: (bi, 0, jnp.maximum(t - 1, 0))
    const2 = lambda bi, t: (0, 0)
    return pl.pallas_call(
        functools.partial(_attn_a_kernel, n_sub=tq // A_SUB, prompt=True, alpha=alpha),
        grid=(bsz, s // tq),
        in_specs=[pl.BlockSpec((1, tq, d), cur), pl.BlockSpec((1, tq, width), cur),
                  pl.BlockSpec((1, tq, width), prev), pl.BlockSpec((1, tq, width), cur),
                  pl.BlockSpec((1, width, tq), prev_t), pl.BlockSpec((1, width, tq), cur_t),
                  _resident(bias.shape), _resident(wo.shape),
                  pl.BlockSpec((1, d), const2), pl.BlockSpec((1, d), const2)],
        out_specs=pl.BlockSpec((1, tq, d), cur),
        out_shape=jax.ShapeDtypeStruct((bsz, s, d), F32),
        scratch_shapes=_attn_a_scratch(width, 2 * tq, tq) + [pltpu.VMEM((width, 2 * tq), BF16)],
        compiler_params=_params(2),
        name="attn_a_prompt",
    )(x, q, k, k, vt, vt, bias, wo, g, b)


def _attn_a_sample(x, q, kwin, vtwin, bias, wo, g, b, alpha):
    bsz, _, d = x.shape
    width = q.shape[2]
    blk = lambda n, c: pl.BlockSpec((1, n, c), lambda bi: (bi, 0, 0))
    const2 = lambda bi: (0, 0)
    return pl.pallas_call(
        functools.partial(_attn_a_kernel, n_sub=1, prompt=False, alpha=alpha),
        grid=(bsz,),
        in_specs=[blk(A_SUB, d), blk(A_SUB, width), blk(A_WIN, width), blk(width, A_WIN),
                  _resident(bias.shape), _resident(wo.shape),
                  pl.BlockSpec((1, d), const2), pl.BlockSpec((1, d), const2)],
        out_specs=blk(A_SUB, d),
        out_shape=jax.ShapeDtypeStruct((bsz, A_SUB, d), F32),
        scratch_shapes=_attn_a_scratch(width, A_WIN, A_SUB),
        compiler_params=_params(1),
        name="attn_a_sample",
    )(x, q, kwin, vtwin, bias, wo, g, b)


def _rel_bias_tables(rel_bias, n_sample):
    r = jnp.arange(A_SUB)[:, None]
    c = jnp.arange(A_WIN)[None, :]
    period = A_SUB + A_WIN
    diff = jnp.arange(period)
    diff = jnp.where(diff < A_WIN, diff, diff - period)
    line = rel_bias[:, jnp.clip(A_PAST - diff, -REL_CLIP, REL_CLIP) + REL_CLIP].astype(F32) * LOG2E
    table = jnp.tile(line, (1, A_SUB))[:, :A_SUB * (period - 1)].reshape(-1, A_SUB, period - 1)[:, :, :A_WIN]
    lo = (r // CHUNK) * CHUNK
    band = (c >= lo) & (c < lo + A_PAST + CHUNK)
    prompt = jnp.where(band[None], table, -jnp.inf)
    live = (c < A_PAST + n_sample)
    sample = jnp.where(live[None], jnp.where((r < n_sample)[None], table, 0.0), -jnp.inf)

    def pair_layout(tab):
        h = tab.shape[0]
        return tab.transpose(0, 2, 1).reshape(h // 2, 2, A_WIN, A_SUB).transpose(0, 2, 1, 3).reshape(h // 2, A_WIN, 2 * A_SUB)

    return pair_layout(prompt), pair_layout(sample)


def _round_to_bf16(x):
    bits = struct.unpack("<I", struct.pack("<f", x))[0]
    bits = (bits + 0x7FFF + ((bits >> 16) & 1)) & 0xFFFF0000
    return struct.unpack("<f", struct.pack("<I", bits))[0]


def _bf16_pieces(x):
    p1 = _round_to_bf16(x)
    p2 = _round_to_bf16(x - p1)
    return (p1, p2, _round_to_bf16(x - p1 - p2))


def _pos_lanes(pos, lane):
    hi = (lax.shift_right_logical(pos, 6) * CHUNK).astype(F32)
    lo = (pos & (CHUNK - 1)).astype(F32)
    return jnp.where(lane < POS_LANE0 + 3, hi, jnp.where(lane < POS_LANE0 + 6, lo, 0.0))


def _proj_b_kernel(x_ref, w_ref, q_ref, qi_ref, ka_ref, kf_ref, vf_ref, ki_ref, kd_ref, wit_ref, vt_ref,
                   *, nq, nkv, nqi, pos0, period):
    h = _dot(x_ref[...].astype(BF16), w_ref[...])
    q_ref[...] = (h[:, :nq] * (HEAD_DIM ** -0.5 * LOG2E)).astype(BF16)
    k = h[:, nq:nq + nkv]
    v = h[:, nq + nkv:nq + 2 * nkv]
    kf_ref[...] = k
    vf_ref[...] = v
    o0 = nq + 2 * nkv
    qi_ref[...] = (h[:, o0:o0 + nqi] * (HEAD_DIM ** -0.5)).astype(BF16)
    tail = h[:, o0 + nqi:o0 + nqi + LANES]
    ki_ref[...] = tail[:, :HEAD_DIM]
    lane = lax.broadcasted_iota(I32, tail.shape, 1)
    kd_ref[...] = jnp.where(lane < HEAD_DIM, tail, pltpu.roll(tail, HEAD_DIM, 1)).astype(BF16)
    wit_ref[...] = tail.T[HEAD_DIM:HEAD_DIM + IDX_HEADS, :] * (IDX_HEADS ** -0.5)
    tm = tail.shape[0]
    row = pl.program_id(0) * tm + lax.broadcasted_iota(I32, (tm, 1), 0)
    pos = _pos_lanes(pos0 + (row & (period - 1)), lane)
    for pair in range(nkv // LANES):
        for half in range(2):
            tile = slice((2 * pair + half) * LANES, (2 * pair + half + 1) * LANES)
            vp = v[:, pair * LANES:(pair + 1) * LANES]
            kp = k[:, pair * LANES:(pair + 1) * LANES]
            if half:
                vp, kp = pltpu.roll(vp, HEAD_DIM, 1), pltpu.roll(kp, HEAD_DIM, 1)
            vt_ref[0, tile, :] = jnp.where(lane < HEAD_DIM, vp, 1.0).T.astype(BF16)
            ka_ref[:, tile] = jnp.where(lane < HEAD_DIM, kp, pos).astype(BF16)


def _proj_b(x, w_pad, nq, nkv, nqi, tm, pos0, period):
    m, d = x.shape
    assert period & (period - 1) == 0
    rows = lambda c: pl.BlockSpec((tm, c), lambda i: (i, 0))
    outs = [(nq, BF16), (nqi, BF16), (2 * nkv, BF16), (nkv, F32), (nkv, F32), (HEAD_DIM, F32), (LANES, BF16)]
    return pl.pallas_call(
        functools.partial(_proj_b_kernel, nq=nq, nkv=nkv, nqi=nqi, pos0=pos0, period=period),
        grid=(m // tm,),
        in_specs=[rows(d), _resident(w_pad.shape)],
        out_specs=[rows(c) for c, _ in outs] + [pl.BlockSpec((IDX_HEADS, tm), lambda i: (0, i)),
                                                 pl.BlockSpec((1, 2 * nkv, tm), lambda i: (i, 0, 0))],
        out_shape=[jax.ShapeDtypeStruct((m, c), dt) for c, dt in outs]
        + [jax.ShapeDtypeStruct((IDX_HEADS, m), F32), jax.ShapeDtypeStruct((m // tm, 2 * nkv, tm), BF16)],
        compiler_params=_params(1),
        name="proj_b",
    )(x, w_pad)


def _sparse_kernel(x_ref, q_ref, qi_ref, wit_ref, ka_ref, vt_ref, kd_ref, wo_ref, g_ref, b_ref, o_ref,
                   key_scr, hi_scr, lo_scr, bias_scr, qt_scr, s_scr, acc_scr, m_scr, ot_scr,
                   *, tq, kb, pos0, n_keys, n_sel, n_heads, group, slopes, alpha):
    t = pl.program_id(1)
    q0 = pos0 + t * tq
    qpos_row = q0 + lax.broadcasted_iota(I32, (1, tq), 1)
    lim_row = jnp.minimum((lax.shift_right_logical(qpos_row, 6) + 1) * CHUNK, n_keys)
    kmax = jnp.minimum((lax.shift_right_logical(q0 + tq - 1, 6) + 1) * CHUNK, n_keys)
    nkb = lax.shift_right_logical(kmax + kb - 1, int(math.log2(kb)))
    lane = lax.broadcasted_iota(I32, (tq, LANES), 1)
    lo_half = lane < HEAD_DIM

    def key_block(kbi):
        return pl.multiple_of(kbi * kb, kb)

    def key_index(off):
        return off + lax.broadcasted_iota(I32, (kb, tq), 0)

    qis = []
    for p in range(IDX_HEADS // 2):
        q2 = qi_ref[0, :, p * LANES:(p + 1) * LANES]
        qis.append(jnp.where(lo_half, q2, jnp.zeros_like(q2)))
        qis.append(jnp.where(lo_half, jnp.zeros_like(q2), q2))
    wis = wit_ref[...]
    qi_all = jnp.concatenate(qis, axis=0)

    def score_block(kbi, carry):
        off = key_block(kbi)
        dots = _dot_nt(kd_ref[0, pl.ds(off, kb), :], qi_all)
        acc = jnp.zeros((kb, tq), F32)
        for h in range(IDX_HEADS):
            acc = acc + jnp.maximum(dots[:, h * tq:(h + 1) * tq], 0.0) * wis[h:h + 1, :]
        acc = jnp.where(key_index(off) < lim_row, acc, -jnp.inf)
        bits = lax.bitcast_convert_type(acc, I32)
        key = bits ^ (lax.shift_right_arithmetic(bits, 31) & 0x7FFFFFFF)
        key_scr[pl.ds(off, kb), :] = key
        hi_scr[pl.ds(off, kb), :] = lax.shift_right_arithmetic(key, 16).astype(I16)
        lo_scr[pl.ds(off, kb), :] = ((key & 0xFFFF) + I16_MIN).astype(I16)
        return carry

    lax.fori_loop(0, nkb, score_block, 0)

    @pl.when(nkb % 2 == 1)
    def _():
        key_scr[pl.ds(key_block(nkb), kb), :] = jnp.full((kb, tq), INT_MIN, I32)
        hi_scr[pl.ds(key_block(nkb), kb), :] = jnp.full((kb, tq), I16_MIN, I16)
        lo_scr[pl.ds(key_block(nkb), kb), :] = jnp.full((kb, tq), I16_MIN, I16)

    n_steps = lax.shift_right_logical(nkb + 1, 1)

    def step_rows(i):
        return pl.ds(pl.multiple_of(i * 2 * kb, 2 * kb), 2 * kb)

    def count(pred):
        def body(i, part):
            hit = pred(key_scr[step_rows(i), :])
            return part + jnp.sum(jnp.where(hit, 1, 0).reshape(-1, COUNT_ROWS, tq), axis=0)
        part = lax.fori_loop(0, n_steps, body, jnp.zeros((COUNT_ROWS, tq), I32))
        return jnp.sum(part, axis=0, keepdims=True)

    def count16(ref, pred):
        def body(i, parts):
            ind = jnp.where(pred(ref[pl.ds(key_block(i), kb), :]), jnp.int16(1), jnp.int16(0))
            parts = list(parts)
            for j, r in enumerate(range(0, kb, COUNT16_ROWS)):
                parts[j % len(parts)] = parts[j % len(parts)] + ind[r:r + COUNT16_ROWS]
            return tuple(parts)
        parts = lax.fori_loop(0, 2 * n_steps, body, (jnp.zeros((COUNT16_ROWS, tq), I16),) * COUNT16_CHAINS)
        return jnp.sum(sum(p.astype(I32) for p in parts), axis=0, keepdims=True)

    def as_i16(row):
        return jnp.broadcast_to(row, (COUNT16_ROWS, tq)).astype(I16)[:1]

    def bisect16(ref, want):
        def body(i, thr):
            cand = thr + lax.shift_left(jnp.int32(1), 15 - i)
            cand16 = as_i16(cand)
            return jnp.where(count16(ref, lambda v: v >= cand16) >= want, cand, thr)
        return lax.fori_loop(0, 16, body, jnp.full((1, tq), I16_MIN, I32))

    thr_hi = bisect16(hi_scr, n_sel)
    thr_hi16 = as_i16(thr_hi)
    n_hi_gt = count16(hi_scr, lambda v: v > thr_hi16)

    def keep_bucket(i, carry):
        rows = step_rows(i)
        lo_scr[rows, :] = jnp.where(hi_scr[rows, :] == thr_hi16, lo_scr[rows, :], jnp.int16(I16_MIN))
        return carry

    lax.fori_loop(0, n_steps, keep_bucket, 0)
    thr = thr_hi * 65536 + (bisect16(lo_scr, n_sel - n_hi_gt) - I16_MIN)
    n_gt = count(lambda key: key > thr)
    n_eq = count(lambda key: key == thr)
    need = n_sel - n_gt
    tie = jnp.max(jnp.where((n_eq > need) & (thr > NEG_INF_KEY), 1, 0)) > 0

    @pl.when(jnp.logical_not(tie))
    def _():
        def body(kbi, carry):
            off = key_block(kbi)
            sel = (key_scr[pl.ds(off, kb), :] >= thr) & (key_index(off) < lim_row)
            bias_scr[pl.ds(off, kb), :] = jnp.where(sel, 0.0, -MASK_BIAS)
            return carry
        lax.fori_loop(0, nkb, body, 0)

    @pl.when(tie)
    def _():
        tri = jnp.where(lax.broadcasted_iota(I32, (kb, kb), 0) > lax.broadcasted_iota(I32, (kb, kb), 1),
                        1.0, 0.0).astype(BF16)
        need_f = need.astype(F32)

        def body(kbi, seen):
            off = key_block(kbi)
            key = key_scr[pl.ds(off, kb), :]
            eq = jnp.where(key == thr, 1.0, 0.0)
            earlier = _dot(tri, eq.astype(BF16)) + seen
            sel = ((key > thr) | ((key == thr) & (earlier < need_f))) & (key_index(off) < lim_row)
            bias_scr[pl.ds(off, kb), :] = jnp.where(sel, 0.0, -MASK_BIAS)
            return seen + jnp.sum(eq, axis=0, keepdims=True)
        lax.fori_loop(0, nkb, body, jnp.zeros((1, tq), F32))

    unit_heads = qt_scr.shape[2] // tq
    n_units = n_heads // unit_heads
    kv_lanes = lambda u: slice((u * unit_heads // group) * LANES, (u * unit_heads // group + 1) * LANES)
    for h in range(n_heads):
        q2 = q_ref[0, :, (h // 2) * LANES:(h // 2 + 1) * LANES].astype(F32)
        if h % 2:
            q2 = pltpu.roll(q2, HEAD_DIM, 1)
        sl = jnp.zeros((1, LANES), F32)
        for i, piece in enumerate(_bf16_pieces(slopes[h] * LOG2E) * 2):
            sl = jnp.where(lane[:1] == POS_LANE0 + i, piece, sl)
        j = h % unit_heads
        qt_scr[h // unit_heads, :, j * tq:(j + 1) * tq] = jnp.where(lo_half, q2, sl).T.astype(BF16)
    m_scr[...] = jnp.full(m_scr.shape, -jnp.inf, F32)
    acc_scr[...] = jnp.zeros(acc_scr.shape, F32)

    def scores(kbi, u, last):
        off = key_block(kbi)
        bias = jnp.concatenate([bias_scr[pl.ds(off, kb), :]] * unit_heads, axis=1)
        s = _dot(ka_ref[0, pl.ds(off, kb), kv_lanes(u)], qt_scr[u]) + bias
        if last:
            ahead = jnp.maximum(key_index(off) - qpos_row, 0).astype(F32)
            s = s - jnp.concatenate([(2.0 * slopes[u * unit_heads + j] * LOG2E) * ahead for j in range(unit_heads)], axis=1)
        s_scr[u] = s

    def absorb(kbi, u):
        s = s_scr[u]
        m_old = m_scr[u]
        m_new = jnp.maximum(m_old, jnp.max(s, axis=0, keepdims=True))
        p = jnp.exp2((s - m_new).astype(BF16))
        acc_scr[u] = acc_scr[u] * jnp.exp2(m_old - m_new) + _dot(vt_ref[0, kbi, kv_lanes(u), :], p)
        m_scr[u] = m_new

    def step(kbi, last):
        for u in range(n_units):
            absorb(kbi - 1, u)
            scores(kbi, u, last)

    def first_scores(last):
        for u in range(n_units):
            scores(0, u, last)

    def step_body(kbi, carry):
        step(kbi, False)
        return carry

    pl.when(nkb == 1)(lambda: first_scores(True))
    pl.when(nkb > 1)(lambda: first_scores(False))
    lax.fori_loop(1, nkb - 1, step_body, 0)
    pl.when(nkb > 1)(lambda: step(nkb - 1, True))
    for u in range(n_units):
        absorb(nkb - 1, u)

    for u in range(n_units):
        acc = acc_scr[u]
        o_t = acc[:HEAD_DIM] / acc[HEAD_DIM:HEAD_DIM + 1]
        for j in range(unit_heads):
            h = u * unit_heads + j
            ot_scr[h * HEAD_DIM:(h + 1) * HEAD_DIM, :] = o_t[:, j * tq:(j + 1) * tq].astype(BF16)
    y = lax.dot_general(ot_scr[...], wo_ref[...], (((0,), (0,)), ((), ())), preferred_element_type=F32)
    o_ref[0] = _layer_norm(alpha * x_ref[0] + y, g_ref[...], b_ref[...])


def _sparse_attn(x, q, qi, wit, k, vt, kd, wo, g, b, *, tq, kb, pos0, n_keys, alpha):
    bsz, s, d = x.shape
    n_t = s // tq
    l_pad = k.shape[1]
    n_heads = q.shape[2] // HEAD_DIM
    n_kv = k.shape[2] // LANES
    group = n_heads // n_kv
    assert kb % tq == 0 and pos0 % tq == 0 and l_pad % kb == 0 and vt.shape == (bsz, l_pad // kb, k.shape[2], kb)
    slopes = tuple(2.0 ** (-8.0 * (h + 1) / n_heads) for h in range(n_heads))
    qrow = lambda c: pl.BlockSpec((1, tq, c), lambda bi, t: (bi, t, 0))
    key_mode = pl.Buffered(1) if n_t > 1 else pl.Buffered(2)
    keys = lambda c: pl.BlockSpec((1, l_pad, c), lambda bi, t: (bi, 0, 0), pipeline_mode=key_mode)
    const2 = lambda bi, t: (0, 0)
    sel_rows = -(-l_pad // (2 * kb)) * 2 * kb
    unit_heads = min(max(UNIT_LANES // tq, 1), group)
    assert group % unit_heads == 0
    n_units, unit_w = n_heads // unit_heads, unit_heads * tq
    kern = functools.partial(_sparse_kernel, tq=tq, kb=kb, pos0=pos0, n_keys=n_keys, n_sel=min(TOPK_MAX, n_keys // 4),
                             n_heads=n_heads, group=group, slopes=slopes, alpha=alpha)
    return pl.pallas_call(
        kern,
        grid=(bsz, n_t),
        in_specs=[qrow(d), qrow(q.shape[2]), qrow(qi.shape[2]),
                  pl.BlockSpec((IDX_HEADS, tq), lambda bi, t: (0, bi * n_t + t)),
                  keys(k.shape[2]),
                  pl.BlockSpec((1,) + vt.shape[1:], lambda bi, t: (bi, 0, 0, 0), pipeline_mode=key_mode),
                  keys(kd.shape[2]),
                  _resident(wo.shape), pl.BlockSpec((1, d), const2), pl.BlockSpec((1, d), const2)],
        out_specs=qrow(d),
        out_shape=jax.ShapeDtypeStruct((bsz, s, d), F32),
        scratch_shapes=[pltpu.VMEM((sel_rows, tq), I32), pltpu.VMEM((sel_rows, tq), I16), pltpu.VMEM((sel_rows, tq), I16),
                        pltpu.VMEM((l_pad, tq), F32),
                        pltpu.VMEM((n_units, LANES, unit_w), BF16), pltpu.VMEM((n_units, kb, unit_w), F32),
                        pltpu.VMEM((n_units, LANES, unit_w), F32),
                        pltpu.VMEM((n_units, 1, unit_w), F32), pltpu.VMEM((q.shape[2], tq), BF16)],
        compiler_params=_params(2),
        name="sparse_attn",
    )(x, q, qi, wit, k, vt, kd, wo, g, b)


def _pad_rows(a, n):
    return jnp.pad(a, ((0, 0), (0, n - a.shape[1]), (0, 0)))


def _mixer_a(xp, xs, cache_k, cache_v, w_in, rel_bias, w_out, g, b, alpha):
    bsz, s, d = xp.shape
    dbs, t, _ = xs.shape
    width = w_out.shape[0]
    w_in = w_in.astype(BF16)
    w_out = w_out.astype(BF16)
    bias_p, bias_s = _rel_bias_tables(rel_bias, t)
    keep = min(A_PAST, s)
    q, k, vt, kf, vf = _proj_a(xp, w_in, keep)
    yp = _attn_a_prompt(xp, q, k, vt, bias_p, w_out, g, b, alpha)
    qs, ks, _, ksf, vsf = _proj_a(xs.reshape(1, dbs * t, d), w_in, dbs * t)
    n_cache = cache_k.shape[1]
    kwin = _pad_rows(jnp.concatenate([cache_k.reshape(dbs, n_cache, width).astype(BF16), ks.reshape(dbs, t, width)], 1), A_WIN)
    vwin = _pad_rows(jnp.concatenate([cache_v.reshape(dbs, n_cache, width), vsf.reshape(dbs, t, width)], 1).astype(BF16), A_WIN)
    ys = _attn_a_sample(_pad_rows(xs, A_SUB), _pad_rows(qs.reshape(dbs, t, width), A_SUB), kwin, vwin.transpose(0, 2, 1),
                        bias_s, w_out, g, b, alpha)[:, :t]
    heads = width // HEAD_DIM
    return (yp, ys, kf.reshape(bsz, keep, heads, HEAD_DIM), vf.reshape(bsz, keep, heads, HEAD_DIM),
            ksf.reshape(dbs, t, heads, HEAD_DIM), vsf.reshape(dbs, t, heads, HEAD_DIM))


def _mixer_b(xp, xs, cache_k, cache_v, cache_idx, w_in, w_out, g, b, alpha):
    bsz, s, d = xp.shape
    dbs, t, _ = xs.shape
    nq = w_out.shape[0]
    nkv = cache_k.shape[2] * cache_k.shape[3]
    nqi = IDX_HEADS * HEAD_DIM
    kvh = nkv // HEAD_DIM
    past = cache_k.shape[1]
    w_pad = jnp.pad(w_in, ((0, 0), (0, nq + 2 * nkv + nqi + LANES - w_in.shape[1]))).astype(BF16)
    w_out = w_out.astype(BF16)
    kb = 512

    q, qi, k, kf, vf, kif, kd, wit, vt = _proj_b(xp.reshape(bsz * s, d), w_pad, nq, nkv, nqi, kb, 0, s)
    r3 = lambda a: a.reshape(bsz, s, a.shape[1])
    yp = _sparse_attn(xp, r3(q), r3(qi), wit, r3(k), vt.reshape(bsz, s // kb, 2 * nkv, kb), r3(kd), w_out, g, b,
                      tq=256, kb=kb, pos0=0, n_keys=s, alpha=alpha)

    qs, qis, ks, ksf, vsf, kisf, kds, wits, _ = _proj_b(xs.reshape(dbs * t, d), w_pad, nq, nkv, nqi, dbs * t, past, t)
    tq_s = LANES
    n_keys = past + t
    l_pad = -(-n_keys // kb) * kb
    s3 = lambda a: a.reshape(dbs, t, a.shape[1])
    v_all = jnp.concatenate([cache_v, vsf.reshape(dbs, t, kvh, HEAD_DIM)], axis=1).astype(BF16)
    v_all = jnp.concatenate([v_all, jnp.ones_like(v_all)], axis=-1).reshape(dbs, n_keys, 2 * nkv)
    keys_vt = _pad_rows(v_all, l_pad).reshape(dbs, l_pad // kb, kb, 2 * nkv).transpose(0, 1, 3, 2)
    kd_cache = jnp.concatenate([cache_idx, cache_idx], axis=-1).astype(BF16)
    frame = jnp.arange(past, dtype=I32)[:, None]
    pos = _pos_lanes(frame, HEAD_DIM + jnp.arange(HEAD_DIM, dtype=I32)[None, :]).astype(BF16)
    pos = jnp.broadcast_to(pos[None, :, None, :], (dbs, past, kvh, HEAD_DIM))
    k_cache = jnp.concatenate([cache_k.astype(BF16), pos], axis=-1).reshape(dbs, past, 2 * nkv)
    keys_k = _pad_rows(jnp.concatenate([k_cache, s3(ks)], 1), l_pad)
    keys_kd = _pad_rows(jnp.concatenate([kd_cache, s3(kds)], 1), l_pad)
    wit_pad = jnp.pad(wits.reshape(IDX_HEADS, dbs, t), ((0, 0), (0, 0), (0, tq_s - t)), mode="edge").reshape(IDX_HEADS, dbs * tq_s)
    qis_pad = jnp.pad(s3(qis), ((0, 0), (0, tq_s - t), (0, 0)), mode="edge")
    ys = _sparse_attn(_pad_rows(xs, tq_s), _pad_rows(s3(qs), tq_s), qis_pad, wit_pad,
                      keys_k, keys_vt, keys_kd, w_out, g, b,
                      tq=tq_s, kb=kb, pos0=past, n_keys=n_keys, alpha=alpha)[:, :t]
    return (yp, ys,
            kf.reshape(bsz, s, kvh, HEAD_DIM), vf.reshape(bsz, s, kvh, HEAD_DIM), kif.reshape(bsz, s, HEAD_DIM),
            ksf.reshape(dbs, t, kvh, HEAD_DIM), vsf.reshape(dbs, t, kvh, HEAD_DIM), kisf.reshape(dbs, t, HEAD_DIM))


def kernel(x_prompt, x_sample, cache_a_k, cache_a_v, cache_b_k, cache_b_v, cache_b_idx, ln_g, ln_b, ffn_w_gate, ffn_w_up, ffn_w_down, a_w_in, a_rel_bias, a_w_out, b_w_in, b_w_out):
    depth = ln_g.shape[0]
    alpha = (2.0 * depth) ** 0.25
    bsz, s, d = x_prompt.shape
    dbs, t, _ = x_sample.shape
    xp, xs = x_prompt, x_sample
    a_out, b_out = [], []

    wg_all, wu_all, wd_all = ffn_w_gate.astype(BF16), ffn_w_up.astype(BF16), ffn_w_down.astype(BF16)

    def ffn(x, layer, i, tm):
        shp = x.shape
        y = _ffn_block(x.reshape(-1, d), wg_all[layer, i], wu_all[layer, i], wd_all[layer, i],
                       ln_g[layer, 2 * i][None], ln_b[layer, 2 * i][None], alpha, tm)
        return y.reshape(shp)

    for layer in range(depth):
        j = layer // 2
        xp = ffn(xp, layer, 0, 512)
        xs = ffn(xs, layer, 0, dbs * t)
        g, b = ln_g[layer, 1][None], ln_b[layer, 1][None]
        if layer % 2 == 0:
            xp, xs, *rest = _mixer_a(xp, xs, cache_a_k[j], cache_a_v[j], a_w_in[j], a_rel_bias[j], a_w_out[j], g, b, alpha)
            a_out.append(rest)
        else:
            xp, xs, *rest = _mixer_b(xp, xs, cache_b_k[j], cache_b_v[j], cache_b_idx[j], b_w_in[j], b_w_out[j], g, b, alpha)
            b_out.append(rest)
        xp = ffn(xp, layer, 1, 512)
        xs = ffn(xs, layer, 1, dbs * t)

    stack = lambda outs, i: jnp.stack([o[i] for o in outs], 0)
    return (xp, xs,
            stack(a_out, 0), stack(a_out, 1), stack(a_out, 2), stack(a_out, 3),
            stack(b_out, 0), stack(b_out, 1), stack(b_out, 2), stack(b_out, 3), stack(b_out, 4), stack(b_out, 5))
```

```python
import functools
import math
import struct

import jax
import jax.numpy as jnp
from jax import lax
from jax.experimental import pallas as pl
from jax.experimental.pallas import tpu as pltpu

F32 = jnp.float32
BF16 = jnp.bfloat16
I32 = jnp.int32
I16 = jnp.int16

CHUNK = 64
CHUNK_BITS = 6
HALF_BITS = 16
FFN_ROWS = 1024
PROJ_ROWS = 512
B_QUERY_TILE = 256
B_SAMPLE_TILE = 128
A_PAST = 512
REL_CLIP = 128
HEAD_DIM = 64
LANES = 128
IDX_HEADS = 8
TOPK_MAX = 256
LN_EPS = 1e-5
A_SUB = 4 * CHUNK
A_WIN = A_PAST + A_SUB
VMEM_LIMIT = 52 * 1024 * 1024

LOG2E = 1.4426950408889634
POS_LANE0 = HEAD_DIM
MASK_BIAS = 1e30
ONES_ROWS = 16
FFN_CHUNK = 768
UNIT_LANES = 512
COUNT_ROWS = 16

assert 1 << CHUNK_BITS == CHUNK

NT_DIMS = (((1,), (1,)), ((), ()))
INT_MIN = -2147483648
I16_MIN = -32768
COUNT16_ROWS = 16
COUNT16_CHAINS = 4
NEG_INF_KEY = -2139095041


def _params(n_grid, flags=None):
    return pltpu.CompilerParams(dimension_semantics=("arbitrary",) * n_grid,
                                vmem_limit_bytes=VMEM_LIMIT, flags=flags)


def _resident(shape):
    zeros = (0,) * len(shape)
    return pl.BlockSpec(shape, lambda *_: zeros, pipeline_mode=pl.Buffered(1))


def _layer_norm(z, g, b):
    mu = jnp.mean(z, axis=-1, keepdims=True)
    d = z - mu
    var = jnp.mean(d * d, axis=-1, keepdims=True)
    return d * lax.rsqrt(var + LN_EPS) * g + b


def _dot(a, b):
    return jnp.dot(a, b, preferred_element_type=F32)


def _dot_nt(a, b):
    return lax.dot_general(a, b, NT_DIMS, preferred_element_type=F32)


def _ffn_kernel(x_ref, wg_ref, wu_ref, wd_ref, g_ref, b_ref, o_ref, h_scr, *, alpha):
    x = x_ref[...]
    xb = x.astype(BF16)
    f = wg_ref.shape[1]
    for c0 in range(0, f, FFN_CHUNK):
        sl = slice(c0, min(c0 + FFN_CHUNK, f))
        gate = _dot(xb, wg_ref[:, sl])
        up = _dot(xb, wu_ref[:, sl])
        h_scr[:, sl] = (gate * (1.0 / (1.0 + jnp.exp(-gate))) * up).astype(BF16)
    y = _dot(h_scr[...], wd_ref[...])
    o_ref[...] = _layer_norm(alpha * x + 0.5 * y, g_ref[...], b_ref[...])


def _ffn_block(x, wg, wu, wd, g, b, alpha, tm):
    m, d = x.shape
    f = wg.shape[1]
    const = lambda i: (0, 0)
    return pl.pallas_call(
        functools.partial(_ffn_kernel, alpha=alpha),
        grid=(m // tm,),
        in_specs=[pl.BlockSpec((tm, d), lambda i: (i, 0)),
                  _resident((d, f)), _resident((d, f)), _resident((f, d)),
                  pl.BlockSpec((1, d), const), pl.BlockSpec((1, d), const)],
        out_specs=pl.BlockSpec((tm, d), lambda i: (i, 0)),
        out_shape=jax.ShapeDtypeStruct((m, d), F32),
        scratch_shapes=[pltpu.VMEM((tm, f), BF16)],
        compiler_params=_params(1),
        name="ffn_ln",
    )(x, wg, wu, wd, g, b)


def _proj_a_kernel(x_ref, w_ref, q_ref, k_ref, vt_ref, kf_ref, vf_ref, *, width):
    h = _dot(x_ref[0].astype(BF16), w_ref[...])
    k = h[:, width:2 * width]
    v = h[:, 2 * width:]
    q_ref[0] = (h[:, :width] * (HEAD_DIM ** -0.5 * LOG2E)).astype(BF16)
    k_ref[0] = k.astype(BF16)
    vt_ref[0] = v.T.astype(BF16)

    @pl.when(pl.program_id(1) == pl.num_programs(1) - 1)
    def _():
        kf_ref[0] = k
        vf_ref[0] = v


def _proj_a(x, w, tm):
    bsz, s, d = x.shape
    width = w.shape[1] // 3
    row = pl.BlockSpec((1, tm, width), lambda b, t: (b, t, 0))
    last = pl.BlockSpec((1, tm, width), lambda b, t: (b, 0, 0))
    return pl.pallas_call(
        functools.partial(_proj_a_kernel, width=width),
        grid=(bsz, s // tm),
        in_specs=[pl.BlockSpec((1, tm, d), lambda b, t: (b, t, 0)), _resident(w.shape)],
        out_specs=[row, row, pl.BlockSpec((1, width, tm), lambda b, t: (b, 0, t)), last, last],
        out_shape=[jax.ShapeDtypeStruct((bsz, s, width), BF16)] * 2 + [jax.ShapeDtypeStruct((bsz, width, s), BF16)]
        + [jax.ShapeDtypeStruct((bsz, tm, width), F32)] * 2,
        compiler_params=_params(2),
        name="proj_a",
    )(x, w)


def _attn_a_kernel(*refs, n_sub, prompt, alpha):
    if prompt:
        (x_ref, q_ref, kp_ref, kc_ref, vtp_ref, vtc_ref, bias_ref, wo_ref, g_ref, b_ref,
         o_ref, kwin, q_scr, s_scr, ot_scr, vtwin) = refs
        vtwin[:, 0:A_PAST] = vtp_ref[0]
        vtwin[:, A_PAST:2 * A_PAST] = vtc_ref[0]
        vt_at = lambda rows, r0: vtwin[rows, r0:r0 + A_WIN]
    else:
        x_ref, q_ref, k_ref, vt_ref, bias_ref, wo_ref, g_ref, b_ref, o_ref, kwin, q_scr, s_scr, ot_scr = refs
        vt_at = lambda rows, r0: vt_ref[0, rows, r0:r0 + A_WIN]
    n_pairs = q_ref.shape[2] // LANES
    for p in range(n_pairs):
        cols = slice(p * LANES, (p + 1) * LANES)
        q_scr[p] = q_ref[0, :, cols]
        if prompt:
            kwin[p, 0:A_PAST] = kp_ref[0, :, cols]
            kwin[p, A_PAST:2 * A_PAST] = kc_ref[0, :, cols]
        else:
            kwin[p] = k_ref[0, :, cols]
    lane = lax.broadcasted_iota(I32, (A_SUB, LANES), 1)
    key_row = lax.broadcasted_iota(I32, (A_WIN, 2 * A_SUB), 0)
    out_row = lax.broadcasted_iota(I32, (LANES, A_SUB), 0)
    ones_rows = jnp.ones((ONES_ROWS, A_WIN), BF16)

    def scores(p, j, side, first_valid):
        r0 = j * A_SUB
        q2 = q_scr[p, r0:r0 + A_SUB, :].astype(F32)
        qt = jnp.concatenate([jnp.where(lane < HEAD_DIM, q2, 0.0).T, jnp.where(lane < HEAD_DIM, 0.0, q2).T], axis=1)
        s = _dot(kwin[p, r0:r0 + A_WIN, :], qt.astype(BF16)) + bias_ref[p]
        if first_valid > r0:
            s = jnp.where(key_row >= first_valid - r0, s, -jnp.inf)
        s_scr[side, j] = s

    def absorb(p, j, side):
        r0 = j * A_SUB
        s = s_scr[side, j]
        e = jnp.exp2((s - jnp.max(s, axis=0, keepdims=True)).astype(BF16))
        rows = pl.ds(p * LANES, LANES)
        o = _dot(jnp.concatenate([vt_at(rows, r0), ones_rows], axis=0), e)
        o = o[:LANES] / o[LANES:LANES + 1]
        ot_scr[rows, r0:r0 + A_SUB] = jnp.where(out_row < HEAD_DIM, o[:, :A_SUB], o[:, A_SUB:]).astype(BF16)

    def overlap(p_scores, p_absorb, side, first_valid):
        for j in range(n_sub):
            scores(p_scores, j, side, first_valid)
        for j in range(n_sub):
            absorb(p_absorb, j, 1 - side)

    def attend(first_valid):
        assert n_pairs % 2 == 0
        for j in range(n_sub):
            scores(0, j, 0, first_valid)

        def two_pairs(i, carry):
            overlap(2 * i + 1, 2 * i, 1, first_valid)
            overlap(2 * i + 2, 2 * i + 1, 0, first_valid)
            return carry

        for i in range(n_pairs // 2 - 1):
            two_pairs(i, 0)
        overlap(n_pairs - 1, n_pairs - 2, 1, first_valid)
        for j in range(n_sub):
            absorb(n_pairs - 1, j, 1)

    if prompt:
        pl.when(pl.program_id(1) == 0)(lambda: attend(A_PAST))
        pl.when(pl.program_id(1) > 0)(lambda: attend(0))
    else:
        attend(0)

    y = lax.dot_general(ot_scr[...], wo_ref[...], (((0,), (0,)), ((), ())), preferred_element_type=F32)
    o_ref[0] = _layer_norm(alpha * x_ref[0] + y, g_ref[...], b_ref[...])


def _attn_a_scratch(width, n_keys, n_queries):
    pairs = width // LANES
    return [pltpu.VMEM((pairs, n_keys, LANES), BF16), pltpu.VMEM((pairs, n_queries, LANES), BF16),
            pltpu.VMEM((2, n_queries // A_SUB, A_WIN, 2 * A_SUB), F32), pltpu.VMEM((width, n_queries), BF16)]


def _attn_a_prompt(x, q, k, vt, bias, wo, g, b, alpha):
    bsz, s, d = x.shape
    width = q.shape[2]
    tq = A_PAST
    cur = lambda bi, t: (bi, t, 0)
    prev = lambda bi, t: (bi, jnp.maximum(t - 1, 0), 0)
    cur_t = lambda bi, t: (bi, 0, t)
    prev_t = lambda bi, t: (bi, 0, jnp.maximum(t - 1, 0))
    const2 = lambda bi, t: (0, 0)
    return pl.pallas_call(
        functools.partial(_attn_a_kernel, n_sub=tq // A_SUB, prompt=True, alpha=alpha),
        grid=(bsz, s // tq),
        in_specs=[pl.BlockSpec((1, tq, d), cur), pl.BlockSpec((1, tq, width), cur),
                  pl.BlockSpec((1, tq, width), prev), pl.BlockSpec((1, tq, width), cur),
                  pl.BlockSpec((1, width, tq), prev_t), pl.BlockSpec((1, width, tq), cur_t),
                  _resident(bias.shape), _resident(wo.shape),
                  pl.BlockSpec((1, d), const2), pl.BlockSpec((1, d), const2)],
        out_specs=pl.BlockSpec((1, tq, d), cur),
        out_shape=jax.ShapeDtypeStruct((bsz, s, d), F32),
        scratch_shapes=_attn_a_scratch(width, 2 * tq, tq) + [pltpu.VMEM((width, 2 * tq), BF16)],
        compiler_params=_params(2),
        name="attn_a_prompt",
    )(x, q, k, k, vt, vt, bias, wo, g, b)


def _attn_a_sample(x, q, kwin, vtwin, bias, wo, g, b, alpha):
    bsz, _, d = x.shape
    width = q.shape[2]
    blk = lambda n, c: pl.BlockSpec((1, n, c), lambda bi: (bi, 0, 0))
    const2 = lambda bi: (0, 0)
    return pl.pallas_call(
        functools.partial(_attn_a_kernel, n_sub=1, prompt=False, alpha=alpha),
        grid=(bsz,),
        in_specs=[blk(A_SUB, d), blk(A_SUB, width), blk(A_WIN, width), blk(width, A_WIN),
                  _resident(bias.shape), _resident(wo.shape),
                  pl.BlockSpec((1, d), const2), pl.BlockSpec((1, d), const2)],
        out_specs=blk(A_SUB, d),
        out_shape=jax.ShapeDtypeStruct((bsz, A_SUB, d), F32),
        scratch_shapes=_attn_a_scratch(width, A_WIN, A_SUB),
        compiler_params=_params(1),
        name="attn_a_sample",
    )(x, q, kwin, vtwin, bias, wo, g, b)


def _rel_bias_tables(rel_bias, n_sample):
    r = jnp.arange(A_SUB)[:, None]
    c = jnp.arange(A_WIN)[None, :]
    period = A_SUB + A_WIN
    diff = jnp.arange(period)
    diff = jnp.where(diff < A_WIN, diff, diff - period)
    line = rel_bias[:, jnp.clip(A_PAST - diff, -REL_CLIP, REL_CLIP) + REL_CLIP].astype(F32) * LOG2E
    table = jnp.tile(line, (1, A_SUB))[:, :A_SUB * (period - 1)].reshape(-1, A_SUB, period - 1)[:, :, :A_WIN]
    lo = (r // CHUNK) * CHUNK
    band = (c >= lo) & (c < lo + A_PAST + CHUNK)
    prompt = jnp.where(band[None], table, -jnp.inf)
    live = (c < A_PAST + n_sample)
    sample = jnp.where(live[None], jnp.where((r < n_sample)[None], table, 0.0), -jnp.inf)

    def pair_layout(tab):
        h = tab.shape[0]
        return tab.transpose(0, 2, 1).reshape(h // 2, 2, A_WIN, A_SUB).transpose(0, 2, 1, 3).reshape(h // 2, A_WIN, 2 * A_SUB)

    return pair_layout(prompt), pair_layout(sample)


def _round_to_bf16(x):
    bits = struct.unpack("<I", struct.pack("<f", x))[0]
    bits = (bits + 0x7FFF + ((bits >> 16) & 1)) & 0xFFFF0000
    return struct.unpack("<f", struct.pack("<I", bits))[0]


def _bf16_pieces(x):
    p1 = _round_to_bf16(x)
    p2 = _round_to_bf16(x - p1)
    return (p1, p2, _round_to_bf16(x - p1 - p2))


def _pos_lanes(pos, lane):
    hi = (lax.shift_right_logical(pos, CHUNK_BITS) * CHUNK).astype(F32)
    lo = (pos & (CHUNK - 1)).astype(F32)
    return jnp.where(lane < POS_LANE0 + 3, hi, jnp.where(lane < POS_LANE0 + 6, lo, 0.0))


def _proj_b_kernel(x_ref, w_ref, q_ref, qi_ref, ka_ref, kf_ref, vf_ref, ki_ref, kd_ref, wit_ref, vt_ref,
                   *, nq, nkv, nqi, pos0, period):
    h = _dot(x_ref[...].astype(BF16), w_ref[...])
    q_ref[...] = (h[:, :nq] * (HEAD_DIM ** -0.5 * LOG2E)).astype(BF16)
    k = h[:, nq:nq + nkv]
    v = h[:, nq + nkv:nq + 2 * nkv]
    kf_ref[...] = k
    vf_ref[...] = v
    o0 = nq + 2 * nkv
    qi_ref[...] = (h[:, o0:o0 + nqi] * (HEAD_DIM ** -0.5)).astype(BF16)
    tail = h[:, o0 + nqi:o0 + nqi + LANES]
    ki_ref[...] = tail[:, :HEAD_DIM]
    lane = lax.broadcasted_iota(I32, tail.shape, 1)
    kd_ref[...] = jnp.where(lane < HEAD_DIM, tail, pltpu.roll(tail, HEAD_DIM, 1)).astype(BF16)
    wit_ref[...] = tail.T[HEAD_DIM:HEAD_DIM + IDX_HEADS, :] * (IDX_HEADS ** -0.5)
    tm = tail.shape[0]
    row = pl.program_id(0) * tm + lax.broadcasted_iota(I32, (tm, 1), 0)
    pos = _pos_lanes(pos0 + (row & (period - 1)), lane)
    for pair in range(nkv // LANES):
        for half in range(2):
            tile = slice((2 * pair + half) * LANES, (2 * pair + half + 1) * LANES)
            vp = v[:, pair * LANES:(pair + 1) * LANES]
            kp = k[:, pair * LANES:(pair + 1) * LANES]
            if half:
                vp, kp = pltpu.roll(vp, HEAD_DIM, 1), pltpu.roll(kp, HEAD_DIM, 1)
            vt_ref[0, tile, :] = jnp.where(lane < HEAD_DIM, vp, 1.0).T.astype(BF16)
            ka_ref[:, tile] = jnp.where(lane < HEAD_DIM, kp, pos).astype(BF16)


def _proj_b(x, w_pad, nq, nkv, nqi, tm, pos0, period):
    m, d = x.shape
    assert period & (period - 1) == 0
    rows = lambda c: pl.BlockSpec((tm, c), lambda i: (i, 0))
    outs = [(nq, BF16), (nqi, BF16), (2 * nkv, BF16), (nkv, F32), (nkv, F32), (HEAD_DIM, F32), (LANES, BF16)]
    return pl.pallas_call(
        functools.partial(_proj_b_kernel, nq=nq, nkv=nkv, nqi=nqi, pos0=pos0, period=period),
        grid=(m // tm,),
        in_specs=[rows(d), _resident(w_pad.shape)],
        out_specs=[rows(c) for c, _ in outs] + [pl.BlockSpec((IDX_HEADS, tm), lambda i: (0, i)),
                                                 pl.BlockSpec((1, 2 * nkv, tm), lambda i: (i, 0, 0))],
        out_shape=[jax.ShapeDtypeStruct((m, c), dt) for c, dt in outs]
        + [jax.ShapeDtypeStruct((IDX_HEADS, m), F32), jax.ShapeDtypeStruct((m // tm, 2 * nkv, tm), BF16)],
        compiler_params=_params(1),
        name="proj_b",
    )(x, w_pad)


def _sparse_kernel(x_ref, q_ref, qi_ref, wit_ref, ka_ref, vt_ref, kd_ref, wo_ref, g_ref, b_ref, o_ref,
                   key_scr, hi_scr, lo_scr, bias_scr, qt_scr, s_scr, acc_scr, m_scr, ot_scr,
                   *, tq, kb, pos0, n_keys, n_sel, n_heads, group, slopes, alpha):
    t = pl.program_id(1)
    q0 = pos0 + t * tq
    qpos_row = q0 + lax.broadcasted_iota(I32, (1, tq), 1)
    lim_row = jnp.minimum((lax.shift_right_logical(qpos_row, CHUNK_BITS) + 1) * CHUNK, n_keys)
    kmax = jnp.minimum((lax.shift_right_logical(q0 + tq - 1, CHUNK_BITS) + 1) * CHUNK, n_keys)
    nkb = lax.shift_right_logical(kmax + kb - 1, int(math.log2(kb)))
    lane = lax.broadcasted_iota(I32, (tq, LANES), 1)
    lo_half = lane < HEAD_DIM

    def key_block(kbi):
        return pl.multiple_of(kbi * kb, kb)

    def key_index(off):
        return off + lax.broadcasted_iota(I32, (kb, tq), 0)

    qis = []
    for p in range(IDX_HEADS // 2):
        q2 = qi_ref[0, :, p * LANES:(p + 1) * LANES]
        qis.append(jnp.where(lo_half, q2, jnp.zeros_like(q2)))
        qis.append(jnp.where(lo_half, jnp.zeros_like(q2), q2))
    wis = wit_ref[...]
    qi_all = jnp.concatenate(qis, axis=0)

    def score_block(kbi, carry):
        off = key_block(kbi)
        dots = _dot_nt(kd_ref[0, pl.ds(off, kb), :], qi_all)
        acc = jnp.zeros((kb, tq), F32)
        for h in range(IDX_HEADS):
            acc = acc + jnp.maximum(dots[:, h * tq:(h + 1) * tq], 0.0) * wis[h:h + 1, :]
        acc = jnp.where(key_index(off) < lim_row, acc, -jnp.inf)
        bits = lax.bitcast_convert_type(acc, I32)
        key = bits ^ (lax.shift_right_arithmetic(bits, 31) & 0x7FFFFFFF)
        key_scr[pl.ds(off, kb), :] = key
        hi_scr[pl.ds(off, kb), :] = lax.shift_right_arithmetic(key, HALF_BITS).astype(I16)
        lo_scr[pl.ds(off, kb), :] = ((key & ((1 << HALF_BITS) - 1)) + I16_MIN).astype(I16)
        return carry

    lax.fori_loop(0, nkb, score_block, 0)

    @pl.when(nkb % 2 == 1)
    def _():
        key_scr[pl.ds(key_block(nkb), kb), :] = jnp.full((kb, tq), INT_MIN, I32)
        hi_scr[pl.ds(key_block(nkb), kb), :] = jnp.full((kb, tq), I16_MIN, I16)
        lo_scr[pl.ds(key_block(nkb), kb), :] = jnp.full((kb, tq), I16_MIN, I16)

    n_steps = lax.shift_right_logical(nkb + 1, 1)

    def step_rows(i):
        return pl.ds(pl.multiple_of(i * 2 * kb, 2 * kb), 2 * kb)

    def count(pred):
        def body(i, part):
            hit = pred(key_scr[step_rows(i), :])
            return part + jnp.sum(jnp.where(hit, 1, 0).reshape(-1, COUNT_ROWS, tq), axis=0)
        part = lax.fori_loop(0, n_steps, body, jnp.zeros((COUNT_ROWS, tq), I32))
        return jnp.sum(part, axis=0, keepdims=True)

    def count16(ref, pred):
        def body(i, parts):
            ind = jnp.where(pred(ref[pl.ds(key_block(i), kb), :]), jnp.int16(1), jnp.int16(0))
            parts = list(parts)
            for j, r in enumerate(range(0, kb, COUNT16_ROWS)):
                parts[j % len(parts)] = parts[j % len(parts)] + ind[r:r + COUNT16_ROWS]
            return tuple(parts)
        parts = lax.fori_loop(0, 2 * n_steps, body, (jnp.zeros((COUNT16_ROWS, tq), I16),) * COUNT16_CHAINS)
        return jnp.sum(sum(p.astype(I32) for p in parts), axis=0, keepdims=True)

    def as_i16(row):
        return jnp.broadcast_to(row, (COUNT16_ROWS, tq)).astype(I16)[:1]

    def bisect16(ref, want):
        def body(i, thr):
            cand = thr + lax.shift_left(jnp.int32(1), HALF_BITS - 1 - i)
            cand16 = as_i16(cand)
            return jnp.where(count16(ref, lambda v: v >= cand16) >= want, cand, thr)
        return lax.fori_loop(0, HALF_BITS, body, jnp.full((1, tq), I16_MIN, I32))

    thr_hi = bisect16(hi_scr, n_sel)
    thr_hi16 = as_i16(thr_hi)
    n_hi_gt = count16(hi_scr, lambda v: v > thr_hi16)

    def keep_bucket(i, carry):
        rows = step_rows(i)
        lo_scr[rows, :] = jnp.where(hi_scr[rows, :] == thr_hi16, lo_scr[rows, :], jnp.int16(I16_MIN))
        return carry

    lax.fori_loop(0, n_steps, keep_bucket, 0)
    thr = thr_hi * (1 << HALF_BITS) + (bisect16(lo_scr, n_sel - n_hi_gt) - I16_MIN)
    n_gt = count(lambda key: key > thr)
    n_eq = count(lambda key: key == thr)
    need = n_sel - n_gt
    tie = jnp.max(jnp.where((n_eq > need) & (thr > NEG_INF_KEY), 1, 0)) > 0

    @pl.when(jnp.logical_not(tie))
    def _():
        def body(kbi, carry):
            off = key_block(kbi)
            sel = (key_scr[pl.ds(off, kb), :] >= thr) & (key_index(off) < lim_row)
            bias_scr[pl.ds(off, kb), :] = jnp.where(sel, 0.0, -MASK_BIAS)
            return carry
        lax.fori_loop(0, nkb, body, 0)

    @pl.when(tie)
    def _():
        tri = jnp.where(lax.broadcasted_iota(I32, (kb, kb), 0) > lax.broadcasted_iota(I32, (kb, kb), 1),
                        1.0, 0.0).astype(BF16)
        need_f = need.astype(F32)

        def body(kbi, seen):
            off = key_block(kbi)
            key = key_scr[pl.ds(off, kb), :]
            eq = jnp.where(key == thr, 1.0, 0.0)
            earlier = _dot(tri, eq.astype(BF16)) + seen
            sel = ((key > thr) | ((key == thr) & (earlier < need_f))) & (key_index(off) < lim_row)
            bias_scr[pl.ds(off, kb), :] = jnp.where(sel, 0.0, -MASK_BIAS)
            return seen + jnp.sum(eq, axis=0, keepdims=True)
        lax.fori_loop(0, nkb, body, jnp.zeros((1, tq), F32))

    unit_heads = qt_scr.shape[2] // tq
    n_units = n_heads // unit_heads
    kv_lanes = lambda u: slice((u * unit_heads // group) * LANES, (u * unit_heads // group + 1) * LANES)
    for h in range(n_heads):
        q2 = q_ref[0, :, (h // 2) * LANES:(h // 2 + 1) * LANES].astype(F32)
        if h % 2:
            q2 = pltpu.roll(q2, HEAD_DIM, 1)
        sl = jnp.zeros((1, LANES), F32)
        for i, piece in enumerate(_bf16_pieces(slopes[h] * LOG2E) * 2):
            sl = jnp.where(lane[:1] == POS_LANE0 + i, piece, sl)
        j = h % unit_heads
        qt_scr[h // unit_heads, :, j * tq:(j + 1) * tq] = jnp.where(lo_half, q2, sl).T.astype(BF16)
    m_scr[...] = jnp.full(m_scr.shape, -jnp.inf, F32)
    acc_scr[...] = jnp.zeros(acc_scr.shape, F32)

    def scores(kbi, u, last):
        off = key_block(kbi)
        bias = jnp.concatenate([bias_scr[pl.ds(off, kb), :]] * unit_heads, axis=1)
        s = _dot(ka_ref[0, pl.ds(off, kb), kv_lanes(u)], qt_scr[u]) + bias
        if last:
            ahead = jnp.maximum(key_index(off) - qpos_row, 0).astype(F32)
            s = s - jnp.concatenate([(2.0 * slopes[u * unit_heads + j] * LOG2E) * ahead for j in range(unit_heads)], axis=1)
        s_scr[u] = s

    def absorb(kbi, u):
        s = s_scr[u]
        m_old = m_scr[u]
        m_new = jnp.maximum(m_old, jnp.max(s, axis=0, keepdims=True))
        p = jnp.exp2((s - m_new).astype(BF16))
        acc_scr[u] = acc_scr[u] * jnp.exp2(m_old - m_new) + _dot(vt_ref[0, kbi, kv_lanes(u), :], p)
        m_scr[u] = m_new

    def step(kbi, last):
        for u in range(n_units):
            absorb(kbi - 1, u)
            scores(kbi, u, last)

    def first_scores(last):
        for u in range(n_units):
            scores(0, u, last)

    def step_body(kbi, carry):
        step(kbi, False)
        return carry

    pl.when(nkb == 1)(lambda: first_scores(True))
    pl.when(nkb > 1)(lambda: first_scores(False))
    lax.fori_loop(1, nkb - 1, step_body, 0)
    pl.when(nkb > 1)(lambda: step(nkb - 1, True))
    for u in range(n_units):
        absorb(nkb - 1, u)

    for u in range(n_units):
        acc = acc_scr[u]
        o_t = acc[:HEAD_DIM] / acc[HEAD_DIM:HEAD_DIM + 1]
        for j in range(unit_heads):
            h = u * unit_heads + j
            ot_scr[h * HEAD_DIM:(h + 1) * HEAD_DIM, :] = o_t[:, j * tq:(j + 1) * tq].astype(BF16)
    y = lax.dot_general(ot_scr[...], wo_ref[...], (((0,), (0,)), ((), ())), preferred_element_type=F32)
    o_ref[0] = _layer_norm(alpha * x_ref[0] + y, g_ref[...], b_ref[...])


def _sparse_attn(x, q, qi, wit, k, vt, kd, wo, g, b, *, tq, kb, pos0, n_keys, alpha):
    bsz, s, d = x.shape
    n_t = s // tq
    l_pad = k.shape[1]
    n_heads = q.shape[2] // HEAD_DIM
    n_kv = k.shape[2] // LANES
    group = n_heads // n_kv
    assert kb % tq == 0 and pos0 % tq == 0 and l_pad % kb == 0 and vt.shape == (bsz, l_pad // kb, k.shape[2], kb)
    slopes = tuple(2.0 ** (-8.0 * (h + 1) / n_heads) for h in range(n_heads))
    qrow = lambda c: pl.BlockSpec((1, tq, c), lambda bi, t: (bi, t, 0))
    key_mode = pl.Buffered(1) if n_t > 1 else pl.Buffered(2)
    keys = lambda c: pl.BlockSpec((1, l_pad, c), lambda bi, t: (bi, 0, 0), pipeline_mode=key_mode)
    const2 = lambda bi, t: (0, 0)
    sel_rows = -(-l_pad // (2 * kb)) * 2 * kb
    unit_heads = min(max(UNIT_LANES // tq, 1), group)
    assert group % unit_heads == 0
    n_units, unit_w = n_heads // unit_heads, unit_heads * tq
    kern = functools.partial(_sparse_kernel, tq=tq, kb=kb, pos0=pos0, n_keys=n_keys, n_sel=min(TOPK_MAX, n_keys // 4),
                             n_heads=n_heads, group=group, slopes=slopes, alpha=alpha)
    return pl.pallas_call(
        kern,
        grid=(bsz, n_t),
        in_specs=[qrow(d), qrow(q.shape[2]), qrow(qi.shape[2]),
                  pl.BlockSpec((IDX_HEADS, tq), lambda bi, t: (0, bi * n_t + t)),
                  keys(k.shape[2]),
                  pl.BlockSpec((1,) + vt.shape[1:], lambda bi, t: (bi, 0, 0, 0), pipeline_mode=key_mode),
                  keys(kd.shape[2]),
                  _resident(wo.shape), pl.BlockSpec((1, d), const2), pl.BlockSpec((1, d), const2)],
        out_specs=qrow(d),
        out_shape=jax.ShapeDtypeStruct((bsz, s, d), F32),
        scratch_shapes=[pltpu.VMEM((sel_rows, tq), I32), pltpu.VMEM((sel_rows, tq), I16), pltpu.VMEM((sel_rows, tq), I16),
                        pltpu.VMEM((l_pad, tq), F32),
                        pltpu.VMEM((n_units, LANES, unit_w), BF16), pltpu.VMEM((n_units, kb, unit_w), F32),
                        pltpu.VMEM((n_units, LANES, unit_w), F32),
                        pltpu.VMEM((n_units, 1, unit_w), F32), pltpu.VMEM((q.shape[2], tq), BF16)],
        compiler_params=_params(2),
        name="sparse_attn",
    )(x, q, qi, wit, k, vt, kd, wo, g, b)


def _pad_rows(a, n):
    return jnp.pad(a, ((0, 0), (0, n - a.shape[1]), (0, 0)))


def _mixer_a(xp, xs, cache_k, cache_v, w_in, rel_bias, w_out, g, b, alpha):
    bsz, s, d = xp.shape
    dbs, t, _ = xs.shape
    width = w_out.shape[0]
    w_in = w_in.astype(BF16)
    w_out = w_out.astype(BF16)
    bias_p, bias_s = _rel_bias_tables(rel_bias, t)
    keep = min(A_PAST, s)
    q, k, vt, kf, vf = _proj_a(xp, w_in, keep)
    yp = _attn_a_prompt(xp, q, k, vt, bias_p, w_out, g, b, alpha)
    qs, ks, _, ksf, vsf = _proj_a(xs.reshape(1, dbs * t, d), w_in, dbs * t)
    n_cache = cache_k.shape[1]
    kwin = _pad_rows(jnp.concatenate([cache_k.reshape(dbs, n_cache, width).astype(BF16), ks.reshape(dbs, t, width)], 1), A_WIN)
    vwin = _pad_rows(jnp.concatenate([cache_v.reshape(dbs, n_cache, width), vsf.reshape(dbs, t, width)], 1).astype(BF16), A_WIN)
    ys = _attn_a_sample(_pad_rows(xs, A_SUB), _pad_rows(qs.reshape(dbs, t, width), A_SUB), kwin, vwin.transpose(0, 2, 1),
                        bias_s, w_out, g, b, alpha)[:, :t]
    heads = width // HEAD_DIM
    return (yp, ys, kf.reshape(bsz, keep, heads, HEAD_DIM), vf.reshape(bsz, keep, heads, HEAD_DIM),
            ksf.reshape(dbs, t, heads, HEAD_DIM), vsf.reshape(dbs, t, heads, HEAD_DIM))


def _mixer_b(xp, xs, cache_k, cache_v, cache_idx, w_in, w_out, g, b, alpha):
    bsz, s, d = xp.shape
    dbs, t, _ = xs.shape
    nq = w_out.shape[0]
    nkv = cache_k.shape[2] * cache_k.shape[3]
    nqi = IDX_HEADS * HEAD_DIM
    kvh = nkv // HEAD_DIM
    past = cache_k.shape[1]
    w_pad = jnp.pad(w_in, ((0, 0), (0, nq + 2 * nkv + nqi + LANES - w_in.shape[1]))).astype(BF16)
    w_out = w_out.astype(BF16)
    kb = PROJ_ROWS

    q, qi, k, kf, vf, kif, kd, wit, vt = _proj_b(xp.reshape(bsz * s, d), w_pad, nq, nkv, nqi, kb, 0, s)
    r3 = lambda a: a.reshape(bsz, s, a.shape[1])
    yp = _sparse_attn(xp, r3(q), r3(qi), wit, r3(k), vt.reshape(bsz, s // kb, 2 * nkv, kb), r3(kd), w_out, g, b,
                      tq=B_QUERY_TILE, kb=kb, pos0=0, n_keys=s, alpha=alpha)

    qs, qis, ks, ksf, vsf, kisf, kds, wits, _ = _proj_b(xs.reshape(dbs * t, d), w_pad, nq, nkv, nqi, dbs * t, past, t)
    tq_s = B_SAMPLE_TILE
    n_keys = past + t
    l_pad = -(-n_keys // kb) * kb
    s3 = lambda a: a.reshape(dbs, t, a.shape[1])
    v_all = jnp.concatenate([cache_v, vsf.reshape(dbs, t, kvh, HEAD_DIM)], axis=1).astype(BF16)
    v_all = jnp.concatenate([v_all, jnp.ones_like(v_all)], axis=-1).reshape(dbs, n_keys, 2 * nkv)
    keys_vt = _pad_rows(v_all, l_pad).reshape(dbs, l_pad // kb, kb, 2 * nkv).transpose(0, 1, 3, 2)
    kd_cache = jnp.concatenate([cache_idx, cache_idx], axis=-1).astype(BF16)
    frame = jnp.arange(past, dtype=I32)[:, None]
    pos = _pos_lanes(frame, HEAD_DIM + jnp.arange(HEAD_DIM, dtype=I32)[None, :]).astype(BF16)
    pos = jnp.broadcast_to(pos[None, :, None, :], (dbs, past, kvh, HEAD_DIM))
    k_cache = jnp.concatenate([cache_k.astype(BF16), pos], axis=-1).reshape(dbs, past, 2 * nkv)
    keys_k = _pad_rows(jnp.concatenate([k_cache, s3(ks)], 1), l_pad)
    keys_kd = _pad_rows(jnp.concatenate([kd_cache, s3(kds)], 1), l_pad)
    wit_pad = jnp.pad(wits.reshape(IDX_HEADS, dbs, t), ((0, 0), (0, 0), (0, tq_s - t)), mode="edge").reshape(IDX_HEADS, dbs * tq_s)
    qis_pad = jnp.pad(s3(qis), ((0, 0), (0, tq_s - t), (0, 0)), mode="edge")
    ys = _sparse_attn(_pad_rows(xs, tq_s), _pad_rows(s3(qs), tq_s), qis_pad, wit_pad,
                      keys_k, keys_vt, keys_kd, w_out, g, b,
                      tq=tq_s, kb=kb, pos0=past, n_keys=n_keys, alpha=alpha)[:, :t]
    return (yp, ys,
            kf.reshape(bsz, s, kvh, HEAD_DIM), vf.reshape(bsz, s, kvh, HEAD_DIM), kif.reshape(bsz, s, HEAD_DIM),
            ksf.reshape(dbs, t, kvh, HEAD_DIM), vsf.reshape(dbs, t, kvh, HEAD_DIM), kisf.reshape(dbs, t, HEAD_DIM))


def kernel(x_prompt, x_sample, cache_a_k, cache_a_v, cache_b_k, cache_b_v, cache_b_idx, ln_g, ln_b, ffn_w_gate, ffn_w_up, ffn_w_down, a_w_in, a_rel_bias, a_w_out, b_w_in, b_w_out):
    depth = ln_g.shape[0]
    alpha = (2.0 * depth) ** 0.25
    bsz, s, d = x_prompt.shape
    dbs, t, _ = x_sample.shape
    xp, xs = x_prompt, x_sample
    a_out, b_out = [], []

    wg_all, wu_all, wd_all = ffn_w_gate.astype(BF16), ffn_w_up.astype(BF16), ffn_w_down.astype(BF16)

    def ffn(x, layer, i, tm):
        shp = x.shape
        y = _ffn_block(x.reshape(-1, d), wg_all[layer, i], wu_all[layer, i], wd_all[layer, i],
                       ln_g[layer, 2 * i][None], ln_b[layer, 2 * i][None], alpha, tm)
        return y.reshape(shp)

    for layer in range(depth):
        j = layer // 2
        xp = ffn(xp, layer, 0, FFN_ROWS)
        xs = ffn(xs, layer, 0, dbs * t)
        g, b = ln_g[layer, 1][None], ln_b[layer, 1][None]
        if layer % 2 == 0:
            xp, xs, *rest = _mixer_a(xp, xs, cache_a_k[j], cache_a_v[j], a_w_in[j], a_rel_bias[j], a_w_out[j], g, b, alpha)
            a_out.append(rest)
        else:
            xp, xs, *rest = _mixer_b(xp, xs, cache_b_k[j], cache_b_v[j], cache_b_idx[j], b_w_in[j], b_w_out[j], g, b, alpha)
            b_out.append(rest)
        xp = ffn(xp, layer, 1, FFN_ROWS)
        xs = ffn(xs, layer, 1, dbs * t)

    stack = lambda outs, i: jnp.stack([o[i] for o in outs], 0)
    return (xp, xs,
            stack(a_out, 0), stack(a_out, 1), stack(a_out, 2), stack(a_out, 3),
            stack(b_out, 0), stack(b_out, 1), stack(b_out, 2), stack(b_out, 3), stack(b_out, 4), stack(b_out, 5))
```

```python
import functools
import math
import struct

import jax
import jax.numpy as jnp
from jax import lax
from jax.experimental import pallas as pl
from jax.experimental.pallas import tpu as pltpu

F32 = jnp.float32
BF16 = jnp.bfloat16
I32 = jnp.int32
I16 = jnp.int16

CHUNK = 64
CHUNK_BITS = 6
HALF_BITS = 16
FFN_ROWS = 1024
PROJ_ROWS = 512
B_QUERY_TILE = 256
B_SAMPLE_TILE = 128
A_PAST = 512
REL_CLIP = 128
HEAD_DIM = 64
LANES = 128
IDX_HEADS = 8
TOPK_MAX = 256
LN_EPS = 1e-5
A_SUB = 4 * CHUNK
A_WIN = A_PAST + A_SUB
VMEM_LIMIT = 52 * 1024 * 1024

LOG2E = 1.4426950408889634
POS_LANE0 = HEAD_DIM
MASK_BIAS = 1e30
ONES_ROWS = 16
FFN_CHUNK = 768
UNIT_LANES = 512
COUNT_ROWS = 16

assert 1 << CHUNK_BITS == CHUNK

NT_DIMS = (((1,), (1,)), ((), ()))
INT_MIN = -2147483648
I16_MIN = -32768
COUNT16_ROWS = 16
COUNT16_CHAINS = 4
NEG_INF_KEY = -2139095041


def _params(n_grid, flags=None):
    return pltpu.CompilerParams(dimension_semantics=("arbitrary",) * n_grid,
                                vmem_limit_bytes=VMEM_LIMIT, flags=flags)


def _resident(shape):
    zeros = (0,) * len(shape)
    return pl.BlockSpec(shape, lambda *_: zeros, pipeline_mode=pl.Buffered(1))


def _layer_norm(z, g, b):
    mu = jnp.mean(z, axis=-1, keepdims=True)
    d = z - mu
    var = jnp.mean(d * d, axis=-1, keepdims=True)
    return d * lax.rsqrt(var + LN_EPS) * g + b


def _dot(a, b):
    return jnp.dot(a, b, preferred_element_type=F32)


def _dot_nt(a, b):
    return lax.dot_general(a, b, NT_DIMS, preferred_element_type=F32)


def _ffn_kernel(x_ref, wg_ref, wu_ref, wd_ref, g_ref, b_ref, o_ref, h_scr, *, alpha):
    x = x_ref[...]
    xb = x.astype(BF16)
    f = wg_ref.shape[1]
    for c0 in range(0, f, FFN_CHUNK):
        sl = slice(c0, min(c0 + FFN_CHUNK, f))
        gate = _dot(xb, wg_ref[:, sl])
        up = _dot(xb, wu_ref[:, sl])
        h_scr[:, sl] = (gate * (1.0 / (1.0 + jnp.exp(-gate))) * up).astype(BF16)
    y = _dot(h_scr[...], wd_ref[...])
    o_ref[...] = _layer_norm(alpha * x + 0.5 * y, g_ref[...], b_ref[...])


def _ffn_block(x, wg, wu, wd, g, b, alpha, tm):
    m, d = x.shape
    f = wg.shape[1]
    const = lambda i: (0, 0)
    return pl.pallas_call(
        functools.partial(_ffn_kernel, alpha=alpha),
        grid=(m // tm,),
        in_specs=[pl.BlockSpec((tm, d), lambda i: (i, 0)),
                  _resident((d, f)), _resident((d, f)), _resident((f, d)),
                  pl.BlockSpec((1, d), const), pl.BlockSpec((1, d), const)],
        out_specs=pl.BlockSpec((tm, d), lambda i: (i, 0)),
        out_shape=jax.ShapeDtypeStruct((m, d), F32),
        scratch_shapes=[pltpu.VMEM((tm, f), BF16)],
        compiler_params=_params(1),
        name="ffn_ln",
    )(x, wg, wu, wd, g, b)


def _proj_a_kernel(x_ref, w_ref, q_ref, k_ref, vt_ref, kf_ref, vf_ref, *, width):
    h = _dot(x_ref[0].astype(BF16), w_ref[...])
    k = h[:, width:2 * width]
    v = h[:, 2 * width:]
    q_ref[0] = (h[:, :width] * (HEAD_DIM ** -0.5 * LOG2E)).astype(BF16)
    k_ref[0] = k.astype(BF16)
    vt_ref[0] = v.T.astype(BF16)

    @pl.when(pl.program_id(1) == pl.num_programs(1) - 1)
    def _():
        kf_ref[0] = k
        vf_ref[0] = v


def _proj_a(x, w, tm):
    bsz, s, d = x.shape
    width = w.shape[1] // 3
    row = pl.BlockSpec((1, tm, width), lambda b, t: (b, t, 0))
    last = pl.BlockSpec((1, tm, width), lambda b, t: (b, 0, 0))
    return pl.pallas_call(
        functools.partial(_proj_a_kernel, width=width),
        grid=(bsz, s // tm),
        in_specs=[pl.BlockSpec((1, tm, d), lambda b, t: (b, t, 0)), _resident(w.shape)],
        out_specs=[row, row, pl.BlockSpec((1, width, tm), lambda b, t: (b, 0, t)), last, last],
        out_shape=[jax.ShapeDtypeStruct((bsz, s, width), BF16)] * 2 + [jax.ShapeDtypeStruct((bsz, width, s), BF16)]
        + [jax.ShapeDtypeStruct((bsz, tm, width), F32)] * 2,
        compiler_params=_params(2),
        name="proj_a",
    )(x, w)


def _attn_a_kernel(*refs, n_sub, prompt, alpha):
    if prompt:
        (x_ref, q_ref, kp_ref, kc_ref, vtp_ref, vtc_ref, bias_ref, wo_ref, g_ref, b_ref,
         o_ref, kwin, q_scr, s_scr, ot_scr, vtwin) = refs
        vtwin[:, 0:A_PAST] = vtp_ref[0]
        vtwin[:, A_PAST:2 * A_PAST] = vtc_ref[0]
        vt_at = lambda rows, r0: vtwin[rows, r0:r0 + A_WIN]
    else:
        x_ref, q_ref, k_ref, vt_ref, bias_ref, wo_ref, g_ref, b_ref, o_ref, kwin, q_scr, s_scr, ot_scr = refs
        vt_at = lambda rows, r0: vt_ref[0, rows, r0:r0 + A_WIN]
    n_pairs = q_ref.shape[2] // LANES
    for p in range(n_pairs):
        cols = slice(p * LANES, (p + 1) * LANES)
        q_scr[p] = q_ref[0, :, cols]
        if prompt:
            kwin[p, 0:A_PAST] = kp_ref[0, :, cols]
            kwin[p, A_PAST:2 * A_PAST] = kc_ref[0, :, cols]
        else:
            kwin[p] = k_ref[0, :, cols]
    lane = lax.broadcasted_iota(I32, (A_SUB, LANES), 1)
    key_row = lax.broadcasted_iota(I32, (A_WIN, 2 * A_SUB), 0)
    out_row = lax.broadcasted_iota(I32, (LANES, A_SUB), 0)
    ones_rows = jnp.ones((ONES_ROWS, A_WIN), BF16)

    def scores(p, j, side, first_valid):
        r0 = j * A_SUB
        q2 = q_scr[p, r0:r0 + A_SUB, :].astype(F32)
        qt = jnp.concatenate([jnp.where(lane < HEAD_DIM, q2, 0.0).T, jnp.where(lane < HEAD_DIM, 0.0, q2).T], axis=1)
        s = _dot(kwin[p, r0:r0 + A_WIN, :], qt.astype(BF16)) + bias_ref[p]
        if first_valid > r0:
            s = jnp.where(key_row >= first_valid - r0, s, -jnp.inf)
        s_scr[side, j] = s

    def absorb(p, j, side):
        r0 = j * A_SUB
        s = s_scr[side, j]
        e = jnp.exp2((s - jnp.max(s, axis=0, keepdims=True)).astype(BF16))
        rows = pl.ds(p * LANES, LANES)
        o = _dot(jnp.concatenate([vt_at(rows, r0), ones_rows], axis=0), e)
        o = o[:LANES] / o[LANES:LANES + 1]
        ot_scr[rows, r0:r0 + A_SUB] = jnp.where(out_row < HEAD_DIM, o[:, :A_SUB], o[:, A_SUB:]).astype(BF16)

    def overlap(p_scores, p_absorb, side, first_valid):
        for j in range(n_sub):
            scores(p_scores, j, side, first_valid)
        for j in range(n_sub):
            absorb(p_absorb, j, 1 - side)

    def attend(first_valid):
        assert n_pairs % 2 == 0
        for j in range(n_sub):
            scores(0, j, 0, first_valid)

        def two_pairs(i, carry):
            overlap(2 * i + 1, 2 * i, 1, first_valid)
            overlap(2 * i + 2, 2 * i + 1, 0, first_valid)
            return carry

        for i in range(n_pairs // 2 - 1):
            two_pairs(i, 0)
        overlap(n_pairs - 1, n_pairs - 2, 1, first_valid)
        for j in range(n_sub):
            absorb(n_pairs - 1, j, 1)

    if prompt:
        pl.when(pl.program_id(1) == 0)(lambda: attend(A_PAST))
        pl.when(pl.program_id(1) > 0)(lambda: attend(0))
    else:
        attend(0)

    y = lax.dot_general(ot_scr[...], wo_ref[...], (((0,), (0,)), ((), ())), preferred_element_type=F32)
    o_ref[0] = _layer_norm(alpha * x_ref[0] + y, g_ref[...], b_ref[...])


def _attn_a_scratch(width, n_keys, n_queries):
    pairs = width // LANES
    return [pltpu.VMEM((pairs, n_keys, LANES), BF16), pltpu.VMEM((pairs, n_queries, LANES), BF16),
            pltpu.VMEM((2, n_queries // A_SUB, A_WIN, 2 * A_SUB), F32), pltpu.VMEM((width, n_queries), BF16)]


def _attn_a_prompt(x, q, k, vt, bias, wo, g, b, alpha):
    bsz, s, d = x.shape
    width = q.shape[2]
    tq = A_PAST
    cur = lambda bi, t: (bi, t, 0)
    prev = lambda bi, t: (bi, jnp.maximum(t - 1, 0), 0)
    cur_t = lambda bi, t: (bi, 0, t)
    prev_t = lambda bi, t: (bi, 0, jnp.maximum(t - 1, 0))
    const2 = lambda bi, t: (0, 0)
    return pl.pallas_call(
        functools.partial(_attn_a_kernel, n_sub=tq // A_SUB, prompt=True, alpha=alpha),
        grid=(bsz, s // tq),
        in_specs=[pl.BlockSpec((1, tq, d), cur), pl.BlockSpec((1, tq, width), cur),
                  pl.BlockSpec((1, tq, width), prev), pl.BlockSpec((1, tq, width), cur),
                  pl.BlockSpec((1, width, tq), prev_t), pl.BlockSpec((1, width, tq), cur_t),
                  _resident(bias.shape), _resident(wo.shape),
                  pl.BlockSpec((1, d), const2), pl.BlockSpec((1, d), const2)],
        out_specs=pl.BlockSpec((1, tq, d), cur),
        out_shape=jax.ShapeDtypeStruct((bsz, s, d), F32),
        scratch_shapes=_attn_a_scratch(width, 2 * tq, tq) + [pltpu.VMEM((width, 2 * tq), BF16)],
        compiler_params=_params(2),
        name="attn_a_prompt",
    )(x, q, k, k, vt, vt, bias, wo, g, b)


def _attn_a_sample(x, q, kwin, vtwin, bias, wo, g, b, alpha):
    bsz, _, d = x.shape
    width = q.shape[2]
    blk = lambda n, c: pl.BlockSpec((1, n, c), lambda bi: (bi, 0, 0))
    const2 = lambda bi: (0, 0)
    return pl.pallas_call(
        functools.partial(_attn_a_kernel, n_sub=1, prompt=False, alpha=alpha),
        grid=(bsz,),
        in_specs=[blk(A_SUB, d), blk(A_SUB, width), blk(A_WIN, width), blk(width, A_WIN),
                  _resident(bias.shape), _resident(wo.shape),
                  pl.BlockSpec((1, d), const2), pl.BlockSpec((1, d), const2)],
        out_specs=blk(A_SUB, d),
        out_shape=jax.ShapeDtypeStruct((bsz, A_SUB, d), F32),
        scratch_shapes=_attn_a_scratch(width, A_WIN, A_SUB),
        compiler_params=_params(1),
        name="attn_a_sample",
    )(x, q, kwin, vtwin, bias, wo, g, b)


def _rel_bias_tables(rel_bias, n_sample):
    r = jnp.arange(A_SUB)[:, None]
    c = jnp.arange(A_WIN)[None, :]
    period = A_SUB + A_WIN
    diff = jnp.arange(period)
    diff = jnp.where(diff < A_WIN, diff, diff - period)
    line = rel_bias[:, jnp.clip(A_PAST - diff, -REL_CLIP, REL_CLIP) + REL_CLIP].astype(F32) * LOG2E
    table = jnp.tile(line, (1, A_SUB))[:, :A_SUB * (period - 1)].reshape(-1, A_SUB, period - 1)[:, :, :A_WIN]
    lo = (r // CHUNK) * CHUNK
    band = (c >= lo) & (c < lo + A_PAST + CHUNK)
    prompt = jnp.where(band[None], table, -jnp.inf)
    live = (c < A_PAST + n_sample)
    sample = jnp.where(live[None], jnp.where((r < n_sample)[None], table, 0.0), -jnp.inf)

    def pair_layout(tab):
        h = tab.shape[0]
        return tab.transpose(0, 2, 1).reshape(h // 2, 2, A_WIN, A_SUB).transpose(0, 2, 1, 3).reshape(h // 2, A_WIN, 2 * A_SUB)

    return pair_layout(prompt), pair_layout(sample)


def _round_to_bf16(x):
    bits = struct.unpack("<I", struct.pack("<f", x))[0]
    bits = (bits + 0x7FFF + ((bits >> 16) & 1)) & 0xFFFF0000
    return struct.unpack("<f", struct.pack("<I", bits))[0]


def _bf16_pieces(x):
    p1 = _round_to_bf16(x)
    p2 = _round_to_bf16(x - p1)
    return (p1, p2, _round_to_bf16(x - p1 - p2))


def _pos_lanes(pos, lane):
    hi = (lax.shift_right_logical(pos, CHUNK_BITS) * CHUNK).astype(F32)
    lo = (pos & (CHUNK - 1)).astype(F32)
    return jnp.where(lane < POS_LANE0 + 3, hi, jnp.where(lane < POS_LANE0 + 6, lo, 0.0))


def _alibi_slopes(n_heads):
    return tuple(2.0 ** (-8.0 * (h + 1) / n_heads) for h in range(n_heads))


def _proj_b_kernel(x_ref, w_ref, qt_ref, qi_ref, ka_ref, kf_ref, vf_ref, ki_ref, kd_ref, wit_ref, vt_ref,
                   *, nq, nkv, nqi, pos0, period):
    h = _dot(x_ref[...].astype(BF16), w_ref[...])
    n_heads = nq // HEAD_DIM
    slopes = _alibi_slopes(n_heads)
    lane_q = lax.broadcasted_iota(I32, (x_ref.shape[0], LANES), 1)
    for hd in range(n_heads):
        q2 = h[:, (hd // 2) * LANES:(hd // 2 + 1) * LANES] * (HEAD_DIM ** -0.5 * LOG2E)
        if hd % 2:
            q2 = pltpu.roll(q2, HEAD_DIM, 1)
        sl = jnp.zeros((1, LANES), F32)
        for i, piece in enumerate(_bf16_pieces(slopes[hd] * LOG2E) * 2):
            sl = jnp.where(lane_q[:1] == POS_LANE0 + i, piece, sl)
        qt_ref[hd] = jnp.where(lane_q < HEAD_DIM, q2, sl).T.astype(BF16)
    k = h[:, nq:nq + nkv]
    v = h[:, nq + nkv:nq + 2 * nkv]
    kf_ref[...] = k
    vf_ref[...] = v
    o0 = nq + 2 * nkv
    qi_ref[...] = (h[:, o0:o0 + nqi] * (HEAD_DIM ** -0.5)).astype(BF16)
    tail = h[:, o0 + nqi:o0 + nqi + LANES]
    ki_ref[...] = tail[:, :HEAD_DIM]
    lane = lax.broadcasted_iota(I32, tail.shape, 1)
    kd_ref[...] = jnp.where(lane < HEAD_DIM, tail, pltpu.roll(tail, HEAD_DIM, 1)).astype(BF16)
    wit_ref[...] = tail.T[HEAD_DIM:HEAD_DIM + IDX_HEADS, :] * (IDX_HEADS ** -0.5)
    tm = tail.shape[0]
    row = pl.program_id(0) * tm + lax.broadcasted_iota(I32, (tm, 1), 0)
    pos = _pos_lanes(pos0 + (row & (period - 1)), lane)
    for pair in range(nkv // LANES):
        for half in range(2):
            tile = slice((2 * pair + half) * LANES, (2 * pair + half + 1) * LANES)
            vp = v[:, pair * LANES:(pair + 1) * LANES]
            kp = k[:, pair * LANES:(pair + 1) * LANES]
            if half:
                vp, kp = pltpu.roll(vp, HEAD_DIM, 1), pltpu.roll(kp, HEAD_DIM, 1)
            vt_ref[0, tile, :] = jnp.where(lane < HEAD_DIM, vp, 1.0).T.astype(BF16)
            ka_ref[:, tile] = jnp.where(lane < HEAD_DIM, kp, pos).astype(BF16)


def _proj_b(x, w_pad, nq, nkv, nqi, tm, pos0, period):
    m, d = x.shape
    assert period & (period - 1) == 0
    rows = lambda c: pl.BlockSpec((tm, c), lambda i: (i, 0))
    outs = [(nqi, BF16), (2 * nkv, BF16), (nkv, F32), (nkv, F32), (HEAD_DIM, F32), (LANES, BF16)]
    n_heads = nq // HEAD_DIM
    return pl.pallas_call(
        functools.partial(_proj_b_kernel, nq=nq, nkv=nkv, nqi=nqi, pos0=pos0, period=period),
        grid=(m // tm,),
        in_specs=[rows(d), _resident(w_pad.shape)],
        out_specs=[pl.BlockSpec((n_heads, LANES, tm), lambda i: (0, 0, i))] + [rows(c) for c, _ in outs]
        + [pl.BlockSpec((IDX_HEADS, tm), lambda i: (0, i)), pl.BlockSpec((1, 2 * nkv, tm), lambda i: (i, 0, 0))],
        out_shape=[jax.ShapeDtypeStruct((n_heads, LANES, m), BF16)] + [jax.ShapeDtypeStruct((m, c), dt) for c, dt in outs]
        + [jax.ShapeDtypeStruct((IDX_HEADS, m), F32), jax.ShapeDtypeStruct((m // tm, 2 * nkv, tm), BF16)],
        compiler_params=_params(1),
        name="proj_b",
    )(x, w_pad)


def _sparse_kernel(x_ref, qt_ref, qi_ref, wit_ref, ka_ref, vt_ref, kd_ref, wo_ref, g_ref, b_ref, o_ref,
                   key_scr, hi_scr, lo_scr, bias_scr, s_scr, acc_scr, m_scr, ot_scr,
                   *, tq, kb, pos0, n_keys, n_sel, n_heads, group, slopes, alpha):
    t = pl.program_id(1)
    q0 = pos0 + t * tq
    qpos_row = q0 + lax.broadcasted_iota(I32, (1, tq), 1)
    lim_row = jnp.minimum((lax.shift_right_logical(qpos_row, CHUNK_BITS) + 1) * CHUNK, n_keys)
    kmax = jnp.minimum((lax.shift_right_logical(q0 + tq - 1, CHUNK_BITS) + 1) * CHUNK, n_keys)
    nkb = lax.shift_right_logical(kmax + kb - 1, int(math.log2(kb)))
    lane = lax.broadcasted_iota(I32, (tq, LANES), 1)
    lo_half = lane < HEAD_DIM

    def key_block(kbi):
        return pl.multiple_of(kbi * kb, kb)

    def key_index(off):
        return off + lax.broadcasted_iota(I32, (kb, tq), 0)

    qis = []
    for p in range(IDX_HEADS // 2):
        q2 = qi_ref[0, :, p * LANES:(p + 1) * LANES]
        qis.append(jnp.where(lo_half, q2, jnp.zeros_like(q2)))
        qis.append(jnp.where(lo_half, jnp.zeros_like(q2), q2))
    wis = wit_ref[...]
    qi_all = jnp.concatenate(qis, axis=0)

    def score_block(kbi, carry):
        off = key_block(kbi)
        dots = _dot_nt(kd_ref[0, pl.ds(off, kb), :], qi_all)
        acc = jnp.zeros((kb, tq), F32)
        for h in range(IDX_HEADS):
            acc = acc + jnp.maximum(dots[:, h * tq:(h + 1) * tq], 0.0) * wis[h:h + 1, :]
        acc = jnp.where(key_index(off) < lim_row, acc, -jnp.inf)
        bits = lax.bitcast_convert_type(acc, I32)
        key = bits ^ (lax.shift_right_arithmetic(bits, 31) & 0x7FFFFFFF)
        key_scr[pl.ds(off, kb), :] = key
        hi_scr[pl.ds(off, kb), :] = lax.shift_right_arithmetic(key, HALF_BITS).astype(I16)
        lo_scr[pl.ds(off, kb), :] = ((key & ((1 << HALF_BITS) - 1)) + I16_MIN).astype(I16)
        return carry

    lax.fori_loop(0, nkb, score_block, 0)

    @pl.when(nkb % 2 == 1)
    def _():
        key_scr[pl.ds(key_block(nkb), kb), :] = jnp.full((kb, tq), INT_MIN, I32)
        hi_scr[pl.ds(key_block(nkb), kb), :] = jnp.full((kb, tq), I16_MIN, I16)
        lo_scr[pl.ds(key_block(nkb), kb), :] = jnp.full((kb, tq), I16_MIN, I16)

    n_steps = lax.shift_right_logical(nkb + 1, 1)

    def step_rows(i):
        return pl.ds(pl.multiple_of(i * 2 * kb, 2 * kb), 2 * kb)

    def count(pred):
        def body(i, part):
            hit = pred(key_scr[step_rows(i), :])
            return part + jnp.sum(jnp.where(hit, 1, 0).reshape(-1, COUNT_ROWS, tq), axis=0)
        part = lax.fori_loop(0, n_steps, body, jnp.zeros((COUNT_ROWS, tq), I32))
        return jnp.sum(part, axis=0, keepdims=True)

    def count16(ref, pred):
        def body(i, parts):
            ind = jnp.where(pred(ref[pl.ds(key_block(i), kb), :]), jnp.int16(1), jnp.int16(0))
            parts = list(parts)
            for j, r in enumerate(range(0, kb, COUNT16_ROWS)):
                parts[j % len(parts)] = parts[j % len(parts)] + ind[r:r + COUNT16_ROWS]
            return tuple(parts)
        parts = lax.fori_loop(0, 2 * n_steps, body, (jnp.zeros((COUNT16_ROWS, tq), I16),) * COUNT16_CHAINS)
        return jnp.sum(sum(p.astype(I32) for p in parts), axis=0, keepdims=True)

    def as_i16(row):
        return jnp.broadcast_to(row, (COUNT16_ROWS, tq)).astype(I16)[:1]

    def bisect16(ref, want):
        def body(i, thr):
            cand = thr + lax.shift_left(jnp.int32(1), HALF_BITS - 1 - i)
            cand16 = as_i16(cand)
            return jnp.where(count16(ref, lambda v: v >= cand16) >= want, cand, thr)
        return lax.fori_loop(0, HALF_BITS, body, jnp.full((1, tq), I16_MIN, I32))

    thr_hi = bisect16(hi_scr, n_sel)
    thr_hi16 = as_i16(thr_hi)
    n_hi_gt = count16(hi_scr, lambda v: v > thr_hi16)

    def keep_bucket(i, carry):
        rows = step_rows(i)
        lo_scr[rows, :] = jnp.where(hi_scr[rows, :] == thr_hi16, lo_scr[rows, :], jnp.int16(I16_MIN))
        return carry

    lax.fori_loop(0, n_steps, keep_bucket, 0)
    thr = thr_hi * (1 << HALF_BITS) + (bisect16(lo_scr, n_sel - n_hi_gt) - I16_MIN)
    n_gt = count(lambda key: key > thr)
    n_eq = count(lambda key: key == thr)
    need = n_sel - n_gt
    tie = jnp.max(jnp.where((n_eq > need) & (thr > NEG_INF_KEY), 1, 0)) > 0

    @pl.when(jnp.logical_not(tie))
    def _():
        def body(kbi, carry):
            off = key_block(kbi)
            sel = (key_scr[pl.ds(off, kb), :] >= thr) & (key_index(off) < lim_row)
            bias_scr[pl.ds(off, kb), :] = jnp.where(sel, 0.0, -MASK_BIAS)
            return carry
        lax.fori_loop(0, nkb, body, 0)

    @pl.when(tie)
    def _():
        tri = jnp.where(lax.broadcasted_iota(I32, (kb, kb), 0) > lax.broadcasted_iota(I32, (kb, kb), 1),
                        1.0, 0.0).astype(BF16)
        need_f = need.astype(F32)

        def body(kbi, seen):
            off = key_block(kbi)
            key = key_scr[pl.ds(off, kb), :]
            eq = jnp.where(key == thr, 1.0, 0.0)
            earlier = _dot(tri, eq.astype(BF16)) + seen
            sel = ((key > thr) | ((key == thr) & (earlier < need_f))) & (key_index(off) < lim_row)
            bias_scr[pl.ds(off, kb), :] = jnp.where(sel, 0.0, -MASK_BIAS)
            return seen + jnp.sum(eq, axis=0, keepdims=True)
        lax.fori_loop(0, nkb, body, jnp.zeros((1, tq), F32))

    unit_heads = s_scr.shape[2] // tq
    n_units = n_heads // unit_heads
    kv_lanes = lambda u: slice((u * unit_heads // group) * LANES, (u * unit_heads // group + 1) * LANES)
    m_scr[...] = jnp.full(m_scr.shape, -jnp.inf, F32)
    acc_scr[...] = jnp.zeros(acc_scr.shape, F32)

    def scores(kbi, u, last):
        off = key_block(kbi)
        bias = jnp.concatenate([bias_scr[pl.ds(off, kb), :]] * unit_heads, axis=1)
        q_t = jnp.concatenate([qt_ref[u * unit_heads + j] for j in range(unit_heads)], axis=1)
        s = _dot(ka_ref[0, pl.ds(off, kb), kv_lanes(u)], q_t) + bias
        if last:
            ahead = jnp.maximum(key_index(off) - qpos_row, 0).astype(F32)
            s = s - jnp.concatenate([(2.0 * slopes[u * unit_heads + j] * LOG2E) * ahead for j in range(unit_heads)], axis=1)
        s_scr[u] = s

    def absorb(kbi, u):
        s = s_scr[u]
        m_old = m_scr[u]
        m_new = jnp.maximum(m_old, jnp.max(s, axis=0, keepdims=True))
        p = jnp.exp2((s - m_new).astype(BF16))
        acc_scr[u] = acc_scr[u] * jnp.exp2(m_old - m_new) + _dot(vt_ref[0, kbi, kv_lanes(u), :], p)
        m_scr[u] = m_new

    def step(kbi, last):
        for u in range(n_units):
            absorb(kbi - 1, u)
            scores(kbi, u, last)

    def first_scores(last):
        for u in range(n_units):
            scores(0, u, last)

    def step_body(kbi, carry):
        step(kbi, False)
        return carry

    pl.when(nkb == 1)(lambda: first_scores(True))
    pl.when(nkb > 1)(lambda: first_scores(False))
    lax.fori_loop(1, nkb - 1, step_body, 0)
    pl.when(nkb > 1)(lambda: step(nkb - 1, True))
    for u in range(n_units):
        absorb(nkb - 1, u)

    for u in range(n_units):
        acc = acc_scr[u]
        o_t = acc[:HEAD_DIM] / acc[HEAD_DIM:HEAD_DIM + 1]
        for j in range(unit_heads):
            h = u * unit_heads + j
            ot_scr[h * HEAD_DIM:(h + 1) * HEAD_DIM, :] = o_t[:, j * tq:(j + 1) * tq].astype(BF16)
    y = lax.dot_general(ot_scr[...], wo_ref[...], (((0,), (0,)), ((), ())), preferred_element_type=F32)
    o_ref[0] = _layer_norm(alpha * x_ref[0] + y, g_ref[...], b_ref[...])


def _sparse_attn(x, qt, qi, wit, k, vt, kd, wo, g, b, *, tq, kb, pos0, n_keys, alpha):
    bsz, s, d = x.shape
    n_t = s // tq
    l_pad = k.shape[1]
    n_heads = qt.shape[0]
    n_kv = k.shape[2] // LANES
    group = n_heads // n_kv
    assert kb % tq == 0 and pos0 % tq == 0 and l_pad % kb == 0 and vt.shape == (bsz, l_pad // kb, k.shape[2], kb)
    slopes = _alibi_slopes(n_heads)
    qrow = lambda c: pl.BlockSpec((1, tq, c), lambda bi, t: (bi, t, 0))
    key_mode = pl.Buffered(1) if n_t > 1 else pl.Buffered(2)
    keys = lambda c: pl.BlockSpec((1, l_pad, c), lambda bi, t: (bi, 0, 0), pipeline_mode=key_mode)
    const2 = lambda bi, t: (0, 0)
    sel_rows = -(-l_pad // (2 * kb)) * 2 * kb
    unit_heads = min(max(UNIT_LANES // tq, 1), group)
    assert group % unit_heads == 0
    n_units, unit_w = n_heads // unit_heads, unit_heads * tq
    kern = functools.partial(_sparse_kernel, tq=tq, kb=kb, pos0=pos0, n_keys=n_keys, n_sel=min(TOPK_MAX, n_keys // 4),
                             n_heads=n_heads, group=group, slopes=slopes, alpha=alpha)
    return pl.pallas_call(
        kern,
        grid=(bsz, n_t),
        in_specs=[qrow(d), pl.BlockSpec((n_heads, LANES, tq), lambda bi, t: (0, 0, bi * n_t + t)), qrow(qi.shape[2]),
                  pl.BlockSpec((IDX_HEADS, tq), lambda bi, t: (0, bi * n_t + t)),
                  keys(k.shape[2]),
                  pl.BlockSpec((1,) + vt.shape[1:], lambda bi, t: (bi, 0, 0, 0), pipeline_mode=key_mode),
                  keys(kd.shape[2]),
                  _resident(wo.shape), pl.BlockSpec((1, d), const2), pl.BlockSpec((1, d), const2)],
        out_specs=qrow(d),
        out_shape=jax.ShapeDtypeStruct((bsz, s, d), F32),
        scratch_shapes=[pltpu.VMEM((sel_rows, tq), I32), pltpu.VMEM((sel_rows, tq), I16), pltpu.VMEM((sel_rows, tq), I16),
                        pltpu.VMEM((l_pad, tq), F32),
                        pltpu.VMEM((n_units, kb, unit_w), F32), pltpu.VMEM((n_units, LANES, unit_w), F32),
                        pltpu.VMEM((n_units, 1, unit_w), F32), pltpu.VMEM((n_heads * HEAD_DIM, tq), BF16)],
        compiler_params=_params(2),
        name="sparse_attn",
    )(x, qt, qi, wit, k, vt, kd, wo, g, b)


def _pad_rows(a, n):
    return jnp.pad(a, ((0, 0), (0, n - a.shape[1]), (0, 0)))


def _mixer_a(xp, xs, cache_k, cache_v, w_in, rel_bias, w_out, g, b, alpha):
    bsz, s, d = xp.shape
    dbs, t, _ = xs.shape
    width = w_out.shape[0]
    w_in = w_in.astype(BF16)
    w_out = w_out.astype(BF16)
    bias_p, bias_s = _rel_bias_tables(rel_bias, t)
    keep = min(A_PAST, s)
    q, k, vt, kf, vf = _proj_a(xp, w_in, keep)
    yp = _attn_a_prompt(xp, q, k, vt, bias_p, w_out, g, b, alpha)
    qs, ks, _, ksf, vsf = _proj_a(xs.reshape(1, dbs * t, d), w_in, dbs * t)
    n_cache = cache_k.shape[1]
    kwin = _pad_rows(jnp.concatenate([cache_k.reshape(dbs, n_cache, width).astype(BF16), ks.reshape(dbs, t, width)], 1), A_WIN)
    vwin = _pad_rows(jnp.concatenate([cache_v.reshape(dbs, n_cache, width), vsf.reshape(dbs, t, width)], 1).astype(BF16), A_WIN)
    ys = _attn_a_sample(_pad_rows(xs, A_SUB), _pad_rows(qs.reshape(dbs, t, width), A_SUB), kwin, vwin.transpose(0, 2, 1),
                        bias_s, w_out, g, b, alpha)[:, :t]
    heads = width // HEAD_DIM
    return (yp, ys, kf.reshape(bsz, keep, heads, HEAD_DIM), vf.reshape(bsz, keep, heads, HEAD_DIM),
            ksf.reshape(dbs, t, heads, HEAD_DIM), vsf.reshape(dbs, t, heads, HEAD_DIM))


def _mixer_b(xp, xs, cache_k, cache_v, cache_idx, w_in, w_out, g, b, alpha):
    bsz, s, d = xp.shape
    dbs, t, _ = xs.shape
    nq = w_out.shape[0]
    nkv = cache_k.shape[2] * cache_k.shape[3]
    nqi = IDX_HEADS * HEAD_DIM
    kvh = nkv // HEAD_DIM
    past = cache_k.shape[1]
    w_pad = jnp.pad(w_in, ((0, 0), (0, nq + 2 * nkv + nqi + LANES - w_in.shape[1]))).astype(BF16)
    w_out = w_out.astype(BF16)
    kb = PROJ_ROWS

    qt, qi, k, kf, vf, kif, kd, wit, vt = _proj_b(xp.reshape(bsz * s, d), w_pad, nq, nkv, nqi, kb, 0, s)
    r3 = lambda a: a.reshape(bsz, s, a.shape[1])
    yp = _sparse_attn(xp, qt, r3(qi), wit, r3(k), vt.reshape(bsz, s // kb, 2 * nkv, kb), r3(kd), w_out, g, b,
                      tq=B_QUERY_TILE, kb=kb, pos0=0, n_keys=s, alpha=alpha)

    qts, qis, ks, ksf, vsf, kisf, kds, wits, _ = _proj_b(xs.reshape(dbs * t, d), w_pad, nq, nkv, nqi, dbs * t, past, t)
    tq_s = B_SAMPLE_TILE
    n_keys = past + t
    l_pad = -(-n_keys // kb) * kb
    s3 = lambda a: a.reshape(dbs, t, a.shape[1])
    v_all = jnp.concatenate([cache_v, vsf.reshape(dbs, t, kvh, HEAD_DIM)], axis=1).astype(BF16)
    v_all = jnp.concatenate([v_all, jnp.ones_like(v_all)], axis=-1).reshape(dbs, n_keys, 2 * nkv)
    keys_vt = _pad_rows(v_all, l_pad).reshape(dbs, l_pad // kb, kb, 2 * nkv).transpose(0, 1, 3, 2)
    kd_cache = jnp.concatenate([cache_idx, cache_idx], axis=-1).astype(BF16)
    frame = jnp.arange(past, dtype=I32)[:, None]
    pos = _pos_lanes(frame, HEAD_DIM + jnp.arange(HEAD_DIM, dtype=I32)[None, :]).astype(BF16)
    pos = jnp.broadcast_to(pos[None, :, None, :], (dbs, past, kvh, HEAD_DIM))
    k_cache = jnp.concatenate([cache_k.astype(BF16), pos], axis=-1).reshape(dbs, past, 2 * nkv)
    keys_k = _pad_rows(jnp.concatenate([k_cache, s3(ks)], 1), l_pad)
    keys_kd = _pad_rows(jnp.concatenate([kd_cache, s3(kds)], 1), l_pad)
    wit_pad = jnp.pad(wits.reshape(IDX_HEADS, dbs, t), ((0, 0), (0, 0), (0, tq_s - t)), mode="edge").reshape(IDX_HEADS, dbs * tq_s)
    qis_pad = jnp.pad(s3(qis), ((0, 0), (0, tq_s - t), (0, 0)), mode="edge")
    qts_pad = jnp.pad(qts.reshape(-1, LANES, dbs, t), ((0, 0), (0, 0), (0, 0), (0, tq_s - t))).reshape(-1, LANES, dbs * tq_s)
    ys = _sparse_attn(_pad_rows(xs, tq_s), qts_pad, qis_pad, wit_pad,
                      keys_k, keys_vt, keys_kd, w_out, g, b,
                      tq=tq_s, kb=kb, pos0=past, n_keys=n_keys, alpha=alpha)[:, :t]
    return (yp, ys,
            kf.reshape(bsz, s, kvh, HEAD_DIM), vf.reshape(bsz, s, kvh, HEAD_DIM), kif.reshape(bsz, s, HEAD_DIM),
            ksf.reshape(dbs, t, kvh, HEAD_DIM), vsf.reshape(dbs, t, kvh, HEAD_DIM), kisf.reshape(dbs, t, HEAD_DIM))


def kernel(x_prompt, x_sample, cache_a_k, cache_a_v, cache_b_k, cache_b_v, cache_b_idx, ln_g, ln_b, ffn_w_gate, ffn_w_up, ffn_w_down, a_w_in, a_rel_bias, a_w_out, b_w_in, b_w_out):
    depth = ln_g.shape[0]
    alpha = (2.0 * depth) ** 0.25
    bsz, s, d = x_prompt.shape
    dbs, t, _ = x_sample.shape
    xp, xs = x_prompt, x_sample
    a_out, b_out = [], []

    wg_all, wu_all, wd_all = ffn_w_gate.astype(BF16), ffn_w_up.astype(BF16), ffn_w_down.astype(BF16)

    def ffn(x, layer, i, tm):
        shp = x.shape
        y = _ffn_block(x.reshape(-1, d), wg_all[layer, i], wu_all[layer, i], wd_all[layer, i],
                       ln_g[layer, 2 * i][None], ln_b[layer, 2 * i][None], alpha, tm)
        return y.reshape(shp)

    for layer in range(depth):
        j = layer // 2
        xp = ffn(xp, layer, 0, FFN_ROWS)
        xs = ffn(xs, layer, 0, dbs * t)
        g, b = ln_g[layer, 1][None], ln_b[layer, 1][None]
        if layer % 2 == 0:
            xp, xs, *rest = _mixer_a(xp, xs, cache_a_k[j], cache_a_v[j], a_w_in[j], a_rel_bias[j], a_w_out[j], g, b, alpha)
            a_out.append(rest)
        else:
            xp, xs, *rest = _mixer_b(xp, xs, cache_b_k[j], cache_b_v[j], cache_b_idx[j], b_w_in[j], b_w_out[j], g, b, alpha)
            b_out.append(rest)
        xp = ffn(xp, layer, 1, FFN_ROWS)
        xs = ffn(xs, layer, 1, dbs * t)

    stack = lambda outs, i: jnp.stack([o[i] for o in outs], 0)
    return (xp, xs,
            stack(a_out, 0), stack(a_out, 1), stack(a_out, 2), stack(a_out, 3),
            stack(b_out, 0), stack(b_out, 1), stack(b_out, 2), stack(b_out, 3), stack(b_out, 4), stack(b_out, 5))
```

```python
import functools
import math
import struct

import jax
import jax.numpy as jnp
from jax import lax
from jax.experimental import pallas as pl
from jax.experimental.pallas import tpu as pltpu

F32 = jnp.float32
BF16 = jnp.bfloat16
I32 = jnp.int32
I16 = jnp.int16

CHUNK = 64
CHUNK_BITS = 6
HALF_BITS = 16
FFN_ROWS = 1024
PROJ_ROWS = 512
B_QUERY_TILE = 256
B_SAMPLE_TILE = 128
A_PAST = 512
REL_CLIP = 128
HEAD_DIM = 64
LANES = 128
IDX_HEADS = 8
TOPK_MAX = 256
LN_EPS = 1e-5
A_SUB = 4 * CHUNK
A_WIN = A_PAST + A_SUB
VMEM_LIMIT = 52 * 1024 * 1024

LOG2E = 1.4426950408889634
POS_LANE0 = HEAD_DIM
MASK_BIAS = 1e30
ONES_ROWS = 16
FFN_CHUNK = 768
UNIT_LANES = 512
COUNT_ROWS = 16

assert 1 << CHUNK_BITS == CHUNK

NT_DIMS = (((1,), (1,)), ((), ()))
INT_MIN = -2147483648
I16_MIN = -32768
COUNT16_ROWS = 16
COUNT16_CHAINS = 4
NEG_INF_KEY = -2139095041


def _params(n_grid, flags=None):
    return pltpu.CompilerParams(dimension_semantics=("arbitrary",) * n_grid,
                                vmem_limit_bytes=VMEM_LIMIT, flags=flags)


def _resident(shape):
    zeros = (0,) * len(shape)
    return pl.BlockSpec(shape, lambda *_: zeros, pipeline_mode=pl.Buffered(1))


def _layer_norm(z, g, b):
    mu = jnp.mean(z, axis=-1, keepdims=True)
    d = z - mu
    var = jnp.mean(d * d, axis=-1, keepdims=True)
    return d * lax.rsqrt(var + LN_EPS) * g + b


def _dot(a, b):
    return jnp.dot(a, b, preferred_element_type=F32)


def _dot_nt(a, b):
    return lax.dot_general(a, b, NT_DIMS, preferred_element_type=F32)


def _ffn_kernel(x_ref, wg_ref, wu_ref, wd_ref, g_ref, b_ref, o_ref, h_scr, *, alpha):
    x = x_ref[...]
    xb = x.astype(BF16)
    f = wg_ref.shape[1]
    for c0 in range(0, f, FFN_CHUNK):
        sl = slice(c0, min(c0 + FFN_CHUNK, f))
        gate = _dot(xb, wg_ref[:, sl])
        up = _dot(xb, wu_ref[:, sl])
        h_scr[:, sl] = (gate * (1.0 / (1.0 + jnp.exp(-gate))) * up).astype(BF16)
    y = _dot(h_scr[...], wd_ref[...])
    o_ref[...] = _layer_norm(alpha * x + 0.5 * y, g_ref[...], b_ref[...])


def _ffn_block(x, wg, wu, wd, g, b, alpha, tm):
    m, d = x.shape
    f = wg.shape[1]
    const = lambda i: (0, 0)
    return pl.pallas_call(
        functools.partial(_ffn_kernel, alpha=alpha),
        grid=(m // tm,),
        in_specs=[pl.BlockSpec((tm, d), lambda i: (i, 0)),
                  _resident((d, f)), _resident((d, f)), _resident((f, d)),
                  pl.BlockSpec((1, d), const), pl.BlockSpec((1, d), const)],
        out_specs=pl.BlockSpec((tm, d), lambda i: (i, 0)),
        out_shape=jax.ShapeDtypeStruct((m, d), F32),
        scratch_shapes=[pltpu.VMEM((tm, f), BF16)],
        compiler_params=_params(1),
        name="ffn_ln",
    )(x, wg, wu, wd, g, b)


def _proj_a_kernel(x_ref, w_ref, q_ref, k_ref, vt_ref, kf_ref, vf_ref, *, width):
    h = _dot(x_ref[0].astype(BF16), w_ref[...])
    k = h[:, width:2 * width]
    v = h[:, 2 * width:]
    q_ref[0] = (h[:, :width] * (HEAD_DIM ** -0.5 * LOG2E)).astype(BF16)
    k_ref[0] = k.astype(BF16)
    vt_ref[0] = v.T.astype(BF16)

    @pl.when(pl.program_id(1) == pl.num_programs(1) - 1)
    def _():
        kf_ref[0] = k
        vf_ref[0] = v


def _proj_a(x, w, tm):
    bsz, s, d = x.shape
    width = w.shape[1] // 3
    row = pl.BlockSpec((1, tm, width), lambda b, t: (b, t, 0))
    last = pl.BlockSpec((1, tm, width), lambda b, t: (b, 0, 0))
    return pl.pallas_call(
        functools.partial(_proj_a_kernel, width=width),
        grid=(bsz, s // tm),
        in_specs=[pl.BlockSpec((1, tm, d), lambda b, t: (b, t, 0)), _resident(w.shape)],
        out_specs=[row, row, pl.BlockSpec((1, width, tm), lambda b, t: (b, 0, t)), last, last],
        out_shape=[jax.ShapeDtypeStruct((bsz, s, width), BF16)] * 2 + [jax.ShapeDtypeStruct((bsz, width, s), BF16)]
        + [jax.ShapeDtypeStruct((bsz, tm, width), F32)] * 2,
        compiler_params=_params(2),
        name="proj_a",
    )(x, w)


def _attn_a_kernel(*refs, n_sub, prompt, alpha):
    if prompt:
        (x_ref, q_ref, kp_ref, kc_ref, vtp_ref, vtc_ref, bias_ref, wo_ref, g_ref, b_ref,
         o_ref, kwin, q_scr, s_scr, ot_scr, vtwin) = refs
        vtwin[:, 0:A_PAST] = vtp_ref[0]
        vtwin[:, A_PAST:2 * A_PAST] = vtc_ref[0]
        vt_at = lambda rows, r0: vtwin[rows, r0:r0 + A_WIN]
    else:
        x_ref, q_ref, k_ref, vt_ref, bias_ref, wo_ref, g_ref, b_ref, o_ref, kwin, q_scr, s_scr, ot_scr = refs
        vt_at = lambda rows, r0: vt_ref[0, rows, r0:r0 + A_WIN]
    n_pairs = q_ref.shape[2] // LANES
    for p in range(n_pairs):
        cols = slice(p * LANES, (p + 1) * LANES)
        q_scr[p] = q_ref[0, :, cols]
        if prompt:
            kwin[p, 0:A_PAST] = kp_ref[0, :, cols]
            kwin[p, A_PAST:2 * A_PAST] = kc_ref[0, :, cols]
        else:
            kwin[p] = k_ref[0, :, cols]
    lane = lax.broadcasted_iota(I32, (A_SUB, LANES), 1)
    key_row = lax.broadcasted_iota(I32, (A_WIN, 2 * A_SUB), 0)
    out_row = lax.broadcasted_iota(I32, (LANES, A_SUB), 0)
    ones_rows = jnp.ones((ONES_ROWS, A_WIN), BF16)

    def scores(p, j, side, first_valid):
        r0 = j * A_SUB
        q2 = q_scr[p, r0:r0 + A_SUB, :].astype(F32)
        qt = jnp.concatenate([jnp.where(lane < HEAD_DIM, q2, 0.0).T, jnp.where(lane < HEAD_DIM, 0.0, q2).T], axis=1)
        s = _dot(kwin[p, r0:r0 + A_WIN, :], qt.astype(BF16)) + bias_ref[p]
        if first_valid > r0:
            s = jnp.where(key_row >= first_valid - r0, s, -jnp.inf)
        s_scr[side, j] = s

    def absorb(p, j, side):
        r0 = j * A_SUB
        s = s_scr[side, j]
        e = jnp.exp2((s - jnp.max(s, axis=0, keepdims=True)).astype(BF16))
        rows = pl.ds(p * LANES, LANES)
        o = _dot(jnp.concatenate([vt_at(rows, r0), ones_rows], axis=0), e)
        o = o[:LANES] / o[LANES:LANES + 1]
        ot_scr[rows, r0:r0 + A_SUB] = jnp.where(out_row < HEAD_DIM, o[:, :A_SUB], o[:, A_SUB:]).astype(BF16)

    def overlap(p_scores, p_absorb, side, first_valid):
        for j in range(n_sub):
            scores(p_scores, j, side, first_valid)
        for j in range(n_sub):
            absorb(p_absorb, j, 1 - side)

    def attend(first_valid):
        assert n_pairs % 2 == 0
        for j in range(n_sub):
            scores(0, j, 0, first_valid)

        def two_pairs(i, carry):
            overlap(2 * i + 1, 2 * i, 1, first_valid)
            overlap(2 * i + 2, 2 * i + 1, 0, first_valid)
            return carry

        for i in range(n_pairs // 2 - 1):
            two_pairs(i, 0)
        overlap(n_pairs - 1, n_pairs - 2, 1, first_valid)
        for j in range(n_sub):
            absorb(n_pairs - 1, j, 1)

    if prompt:
        pl.when(pl.program_id(1) == 0)(lambda: attend(A_PAST))
        pl.when(pl.program_id(1) > 0)(lambda: attend(0))
    else:
        attend(0)

    y = lax.dot_general(ot_scr[...], wo_ref[...], (((0,), (0,)), ((), ())), preferred_element_type=F32)
    o_ref[0] = _layer_norm(alpha * x_ref[0] + y, g_ref[...], b_ref[...])


def _attn_a_scratch(width, n_keys, n_queries):
    pairs = width // LANES
    return [pltpu.VMEM((pairs, n_keys, LANES), BF16), pltpu.VMEM((pairs, n_queries, LANES), BF16),
            pltpu.VMEM((2, n_queries // A_SUB, A_WIN, 2 * A_SUB), F32), pltpu.VMEM((width, n_queries), BF16)]


def _attn_a_prompt(x, q, k, vt, bias, wo, g, b, alpha):
    bsz, s, d = x.shape
    width = q.shape[2]
    tq = A_PAST
    cur = lambda bi, t: (bi, t, 0)
    prev = lambda bi, t: (bi, jnp.maximum(t - 1, 0), 0)
    cur_t = lambda bi, t: (bi, 0, t)
    prev_t = lambda bi, t: (bi, 0, jnp.maximum(t - 1, 0))
    const2 = lambda bi, t: (0, 0)
    return pl.pallas_call(
        functools.partial(_attn_a_kernel, n_sub=tq // A_SUB, prompt=True, alpha=alpha),
        grid=(bsz, s // tq),
        in_specs=[pl.BlockSpec((1, tq, d), cur), pl.BlockSpec((1, tq, width), cur),
                  pl.BlockSpec((1, tq, width), prev), pl.BlockSpec((1, tq, width), cur),
                  pl.BlockSpec((1, width, tq), prev_t), pl.BlockSpec((1, width, tq), cur_t),
                  _resident(bias.shape), _resident(wo.shape),
                  pl.BlockSpec((1, d), const2), pl.BlockSpec((1, d), const2)],
        out_specs=pl.BlockSpec((1, tq, d), cur),
        out_shape=jax.ShapeDtypeStruct((bsz, s, d), F32),
        scratch_shapes=_attn_a_scratch(width, 2 * tq, tq) + [pltpu.VMEM((width, 2 * tq), BF16)],
        compiler_params=_params(2),
        name="attn_a_prompt",
    )(x, q, k, k, vt, vt, bias, wo, g, b)


def _attn_a_sample(x, q, kwin, vtwin, bias, wo, g, b, alpha):
    bsz, _, d = x.shape
    width = q.shape[2]
    blk = lambda n, c: pl.BlockSpec((1, n, c), lambda bi: (bi, 0, 0))
    const2 = lambda bi: (0, 0)
    return pl.pallas_call(
        functools.partial(_attn_a_kernel, n_sub=1, prompt=False, alpha=alpha),
        grid=(bsz,),
        in_specs=[blk(A_SUB, d), blk(A_SUB, width), blk(A_WIN, width), blk(width, A_WIN),
                  _resident(bias.shape), _resident(wo.shape),
                  pl.BlockSpec((1, d), const2), pl.BlockSpec((1, d), const2)],
        out_specs=blk(A_SUB, d),
        out_shape=jax.ShapeDtypeStruct((bsz, A_SUB, d), F32),
        scratch_shapes=_attn_a_scratch(width, A_WIN, A_SUB),
        compiler_params=_params(1),
        name="attn_a_sample",
    )(x, q, kwin, vtwin, bias, wo, g, b)


def _rel_bias_tables(rel_bias, n_sample):
    r = jnp.arange(A_SUB)[:, None]
    c = jnp.arange(A_WIN)[None, :]
    period = A_SUB + A_WIN
    diff = jnp.arange(period)
    diff = jnp.where(diff < A_WIN, diff, diff - period)
    line = rel_bias[:, jnp.clip(A_PAST - diff, -REL_CLIP, REL_CLIP) + REL_CLIP].astype(F32) * LOG2E
    table = jnp.tile(line, (1, A_SUB))[:, :A_SUB * (period - 1)].reshape(-1, A_SUB, period - 1)[:, :, :A_WIN]
    lo = (r // CHUNK) * CHUNK
    band = (c >= lo) & (c < lo + A_PAST + CHUNK)
    prompt = jnp.where(band[None], table, -jnp.inf)
    live = (c < A_PAST + n_sample)
    sample = jnp.where(live[None], jnp.where((r < n_sample)[None], table, 0.0), -jnp.inf)

    def pair_layout(tab):
        h = tab.shape[0]
        return tab.transpose(0, 2, 1).reshape(h // 2, 2, A_WIN, A_SUB).transpose(0, 2, 1, 3).reshape(h // 2, A_WIN, 2 * A_SUB)

    return pair_layout(prompt), pair_layout(sample)


def _round_to_bf16(x):
    bits = struct.unpack("<I", struct.pack("<f", x))[0]
    bits = (bits + 0x7FFF + ((bits >> 16) & 1)) & 0xFFFF0000
    return struct.unpack("<f", struct.pack("<I", bits))[0]


def _bf16_pieces(x):
    p1 = _round_to_bf16(x)
    p2 = _round_to_bf16(x - p1)
    return (p1, p2, _round_to_bf16(x - p1 - p2))


def _pos_lanes(pos, lane):
    hi = (lax.shift_right_logical(pos, CHUNK_BITS) * CHUNK).astype(F32)
    lo = (pos & (CHUNK - 1)).astype(F32)
    return jnp.where(lane < POS_LANE0 + 3, hi, jnp.where(lane < POS_LANE0 + 6, lo, 0.0))


def _alibi_slopes(n_heads):
    return tuple(2.0 ** (-8.0 * (h + 1) / n_heads) for h in range(n_heads))


def _proj_b_kernel(x_ref, w_ref, qt_ref, qi_ref, ka_ref, kf_ref, vf_ref, ki_ref, kd_ref, wit_ref, vt_ref,
                   *, nq, nkv, nqi, pos0, period):
    h = _dot(x_ref[...].astype(BF16), w_ref[...])
    n_heads = nq // HEAD_DIM
    slopes = _alibi_slopes(n_heads)
    lane_q = lax.broadcasted_iota(I32, (x_ref.shape[0], LANES), 1)
    for hd in range(n_heads):
        q2 = h[:, (hd // 2) * LANES:(hd // 2 + 1) * LANES] * (HEAD_DIM ** -0.5 * LOG2E)
        if hd % 2:
            q2 = pltpu.roll(q2, HEAD_DIM, 1)
        sl = jnp.zeros((1, LANES), F32)
        for i, piece in enumerate(_bf16_pieces(slopes[hd] * LOG2E) * 2):
            sl = jnp.where(lane_q[:1] == POS_LANE0 + i, piece, sl)
        qt_ref[hd] = jnp.where(lane_q < HEAD_DIM, q2, sl).T.astype(BF16)
    k = h[:, nq:nq + nkv]
    v = h[:, nq + nkv:nq + 2 * nkv]
    kf_ref[...] = k
    vf_ref[...] = v
    o0 = nq + 2 * nkv
    qi_ref[...] = (h[:, o0:o0 + nqi] * (HEAD_DIM ** -0.5)).astype(BF16)
    tail = h[:, o0 + nqi:o0 + nqi + LANES]
    ki_ref[...] = tail[:, :HEAD_DIM]
    lane = lax.broadcasted_iota(I32, tail.shape, 1)
    kd_ref[...] = jnp.where(lane < HEAD_DIM, tail, pltpu.roll(tail, HEAD_DIM, 1)).astype(BF16)
    wit_ref[...] = tail.T[HEAD_DIM:HEAD_DIM + IDX_HEADS, :] * (IDX_HEADS ** -0.5)
    tm = tail.shape[0]
    row = pl.program_id(0) * tm + lax.broadcasted_iota(I32, (tm, 1), 0)
    pos = _pos_lanes(pos0 + (row & (period - 1)), lane)
    for pair in range(nkv // LANES):
        for half in range(2):
            tile = slice((2 * pair + half) * LANES, (2 * pair + half + 1) * LANES)
            vp = v[:, pair * LANES:(pair + 1) * LANES]
            kp = k[:, pair * LANES:(pair + 1) * LANES]
            if half:
                vp, kp = pltpu.roll(vp, HEAD_DIM, 1), pltpu.roll(kp, HEAD_DIM, 1)
            vt_ref[0, tile, :] = jnp.where(lane < HEAD_DIM, vp, 1.0).T.astype(BF16)
            ka_ref[:, tile] = jnp.where(lane < HEAD_DIM, kp, pos).astype(BF16)


def _proj_b(x, w_pad, nq, nkv, nqi, tm, pos0, period):
    m, d = x.shape
    assert period & (period - 1) == 0
    rows = lambda c: pl.BlockSpec((tm, c), lambda i: (i, 0))
    outs = [(nqi, BF16), (2 * nkv, BF16), (nkv, F32), (nkv, F32), (HEAD_DIM, F32), (LANES, BF16)]
    n_heads = nq // HEAD_DIM
    return pl.pallas_call(
        functools.partial(_proj_b_kernel, nq=nq, nkv=nkv, nqi=nqi, pos0=pos0, period=period),
        grid=(m // tm,),
        in_specs=[rows(d), _resident(w_pad.shape)],
        out_specs=[pl.BlockSpec((n_heads, LANES, tm), lambda i: (0, 0, i))] + [rows(c) for c, _ in outs]
        + [pl.BlockSpec((IDX_HEADS, tm), lambda i: (0, i)), pl.BlockSpec((1, 2 * nkv, tm), lambda i: (i, 0, 0))],
        out_shape=[jax.ShapeDtypeStruct((n_heads, LANES, m), BF16)] + [jax.ShapeDtypeStruct((m, c), dt) for c, dt in outs]
        + [jax.ShapeDtypeStruct((IDX_HEADS, m), F32), jax.ShapeDtypeStruct((m // tm, 2 * nkv, tm), BF16)],
        compiler_params=_params(1),
        name="proj_b",
    )(x, w_pad)


def _sparse_kernel(x_ref, qt_ref, qi_ref, wit_ref, ka_ref, vt_ref, kd_ref, wo_ref, g_ref, b_ref, o_ref,
                   key_scr, hi_scr, lo_scr, bias_scr, s_scr, acc_scr, m_scr, ot_scr,
                   *, tq, kb, pos0, n_keys, n_sel, n_heads, group, slopes, alpha):
    t = pl.program_id(1)
    q0 = pos0 + t * tq
    qpos_row = q0 + lax.broadcasted_iota(I32, (1, tq), 1)
    lim_row = jnp.minimum((lax.shift_right_logical(qpos_row, CHUNK_BITS) + 1) * CHUNK, n_keys)
    kmax = jnp.minimum((lax.shift_right_logical(q0 + tq - 1, CHUNK_BITS) + 1) * CHUNK, n_keys)
    nkb = lax.shift_right_logical(kmax + kb - 1, int(math.log2(kb)))
    lane = lax.broadcasted_iota(I32, (tq, LANES), 1)
    lo_half = lane < HEAD_DIM

    def key_block(kbi):
        return pl.multiple_of(kbi * kb, kb)

    def key_index(off):
        return off + lax.broadcasted_iota(I32, (kb, tq), 0)

    qis = []
    for p in range(IDX_HEADS // 2):
        q2 = qi_ref[0, :, p * LANES:(p + 1) * LANES]
        qis.append(jnp.where(lo_half, q2, jnp.zeros_like(q2)))
        qis.append(jnp.where(lo_half, jnp.zeros_like(q2), q2))
    wis = wit_ref[...]
    qi_all = jnp.concatenate(qis, axis=0)

    def score_block(kbi, carry):
        off = key_block(kbi)
        dots = _dot_nt(kd_ref[0, pl.ds(off, kb), :], qi_all)
        acc = jnp.zeros((kb, tq), F32)
        for h in range(IDX_HEADS):
            acc = acc + jnp.maximum(dots[:, h * tq:(h + 1) * tq], 0.0) * wis[h:h + 1, :]
        acc = jnp.where(key_index(off) < lim_row, acc, -jnp.inf)
        bits = lax.bitcast_convert_type(acc, I32)
        key = bits ^ (lax.shift_right_arithmetic(bits, 31) & 0x7FFFFFFF)
        key_scr[pl.ds(off, kb), :] = key
        hi_scr[pl.ds(off, kb), :] = lax.shift_right_arithmetic(key, HALF_BITS).astype(I16)
        lo_scr[pl.ds(off, kb), :] = ((key & ((1 << HALF_BITS) - 1)) + I16_MIN).astype(I16)
        return carry

    lax.fori_loop(0, nkb, score_block, 0)

    @pl.when(nkb % 2 == 1)
    def _():
        key_scr[pl.ds(key_block(nkb), kb), :] = jnp.full((kb, tq), INT_MIN, I32)
        hi_scr[pl.ds(key_block(nkb), kb), :] = jnp.full((kb, tq), I16_MIN, I16)
        lo_scr[pl.ds(key_block(nkb), kb), :] = jnp.full((kb, tq), I16_MIN, I16)

    n_steps = lax.shift_right_logical(nkb + 1, 1)

    def step_rows(i):
        return pl.ds(pl.multiple_of(i * 2 * kb, 2 * kb), 2 * kb)

    def count(pred):
        def body(i, part):
            hit = pred(key_scr[step_rows(i), :])
            return part + jnp.sum(jnp.where(hit, 1, 0).reshape(-1, COUNT_ROWS, tq), axis=0)
        part = lax.fori_loop(0, n_steps, body, jnp.zeros((COUNT_ROWS, tq), I32))
        return jnp.sum(part, axis=0, keepdims=True)

    def count16(ref, pred):
        def body(i, parts):
            ind = jnp.where(pred(ref[pl.ds(key_block(i), kb), :]), jnp.int16(1), jnp.int16(0))
            parts = list(parts)
            for j, r in enumerate(range(0, kb, COUNT16_ROWS)):
                parts[j % len(parts)] = parts[j % len(parts)] + ind[r:r + COUNT16_ROWS]
            return tuple(parts)
        parts = lax.fori_loop(0, 2 * n_steps, body, (jnp.zeros((COUNT16_ROWS, tq), I16),) * COUNT16_CHAINS)
        return jnp.sum(sum(p.astype(I32) for p in parts), axis=0, keepdims=True)

    def as_i16(row):
        return jnp.broadcast_to(row, (COUNT16_ROWS, tq)).astype(I16)[:1]

    def bisect16(ref, want):
        def body(i, thr):
            cand = thr + lax.shift_left(jnp.int32(1), HALF_BITS - 1 - i)
            cand16 = as_i16(cand)
            return jnp.where(count16(ref, lambda v: v >= cand16) >= want, cand, thr)
        return lax.fori_loop(0, HALF_BITS, body, jnp.full((1, tq), I16_MIN, I32))

    thr_hi = bisect16(hi_scr, n_sel)
    thr_hi16 = as_i16(thr_hi)
    n_hi_gt = count16(hi_scr, lambda v: v > thr_hi16)

    def keep_bucket(i, carry):
        rows = step_rows(i)
        lo_scr[rows, :] = jnp.where(hi_scr[rows, :] == thr_hi16, lo_scr[rows, :], jnp.int16(I16_MIN))
        return carry

    lax.fori_loop(0, n_steps, keep_bucket, 0)
    thr = thr_hi * (1 << HALF_BITS) + (bisect16(lo_scr, n_sel - n_hi_gt) - I16_MIN)
    n_gt = count(lambda key: key > thr)
    n_eq = count(lambda key: key == thr)
    need = n_sel - n_gt
    tie = jnp.max(jnp.where((n_eq > need) & (thr > NEG_INF_KEY), 1, 0)) > 0

    @pl.when(jnp.logical_not(tie))
    def _():
        def body(kbi, carry):
            off = key_block(kbi)
            sel = (key_scr[pl.ds(off, kb), :] >= thr) & (key_index(off) < lim_row)
            bias_scr[pl.ds(off, kb), :] = jnp.where(sel, 0.0, -MASK_BIAS)
            return carry
        lax.fori_loop(0, nkb, body, 0)

    @pl.when(tie)
    def _():
        tri = jnp.where(lax.broadcasted_iota(I32, (kb, kb), 0) > lax.broadcasted_iota(I32, (kb, kb), 1),
                        1.0, 0.0).astype(BF16)
        need_f = need.astype(F32)

        def body(kbi, seen):
            off = key_block(kbi)
            key = key_scr[pl.ds(off, kb), :]
            eq = jnp.where(key == thr, 1.0, 0.0)
            earlier = _dot(tri, eq.astype(BF16)) + seen
            sel = ((key > thr) | ((key == thr) & (earlier < need_f))) & (key_index(off) < lim_row)
            bias_scr[pl.ds(off, kb), :] = jnp.where(sel, 0.0, -MASK_BIAS)
            return seen + jnp.sum(eq, axis=0, keepdims=True)
        lax.fori_loop(0, nkb, body, jnp.zeros((1, tq), F32))

    unit_heads = s_scr.shape[2] // tq
    n_units = n_heads // unit_heads
    kv_lanes = lambda u: slice((u * unit_heads // group) * LANES, (u * unit_heads // group + 1) * LANES)
    m_scr[...] = jnp.full(m_scr.shape, -jnp.inf, F32)
    acc_scr[...] = jnp.zeros(acc_scr.shape, F32)

    def scores(kbi, u, last):
        off = key_block(kbi)
        bias = jnp.concatenate([bias_scr[pl.ds(off, kb), :]] * unit_heads, axis=1)
        q_t = jnp.concatenate([qt_ref[u * unit_heads + j] for j in range(unit_heads)], axis=1)
        s = _dot(ka_ref[0, pl.ds(off, kb), kv_lanes(u)], q_t) + bias
        if last:
            ahead = jnp.maximum(key_index(off) - qpos_row, 0).astype(F32)
            s = s - jnp.concatenate([(2.0 * slopes[u * unit_heads + j] * LOG2E) * ahead for j in range(unit_heads)], axis=1)
        s_scr[u] = s

    def absorb(kbi, u):
        s = s_scr[u]
        m_old = m_scr[u]
        m_new = jnp.maximum(m_old, jnp.max(s, axis=0, keepdims=True))
        p = jnp.exp2((s - m_new).astype(BF16))
        acc_scr[u] = acc_scr[u] * jnp.exp2(m_old - m_new) + _dot(vt_ref[0, kbi, kv_lanes(u), :], p)
        m_scr[u] = m_new

    def step(kbi, last):
        for u in range(n_units):
            absorb(kbi - 1, u)
            scores(kbi, u, last)

    def first_scores(last):
        for u in range(n_units):
            scores(0, u, last)

    def step_body(kbi, carry):
        step(kbi, False)
        return carry

    pl.when(nkb == 1)(lambda: first_scores(True))
    pl.when(nkb > 1)(lambda: first_scores(False))
    lax.fori_loop(1, nkb - 1, step_body, 0)
    pl.when(nkb > 1)(lambda: step(nkb - 1, True))
    for u in range(n_units):
        absorb(nkb - 1, u)

    for u in range(n_units):
        acc = acc_scr[u]
        o_t = acc[:HEAD_DIM] / acc[HEAD_DIM:HEAD_DIM + 1]
        for j in range(unit_heads):
            h = u * unit_heads + j
            ot_scr[h * HEAD_DIM:(h + 1) * HEAD_DIM, :] = o_t[:, j * tq:(j + 1) * tq].astype(BF16)
    y = lax.dot_general(ot_scr[...], wo_ref[...], (((0,), (0,)), ((), ())), preferred_element_type=F32)
    o_ref[0] = _layer_norm(alpha * x_ref[0] + y, g_ref[...], b_ref[...])


def _sparse_attn(x, qt, qi, wit, k, vt, kd, wo, g, b, *, tq, kb, pos0, n_keys, alpha):
    bsz, s, d = x.shape
    n_t = s // tq
    l_pad = k.shape[1]
    n_heads = qt.shape[0]
    n_kv = k.shape[2] // LANES
    group = n_heads // n_kv
    assert kb % tq == 0 and pos0 % tq == 0 and l_pad % kb == 0 and vt.shape == (bsz, l_pad // kb, k.shape[2], kb)
    slopes = _alibi_slopes(n_heads)
    qrow = lambda c: pl.BlockSpec((1, tq, c), lambda bi, t: (bi, t, 0))
    key_mode = pl.Buffered(2)
    keys = lambda c: pl.BlockSpec((1, l_pad, c), lambda bi, t: (bi, 0, 0), pipeline_mode=key_mode)
    const2 = lambda bi, t: (0, 0)
    sel_rows = -(-l_pad // (2 * kb)) * 2 * kb
    unit_heads = min(max(UNIT_LANES // tq, 1), group)
    assert group % unit_heads == 0
    n_units, unit_w = n_heads // unit_heads, unit_heads * tq
    kern = functools.partial(_sparse_kernel, tq=tq, kb=kb, pos0=pos0, n_keys=n_keys, n_sel=min(TOPK_MAX, n_keys // 4),
                             n_heads=n_heads, group=group, slopes=slopes, alpha=alpha)
    return pl.pallas_call(
        kern,
        grid=(bsz, n_t),
        in_specs=[qrow(d), pl.BlockSpec((n_heads, LANES, tq), lambda bi, t: (0, 0, bi * n_t + t)), qrow(qi.shape[2]),
                  pl.BlockSpec((IDX_HEADS, tq), lambda bi, t: (0, bi * n_t + t)),
                  keys(k.shape[2]),
                  pl.BlockSpec((1,) + vt.shape[1:], lambda bi, t: (bi, 0, 0, 0), pipeline_mode=key_mode),
                  keys(kd.shape[2]),
                  _resident(wo.shape), pl.BlockSpec((1, d), const2), pl.BlockSpec((1, d), const2)],
        out_specs=qrow(d),
        out_shape=jax.ShapeDtypeStruct((bsz, s, d), F32),
        scratch_shapes=[pltpu.VMEM((sel_rows, tq), I32), pltpu.VMEM((sel_rows, tq), I16), pltpu.VMEM((sel_rows, tq), I16),
                        pltpu.VMEM((l_pad, tq), F32),
                        pltpu.VMEM((n_units, kb, unit_w), F32), pltpu.VMEM((n_units, LANES, unit_w), F32),
                        pltpu.VMEM((n_units, 1, unit_w), F32), pltpu.VMEM((n_heads * HEAD_DIM, tq), BF16)],
        compiler_params=_params(2),
        name="sparse_attn",
    )(x, qt, qi, wit, k, vt, kd, wo, g, b)


def _pad_rows(a, n):
    return jnp.pad(a, ((0, 0), (0, n - a.shape[1]), (0, 0)))


def _mixer_a(xp, xs, cache_k, cache_v, w_in, rel_bias, w_out, g, b, alpha):
    bsz, s, d = xp.shape
    dbs, t, _ = xs.shape
    width = w_out.shape[0]
    w_in = w_in.astype(BF16)
    w_out = w_out.astype(BF16)
    bias_p, bias_s = _rel_bias_tables(rel_bias, t)
    keep = min(A_PAST, s)
    q, k, vt, kf, vf = _proj_a(xp, w_in, keep)
    yp = _attn_a_prompt(xp, q, k, vt, bias_p, w_out, g, b, alpha)
    qs, ks, _, ksf, vsf = _proj_a(xs.reshape(1, dbs * t, d), w_in, dbs * t)
    n_cache = cache_k.shape[1]
    kwin = _pad_rows(jnp.concatenate([cache_k.reshape(dbs, n_cache, width).astype(BF16), ks.reshape(dbs, t, width)], 1), A_WIN)
    vwin = _pad_rows(jnp.concatenate([cache_v.reshape(dbs, n_cache, width), vsf.reshape(dbs, t, width)], 1).astype(BF16), A_WIN)
    ys = _attn_a_sample(_pad_rows(xs, A_SUB), _pad_rows(qs.reshape(dbs, t, width), A_SUB), kwin, vwin.transpose(0, 2, 1),
                        bias_s, w_out, g, b, alpha)[:, :t]
    heads = width // HEAD_DIM
    return (yp, ys, kf.reshape(bsz, keep, heads, HEAD_DIM), vf.reshape(bsz, keep, heads, HEAD_DIM),
            ksf.reshape(dbs, t, heads, HEAD_DIM), vsf.reshape(dbs, t, heads, HEAD_DIM))


def _mixer_b(xp, xs, cache_k, cache_v, cache_idx, w_in, w_out, g, b, alpha):
    bsz, s, d = xp.shape
    dbs, t, _ = xs.shape
    nq = w_out.shape[0]
    nkv = cache_k.shape[2] * cache_k.shape[3]
    nqi = IDX_HEADS * HEAD_DIM
    kvh = nkv // HEAD_DIM
    past = cache_k.shape[1]
    w_pad = jnp.pad(w_in, ((0, 0), (0, nq + 2 * nkv + nqi + LANES - w_in.shape[1]))).astype(BF16)
    w_out = w_out.astype(BF16)
    kb = PROJ_ROWS

    qt, qi, k, kf, vf, kif, kd, wit, vt = _proj_b(xp.reshape(bsz * s, d), w_pad, nq, nkv, nqi, kb, 0, s)
    r3 = lambda a: a.reshape(bsz, s, a.shape[1])
    yp = _sparse_attn(xp, qt, r3(qi), wit, r3(k), vt.reshape(bsz, s // kb, 2 * nkv, kb), r3(kd), w_out, g, b,
                      tq=B_QUERY_TILE, kb=kb, pos0=0, n_keys=s, alpha=alpha)

    qts, qis, ks, ksf, vsf, kisf, kds, wits, _ = _proj_b(xs.reshape(dbs * t, d), w_pad, nq, nkv, nqi, dbs * t, past, t)
    tq_s = B_SAMPLE_TILE
    n_keys = past + t
    l_pad = -(-n_keys // kb) * kb
    s3 = lambda a: a.reshape(dbs, t, a.shape[1])
    v_all = jnp.concatenate([cache_v, vsf.reshape(dbs, t, kvh, HEAD_DIM)], axis=1).astype(BF16)
    v_all = jnp.concatenate([v_all, jnp.ones_like(v_all)], axis=-1).reshape(dbs, n_keys, 2 * nkv)
    keys_vt = _pad_rows(v_all, l_pad).reshape(dbs, l_pad // kb, kb, 2 * nkv).transpose(0, 1, 3, 2)
    kd_cache = jnp.concatenate([cache_idx, cache_idx], axis=-1).astype(BF16)
    frame = jnp.arange(past, dtype=I32)[:, None]
    pos = _pos_lanes(frame, HEAD_DIM + jnp.arange(HEAD_DIM, dtype=I32)[None, :]).astype(BF16)
    pos = jnp.broadcast_to(pos[None, :, None, :], (dbs, past, kvh, HEAD_DIM))
    k_cache = jnp.concatenate([cache_k.astype(BF16), pos], axis=-1).reshape(dbs, past, 2 * nkv)
    keys_k = _pad_rows(jnp.concatenate([k_cache, s3(ks)], 1), l_pad)
    keys_kd = _pad_rows(jnp.concatenate([kd_cache, s3(kds)], 1), l_pad)
    wit_pad = jnp.pad(wits.reshape(IDX_HEADS, dbs, t), ((0, 0), (0, 0), (0, tq_s - t)), mode="edge").reshape(IDX_HEADS, dbs * tq_s)
    qis_pad = jnp.pad(s3(qis), ((0, 0), (0, tq_s - t), (0, 0)), mode="edge")
    qts_pad = jnp.pad(qts.reshape(-1, LANES, dbs, t), ((0, 0), (0, 0), (0, 0), (0, tq_s - t))).reshape(-1, LANES, dbs * tq_s)
    ys = _sparse_attn(_pad_rows(xs, tq_s), qts_pad, qis_pad, wit_pad,
                      keys_k, keys_vt, keys_kd, w_out, g, b,
                      tq=tq_s, kb=kb, pos0=past, n_keys=n_keys, alpha=alpha)[:, :t]
    return (yp, ys,
            kf.reshape(bsz, s, kvh, HEAD_DIM), vf.reshape(bsz, s, kvh, HEAD_DIM), kif.reshape(bsz, s, HEAD_DIM),
            ksf.reshape(dbs, t, kvh, HEAD_DIM), vsf.reshape(dbs, t, kvh, HEAD_DIM), kisf.reshape(dbs, t, HEAD_DIM))


def kernel(x_prompt, x_sample, cache_a_k, cache_a_v, cache_b_k, cache_b_v, cache_b_idx, ln_g, ln_b, ffn_w_gate, ffn_w_up, ffn_w_down, a_w_in, a_rel_bias, a_w_out, b_w_in, b_w_out):
    depth = ln_g.shape[0]
    alpha = (2.0 * depth) ** 0.25
    bsz, s, d = x_prompt.shape
    dbs, t, _ = x_sample.shape
    xp, xs = x_prompt, x_sample
    a_out, b_out = [], []

    wg_all, wu_all, wd_all = ffn_w_gate.astype(BF16), ffn_w_up.astype(BF16), ffn_w_down.astype(BF16)

    def ffn(x, layer, i, tm):
        shp = x.shape
        y = _ffn_block(x.reshape(-1, d), wg_all[layer, i], wu_all[layer, i], wd_all[layer, i],
                       ln_g[layer, 2 * i][None], ln_b[layer, 2 * i][None], alpha, tm)
        return y.reshape(shp)

    for layer in range(depth):
        j = layer // 2
        xp = ffn(xp, layer, 0, FFN_ROWS)
        xs = ffn(xs, layer, 0, dbs * t)
        g, b = ln_g[layer, 1][None], ln_b[layer, 1][None]
        if layer % 2 == 0:
            xp, xs, *rest = _mixer_a(xp, xs, cache_a_k[j], cache_a_v[j], a_w_in[j], a_rel_bias[j], a_w_out[j], g, b, alpha)
            a_out.append(rest)
        else:
            xp, xs, *rest = _mixer_b(xp, xs, cache_b_k[j], cache_b_v[j], cache_b_idx[j], b_w_in[j], b_w_out[j], g, b, alpha)
            b_out.append(rest)
        xp = ffn(xp, layer, 1, FFN_ROWS)
        xs = ffn(xs, layer, 1, dbs * t)

    stack = lambda outs, i: jnp.stack([o[i] for o in outs], 0)
    return (xp, xs,
            stack(a_out, 0), stack(a_out, 1), stack(a_out, 2), stack(a_out, 3),
            stack(b_out, 0), stack(b_out, 1), stack(b_out, 2), stack(b_out, 3), stack(b_out, 4), stack(b_out, 5))
```

```python
import functools
import math
import struct

import jax
import jax.numpy as jnp
from jax import lax
from jax.experimental import pallas as pl
from jax.experimental.pallas import tpu as pltpu

F32 = jnp.float32
BF16 = jnp.bfloat16
I32 = jnp.int32
I16 = jnp.int16

CHUNK = 64
CHUNK_BITS = 6
HALF_BITS = 16
FFN_ROWS = 1024
PROJ_ROWS = 512
B_QUERY_TILE = 256
B_SAMPLE_TILE = 128
A_PAST = 512
REL_CLIP = 128
HEAD_DIM = 64
LANES = 128
IDX_HEADS = 8
TOPK_MAX = 256
LN_EPS = 1e-5
A_SUB = 4 * CHUNK
A_WIN = A_PAST + A_SUB
VMEM_LIMIT = 52 * 1024 * 1024

LOG2E = 1.4426950408889634
POS_LANE0 = HEAD_DIM
MASK_BIAS = 1e30
ONES_ROWS = 16
FFN_CHUNK = 768
UNIT_LANES = 512
COUNT_ROWS = 16

assert 1 << CHUNK_BITS == CHUNK

NT_DIMS = (((1,), (1,)), ((), ()))
INT_MIN = -2147483648
I16_MIN = -32768
COUNT16_ROWS = 16
COUNT16_CHAINS = 4
NEG_INF_KEY = -2139095041


def _params(n_grid, flags=None):
    return pltpu.CompilerParams(dimension_semantics=("arbitrary",) * n_grid,
                                vmem_limit_bytes=VMEM_LIMIT, flags=flags)


def _resident(shape):
    zeros = (0,) * len(shape)
    return pl.BlockSpec(shape, lambda *_: zeros, pipeline_mode=pl.Buffered(1))


def _layer_norm(z, g, b):
    mu = jnp.mean(z, axis=-1, keepdims=True)
    d = z - mu
    var = jnp.mean(d * d, axis=-1, keepdims=True)
    return d * lax.rsqrt(var + LN_EPS) * g + b


def _dot(a, b):
    return jnp.dot(a, b, preferred_element_type=F32)


def _dot_nt(a, b):
    return lax.dot_general(a, b, NT_DIMS, preferred_element_type=F32)


def _ffn_kernel(x_ref, wg_ref, wu_ref, wd_ref, g_ref, b_ref, o_ref, h_scr, *, alpha):
    x = x_ref[...]
    xb = x.astype(BF16)
    f = wg_ref.shape[1]
    for c0 in range(0, f, FFN_CHUNK):
        sl = slice(c0, min(c0 + FFN_CHUNK, f))
        gate = _dot(xb, wg_ref[:, sl])
        up = _dot(xb, wu_ref[:, sl])
        h_scr[:, sl] = (gate * (1.0 / (1.0 + jnp.exp(-gate))) * up).astype(BF16)
    y = _dot(h_scr[...], wd_ref[...])
    o_ref[...] = _layer_norm(alpha * x + 0.5 * y, g_ref[...], b_ref[...])


def _ffn_block(x, wg, wu, wd, g, b, alpha, tm):
    m, d = x.shape
    f = wg.shape[1]
    const = lambda i: (0, 0)
    return pl.pallas_call(
        functools.partial(_ffn_kernel, alpha=alpha),
        grid=(m // tm,),
        in_specs=[pl.BlockSpec((tm, d), lambda i: (i, 0)),
                  _resident((d, f)), _resident((d, f)), _resident((f, d)),
                  pl.BlockSpec((1, d), const), pl.BlockSpec((1, d), const)],
        out_specs=pl.BlockSpec((tm, d), lambda i: (i, 0)),
        out_shape=jax.ShapeDtypeStruct((m, d), F32),
        scratch_shapes=[pltpu.VMEM((tm, f), BF16)],
        compiler_params=_params(1),
        name="ffn_ln",
    )(x, wg, wu, wd, g, b)


def _proj_a_kernel(x_ref, w_ref, q_ref, k_ref, vt_ref, kf_ref, vf_ref, *, width):
    h = _dot(x_ref[0].astype(BF16), w_ref[...])
    k = h[:, width:2 * width]
    v = h[:, 2 * width:]
    q_ref[0] = (h[:, :width] * (HEAD_DIM ** -0.5 * LOG2E)).astype(BF16)
    k_ref[0] = k.astype(BF16)
    vt_ref[0] = v.T.astype(BF16)

    @pl.when(pl.program_id(1) == pl.num_programs(1) - 1)
    def _():
        kf_ref[0] = k
        vf_ref[0] = v


def _proj_a(x, w, tm):
    bsz, s, d = x.shape
    width = w.shape[1] // 3
    row = pl.BlockSpec((1, tm, width), lambda b, t: (b, t, 0))
    last = pl.BlockSpec((1, tm, width), lambda b, t: (b, 0, 0))
    return pl.pallas_call(
        functools.partial(_proj_a_kernel, width=width),
        grid=(bsz, s // tm),
        in_specs=[pl.BlockSpec((1, tm, d), lambda b, t: (b, t, 0)), _resident(w.shape)],
        out_specs=[row, row, pl.BlockSpec((1, width, tm), lambda b, t: (b, 0, t)), last, last],
        out_shape=[jax.ShapeDtypeStruct((bsz, s, width), BF16)] * 2 + [jax.ShapeDtypeStruct((bsz, width, s), BF16)]
        + [jax.ShapeDtypeStruct((bsz, tm, width), F32)] * 2,
        compiler_params=_params(2),
        name="proj_a",
    )(x, w)


def _attn_a_kernel(*refs, n_sub, prompt, alpha):
    if prompt:
        (x_ref, q_ref, kp_ref, kc_ref, vtp_ref, vtc_ref, bias_ref, wo_ref, g_ref, b_ref,
         o_ref, kwin, q_scr, s_scr, ot_scr, vtwin) = refs
        vtwin[:, 0:A_PAST] = vtp_ref[0]
        vtwin[:, A_PAST:2 * A_PAST] = vtc_ref[0]
        vt_at = lambda rows, r0: vtwin[rows, r0:r0 + A_WIN]
    else:
        x_ref, q_ref, k_ref, vt_ref, bias_ref, wo_ref, g_ref, b_ref, o_ref, kwin, q_scr, s_scr, ot_scr = refs
        vt_at = lambda rows, r0: vt_ref[0, rows, r0:r0 + A_WIN]
    n_pairs = q_ref.shape[2] // LANES
    for p in range(n_pairs):
        cols = slice(p * LANES, (p + 1) * LANES)
        q_scr[p] = q_ref[0, :, cols]
        if prompt:
            kwin[p, 0:A_PAST] = kp_ref[0, :, cols]
            kwin[p, A_PAST:2 * A_PAST] = kc_ref[0, :, cols]
        else:
            kwin[p] = k_ref[0, :, cols]
    lane = lax.broadcasted_iota(I32, (A_SUB, LANES), 1)
    key_row = lax.broadcasted_iota(I32, (A_WIN, 2 * A_SUB), 0)
    out_row = lax.broadcasted_iota(I32, (LANES, A_SUB), 0)
    ones_rows = jnp.ones((ONES_ROWS, A_WIN), BF16)

    def scores(p, j, side, first_valid):
        r0 = j * A_SUB
        q2 = q_scr[p, r0:r0 + A_SUB, :].astype(F32)
        qt = jnp.concatenate([jnp.where(lane < HEAD_DIM, q2, 0.0).T, jnp.where(lane < HEAD_DIM, 0.0, q2).T], axis=1)
        s = _dot(kwin[p, r0:r0 + A_WIN, :], qt.astype(BF16)) + bias_ref[p]
        if first_valid > r0:
            s = jnp.where(key_row >= first_valid - r0, s, -jnp.inf)
        s_scr[side, j] = s

    def absorb(p, j, side):
        r0 = j * A_SUB
        s = s_scr[side, j]
        e = jnp.exp2((s - jnp.max(s, axis=0, keepdims=True)).astype(BF16))
        rows = pl.ds(p * LANES, LANES)
        o = _dot(jnp.concatenate([vt_at(rows, r0), ones_rows], axis=0), e)
        o = o[:LANES] / o[LANES:LANES + 1]
        ot_scr[rows, r0:r0 + A_SUB] = jnp.where(out_row < HEAD_DIM, o[:, :A_SUB], o[:, A_SUB:]).astype(BF16)

    def overlap(p_scores, p_absorb, side, first_valid):
        for j in range(n_sub):
            scores(p_scores, j, side, first_valid)
        for j in range(n_sub):
            absorb(p_absorb, j, 1 - side)

    def attend(first_valid):
        assert n_pairs % 2 == 0
        for j in range(n_sub):
            scores(0, j, 0, first_valid)

        def two_pairs(i, carry):
            overlap(2 * i + 1, 2 * i, 1, first_valid)
            overlap(2 * i + 2, 2 * i + 1, 0, first_valid)
            return carry

        for i in range(n_pairs // 2 - 1):
            two_pairs(i, 0)
        overlap(n_pairs - 1, n_pairs - 2, 1, first_valid)
        for j in range(n_sub):
            absorb(n_pairs - 1, j, 1)

    if prompt:
        pl.when(pl.program_id(1) == 0)(lambda: attend(A_PAST))
        pl.when(pl.program_id(1) > 0)(lambda: attend(0))
    else:
        attend(0)

    y = lax.dot_general(ot_scr[...], wo_ref[...], (((0,), (0,)), ((), ())), preferred_element_type=F32)
    o_ref[0] = _layer_norm(alpha * x_ref[0] + y, g_ref[...], b_ref[...])


def _attn_a_scratch(width, n_keys, n_queries):
    pairs = width // LANES
    return [pltpu.VMEM((pairs, n_keys, LANES), BF16), pltpu.VMEM((pairs, n_queries, LANES), BF16),
            pltpu.VMEM((2, n_queries // A_SUB, A_WIN, 2 * A_SUB), F32), pltpu.VMEM((width, n_queries), BF16)]


def _attn_a_prompt(x, q, k, vt, bias, wo, g, b, alpha):
    bsz, s, d = x.shape
    width = q.shape[2]
    tq = A_PAST
    cur = lambda bi, t: (bi, t, 0)
    prev = lambda bi, t: (bi, jnp.maximum(t - 1, 0), 0)
    cur_t = lambda bi, t: (bi, 0, t)
    prev_t = lambda bi, t: (bi, 0, jnp.maximum(t - 1, 0))
    const2 = lambda bi, t: (0, 0)
    return pl.pallas_call(
        functools.partial(_attn_a_kernel, n_sub=tq // A_SUB, prompt=True, alpha=alpha),
        grid=(bsz, s // tq),
        in_specs=[pl.BlockSpec((1, tq, d), cur), pl.BlockSpec((1, tq, width), cur),
                  pl.BlockSpec((1, tq, width), prev), pl.BlockSpec((1, tq, width), cur),
                  pl.BlockSpec((1, width, tq), prev_t), pl.BlockSpec((1, width, tq), cur_t),
                  _resident(bias.shape), _resident(wo.shape),
                  pl.BlockSpec((1, d), const2), pl.BlockSpec((1, d), const2)],
        out_specs=pl.BlockSpec((1, tq, d), cur),
        out_shape=jax.ShapeDtypeStruct((bsz, s, d), F32),
        scratch_shapes=_attn_a_scratch(width, 2 * tq, tq) + [pltpu.VMEM((width, 2 * tq), BF16)],
        compiler_params=_params(2),
        name="attn_a_prompt",
    )(x, q, k, k, vt, vt, bias, wo, g, b)


def _attn_a_sample(x, q, kwin, vtwin, bias, wo, g, b, alpha):
    bsz, _, d = x.shape
    width = q.shape[2]
    blk = lambda n, c: pl.BlockSpec((1, n, c), lambda bi: (bi, 0, 0))
    const2 = lambda bi: (0, 0)
    return pl.pallas_call(
        functools.partial(_attn_a_kernel, n_sub=1, prompt=False, alpha=alpha),
        grid=(bsz,),
        in_specs=[blk(A_SUB, d), blk(A_SUB, width), blk(A_WIN, width), blk(width, A_WIN),
                  _resident(bias.shape), _resident(wo.shape),
                  pl.BlockSpec((1, d), const2), pl.BlockSpec((1, d), const2)],
        out_specs=blk(A_SUB, d),
        out_shape=jax.ShapeDtypeStruct((bsz, A_SUB, d), F32),
        scratch_shapes=_attn_a_scratch(width, A_WIN, A_SUB),
        compiler_params=_params(1),
        name="attn_a_sample",
    )(x, q, kwin, vtwin, bias, wo, g, b)


def _rel_bias_tables(rel_bias, n_sample):
    r = jnp.arange(A_SUB)[:, None]
    c = jnp.arange(A_WIN)[None, :]
    period = A_SUB + A_WIN
    diff = jnp.arange(period)
    diff = jnp.where(diff < A_WIN, diff, diff - period)
    line = rel_bias[:, jnp.clip(A_PAST - diff, -REL_CLIP, REL_CLIP) + REL_CLIP].astype(F32) * LOG2E
    table = jnp.tile(line, (1, A_SUB))[:, :A_SUB * (period - 1)].reshape(-1, A_SUB, period - 1)[:, :, :A_WIN]
    lo = (r // CHUNK) * CHUNK
    band = (c >= lo) & (c < lo + A_PAST + CHUNK)
    prompt = jnp.where(band[None], table, -jnp.inf)
    live = (c < A_PAST + n_sample)
    sample = jnp.where(live[None], jnp.where((r < n_sample)[None], table, 0.0), -jnp.inf)

    def pair_layout(tab):
        h = tab.shape[0]
        return tab.transpose(0, 2, 1).reshape(h // 2, 2, A_WIN, A_SUB).transpose(0, 2, 1, 3).reshape(h // 2, A_WIN, 2 * A_SUB)

    return pair_layout(prompt), pair_layout(sample)


def _round_to_bf16(x):
    bits = struct.unpack("<I", struct.pack("<f", x))[0]
    bits = (bits + 0x7FFF + ((bits >> 16) & 1)) & 0xFFFF0000
    return struct.unpack("<f", struct.pack("<I", bits))[0]


def _bf16_pieces(x):
    p1 = _round_to_bf16(x)
    p2 = _round_to_bf16(x - p1)
    return (p1, p2, _round_to_bf16(x - p1 - p2))


def _pos_lanes(pos, lane):
    hi = (lax.shift_right_logical(pos, CHUNK_BITS) * CHUNK).astype(F32)
    lo = (pos & (CHUNK - 1)).astype(F32)
    return jnp.where(lane < POS_LANE0 + 3, hi, jnp.where(lane < POS_LANE0 + 6, lo, 0.0))


def _alibi_slopes(n_heads):
    return tuple(2.0 ** (-8.0 * (h + 1) / n_heads) for h in range(n_heads))


def _proj_b_kernel(x_ref, w_ref, qt_ref, qi_ref, ka_ref, kf_ref, vf_ref, ki_ref, kd_ref, wit_ref, vt_ref,
                   *, nq, nkv, nqi, pos0, period):
    h = _dot(x_ref[...].astype(BF16), w_ref[...])
    n_heads = nq // HEAD_DIM
    slopes = _alibi_slopes(n_heads)
    lane_q = lax.broadcasted_iota(I32, (x_ref.shape[0], LANES), 1)
    for hd in range(n_heads):
        q2 = h[:, (hd // 2) * LANES:(hd // 2 + 1) * LANES] * (HEAD_DIM ** -0.5 * LOG2E)
        if hd % 2:
            q2 = pltpu.roll(q2, HEAD_DIM, 1)
        sl = jnp.zeros((1, LANES), F32)
        for i, piece in enumerate(_bf16_pieces(slopes[hd] * LOG2E) * 2):
            sl = jnp.where(lane_q[:1] == POS_LANE0 + i, piece, sl)
        qt_ref[hd] = jnp.where(lane_q < HEAD_DIM, q2, sl).T.astype(BF16)
    k = h[:, nq:nq + nkv]
    v = h[:, nq + nkv:nq + 2 * nkv]
    kf_ref[...] = k
    vf_ref[...] = v
    o0 = nq + 2 * nkv
    qi_ref[...] = (h[:, o0:o0 + nqi] * (HEAD_DIM ** -0.5)).astype(BF16)
    tail = h[:, o0 + nqi:o0 + nqi + LANES]
    ki_ref[...] = tail[:, :HEAD_DIM]
    lane = lax.broadcasted_iota(I32, tail.shape, 1)
    kd_ref[...] = jnp.where(lane < HEAD_DIM, tail, pltpu.roll(tail, HEAD_DIM, 1)).astype(BF16)
    wit_ref[...] = tail.T[HEAD_DIM:HEAD_DIM + IDX_HEADS, :] * (IDX_HEADS ** -0.5)
    tm = tail.shape[0]
    row = pl.program_id(0) * tm + lax.broadcasted_iota(I32, (tm, 1), 0)
    pos = _pos_lanes(pos0 + (row & (period - 1)), lane)
    for pair in range(nkv // LANES):
        for half in range(2):
            tile = slice((2 * pair + half) * LANES, (2 * pair + half + 1) * LANES)
            vp = v[:, pair * LANES:(pair + 1) * LANES]
            kp = k[:, pair * LANES:(pair + 1) * LANES]
            if half:
                vp, kp = pltpu.roll(vp, HEAD_DIM, 1), pltpu.roll(kp, HEAD_DIM, 1)
            vt_ref[0, tile, :] = jnp.where(lane < HEAD_DIM, vp, 1.0).T.astype(BF16)
            ka_ref[:, tile] = jnp.where(lane < HEAD_DIM, kp, pos).astype(BF16)


def _proj_b(x, w_pad, nq, nkv, nqi, tm, pos0, period):
    m, d = x.shape
    assert period & (period - 1) == 0
    rows = lambda c: pl.BlockSpec((tm, c), lambda i: (i, 0))
    outs = [(nqi, BF16), (2 * nkv, BF16), (nkv, F32), (nkv, F32), (HEAD_DIM, F32), (LANES, BF16)]
    n_heads = nq // HEAD_DIM
    return pl.pallas_call(
        functools.partial(_proj_b_kernel, nq=nq, nkv=nkv, nqi=nqi, pos0=pos0, period=period),
        grid=(m // tm,),
        in_specs=[rows(d), _resident(w_pad.shape)],
        out_specs=[pl.BlockSpec((n_heads, LANES, tm), lambda i: (0, 0, i))] + [rows(c) for c, _ in outs]
        + [pl.BlockSpec((IDX_HEADS, tm), lambda i: (0, i)), pl.BlockSpec((1, 2 * nkv, tm), lambda i: (i, 0, 0))],
        out_shape=[jax.ShapeDtypeStruct((n_heads, LANES, m), BF16)] + [jax.ShapeDtypeStruct((m, c), dt) for c, dt in outs]
        + [jax.ShapeDtypeStruct((IDX_HEADS, m), F32), jax.ShapeDtypeStruct((m // tm, 2 * nkv, tm), BF16)],
        compiler_params=_params(1),
        name="proj_b",
    )(x, w_pad)


def _sparse_kernel(x_ref, qt_ref, qi_ref, wit_ref, ka_ref, vt_ref, kd_ref, wo_ref, g_ref, b_ref, o_ref,
                   key_scr, hi_scr, lo_scr, bias_scr, s_scr, acc_scr, m_scr, ot_scr,
                   *, tq, kb, pos0, n_keys, n_sel, n_heads, group, slopes, alpha):
    t = pl.program_id(1)
    q0 = pos0 + t * tq
    qpos_row = q0 + lax.broadcasted_iota(I32, (1, tq), 1)
    lim_row = jnp.minimum((lax.shift_right_logical(qpos_row, CHUNK_BITS) + 1) * CHUNK, n_keys)
    kmax = jnp.minimum((lax.shift_right_logical(q0 + tq - 1, CHUNK_BITS) + 1) * CHUNK, n_keys)
    nkb = lax.shift_right_logical(kmax + kb - 1, int(math.log2(kb)))
    lane = lax.broadcasted_iota(I32, (tq, LANES), 1)
    lo_half = lane < HEAD_DIM

    def key_block(kbi):
        return pl.multiple_of(kbi * kb, kb)

    def key_index(off):
        return off + lax.broadcasted_iota(I32, (kb, tq), 0)

    qis = []
    for p in range(IDX_HEADS // 2):
        q2 = qi_ref[0, :, p * LANES:(p + 1) * LANES]
        qis.append(jnp.where(lo_half, q2, jnp.zeros_like(q2)))
        qis.append(jnp.where(lo_half, jnp.zeros_like(q2), q2))
    wis = wit_ref[...]
    qi_all = jnp.concatenate(qis, axis=0)

    def score_block(kbi, carry, masked):
        off = key_block(kbi)
        dots = _dot_nt(kd_ref[0, pl.ds(off, kb), :], qi_all)
        acc = jnp.zeros((kb, tq), F32)
        for h in range(IDX_HEADS):
            acc = acc + jnp.maximum(dots[:, h * tq:(h + 1) * tq], 0.0) * wis[h:h + 1, :]
        if masked:
            acc = jnp.where(key_index(off) < lim_row, acc, -jnp.inf)
        bits = lax.bitcast_convert_type(acc, I32)
        key = bits ^ (lax.shift_right_arithmetic(bits, 31) & 0x7FFFFFFF)
        key_scr[pl.ds(off, kb), :] = key
        hi_scr[pl.ds(off, kb), :] = lax.shift_right_arithmetic(key, HALF_BITS).astype(I16)
        lo_scr[pl.ds(off, kb), :] = ((key & ((1 << HALF_BITS) - 1)) + I16_MIN).astype(I16)
        return carry

    lim_min = jnp.minimum((lax.shift_right_logical(q0, CHUNK_BITS) + 1) * CHUNK, n_keys)
    n_open = lax.shift_right_logical(lim_min, int(math.log2(kb)))
    lax.fori_loop(0, n_open, functools.partial(score_block, masked=False), 0)
    lax.fori_loop(n_open, nkb, functools.partial(score_block, masked=True), 0)

    @pl.when(nkb % 2 == 1)
    def _():
        key_scr[pl.ds(key_block(nkb), kb), :] = jnp.full((kb, tq), INT_MIN, I32)
        hi_scr[pl.ds(key_block(nkb), kb), :] = jnp.full((kb, tq), I16_MIN, I16)
        lo_scr[pl.ds(key_block(nkb), kb), :] = jnp.full((kb, tq), I16_MIN, I16)

    n_steps = lax.shift_right_logical(nkb + 1, 1)

    def step_rows(i):
        return pl.ds(pl.multiple_of(i * 2 * kb, 2 * kb), 2 * kb)

    def count(pred):
        def body(i, part):
            hit = pred(key_scr[step_rows(i), :])
            return part + jnp.sum(jnp.where(hit, 1, 0).reshape(-1, COUNT_ROWS, tq), axis=0)
        part = lax.fori_loop(0, n_steps, body, jnp.zeros((COUNT_ROWS, tq), I32))
        return jnp.sum(part, axis=0, keepdims=True)

    def count16(ref, pred):
        def body(i, parts):
            ind = jnp.where(pred(ref[pl.ds(key_block(i), kb), :]), jnp.int16(1), jnp.int16(0))
            parts = list(parts)
            for j, r in enumerate(range(0, kb, COUNT16_ROWS)):
                parts[j % len(parts)] = parts[j % len(parts)] + ind[r:r + COUNT16_ROWS]
            return tuple(parts)
        parts = lax.fori_loop(0, 2 * n_steps, body, (jnp.zeros((COUNT16_ROWS, tq), I16),) * COUNT16_CHAINS)
        return jnp.sum(sum(p.astype(I32) for p in parts), axis=0, keepdims=True)

    def as_i16(row):
        return jnp.broadcast_to(row, (COUNT16_ROWS, tq)).astype(I16)[:1]

    def bisect16(ref, want):
        def body(i, thr):
            cand = thr + lax.shift_left(jnp.int32(1), HALF_BITS - 1 - i)
            cand16 = as_i16(cand)
            return jnp.where(count16(ref, lambda v: v >= cand16) >= want, cand, thr)
        return lax.fori_loop(0, HALF_BITS, body, jnp.full((1, tq), I16_MIN, I32))

    thr_hi = bisect16(hi_scr, n_sel)
    thr_hi16 = as_i16(thr_hi)
    n_hi_gt = count16(hi_scr, lambda v: v > thr_hi16)

    def keep_bucket(i, carry):
        rows = step_rows(i)
        lo_scr[rows, :] = jnp.where(hi_scr[rows, :] == thr_hi16, lo_scr[rows, :], jnp.int16(I16_MIN))
        return carry

    lax.fori_loop(0, n_steps, keep_bucket, 0)
    thr_lo = bisect16(lo_scr, n_sel - n_hi_gt)
    thr = thr_hi * (1 << HALF_BITS) + (thr_lo - I16_MIN)
    thr_lo16 = as_i16(thr_lo)
    n_gt = n_hi_gt + count16(lo_scr, lambda v: v > thr_lo16)
    n_eq = count(lambda key: key == thr)
    need = n_sel - n_gt
    tie = jnp.max(jnp.where((n_eq > need) & (thr > NEG_INF_KEY), 1, 0)) > 0

    @pl.when(jnp.logical_not(tie))
    def _():
        def body(kbi, carry):
            off = key_block(kbi)
            sel = (key_scr[pl.ds(off, kb), :] >= thr) & (key_index(off) < lim_row)
            bias_scr[pl.ds(off, kb), :] = jnp.where(sel, 0.0, -MASK_BIAS)
            return carry
        lax.fori_loop(0, nkb, body, 0)

    @pl.when(tie)
    def _():
        tri = jnp.where(lax.broadcasted_iota(I32, (kb, kb), 0) > lax.broadcasted_iota(I32, (kb, kb), 1),
                        1.0, 0.0).astype(BF16)
        need_f = need.astype(F32)

        def body(kbi, seen):
            off = key_block(kbi)
            key = key_scr[pl.ds(off, kb), :]
            eq = jnp.where(key == thr, 1.0, 0.0)
            earlier = _dot(tri, eq.astype(BF16)) + seen
            sel = ((key > thr) | ((key == thr) & (earlier < need_f))) & (key_index(off) < lim_row)
            bias_scr[pl.ds(off, kb), :] = jnp.where(sel, 0.0, -MASK_BIAS)
            return seen + jnp.sum(eq, axis=0, keepdims=True)
        lax.fori_loop(0, nkb, body, jnp.zeros((1, tq), F32))

    unit_heads = s_scr.shape[2] // tq
    n_units = n_heads // unit_heads
    kv_lanes = lambda u: slice((u * unit_heads // group) * LANES, (u * unit_heads // group + 1) * LANES)
    m_scr[...] = jnp.full(m_scr.shape, -jnp.inf, F32)
    acc_scr[...] = jnp.zeros(acc_scr.shape, F32)

    def scores(kbi, u, last):
        off = key_block(kbi)
        bias = jnp.concatenate([bias_scr[pl.ds(off, kb), :]] * unit_heads, axis=1)
        q_t = jnp.concatenate([qt_ref[u * unit_heads + j] for j in range(unit_heads)], axis=1)
        s = _dot(ka_ref[0, pl.ds(off, kb), kv_lanes(u)], q_t) + bias
        if last:
            ahead = jnp.maximum(key_index(off) - qpos_row, 0).astype(F32)
            s = s - jnp.concatenate([(2.0 * slopes[u * unit_heads + j] * LOG2E) * ahead for j in range(unit_heads)], axis=1)
        s_scr[u] = s

    def absorb(kbi, u):
        s = s_scr[u]
        m_old = m_scr[u]
        m_new = jnp.maximum(m_old, jnp.max(s, axis=0, keepdims=True))
        p = jnp.exp2((s - m_new).astype(BF16))
        acc_scr[u] = acc_scr[u] * jnp.exp2(m_old - m_new) + _dot(vt_ref[0, kbi, kv_lanes(u), :], p)
        m_scr[u] = m_new

    def step(kbi, last):
        for u in range(n_units):
            absorb(kbi - 1, u)
            scores(kbi, u, last)

    def first_scores(last):
        for u in range(n_units):
            scores(0, u, last)

    def step_body(kbi, carry):
        step(kbi, False)
        return carry

    pl.when(nkb == 1)(lambda: first_scores(True))
    pl.when(nkb > 1)(lambda: first_scores(False))
    lax.fori_loop(1, nkb - 1, step_body, 0)
    pl.when(nkb > 1)(lambda: step(nkb - 1, True))
    for u in range(n_units):
        absorb(nkb - 1, u)

    for u in range(n_units):
        acc = acc_scr[u]
        o_t = acc[:HEAD_DIM] / acc[HEAD_DIM:HEAD_DIM + 1]
        for j in range(unit_heads):
            h = u * unit_heads + j
            ot_scr[h * HEAD_DIM:(h + 1) * HEAD_DIM, :] = o_t[:, j * tq:(j + 1) * tq].astype(BF16)
    y = lax.dot_general(ot_scr[...], wo_ref[...], (((0,), (0,)), ((), ())), preferred_element_type=F32)
    o_ref[0] = _layer_norm(alpha * x_ref[0] + y, g_ref[...], b_ref[...])


def _sparse_attn(x, qt, qi, wit, k, vt, kd, wo, g, b, *, tq, kb, pos0, n_keys, alpha):
    bsz, s, d = x.shape
    n_t = s // tq
    l_pad = k.shape[1]
    n_heads = qt.shape[0]
    n_kv = k.shape[2] // LANES
    group = n_heads // n_kv
    assert kb % tq == 0 and pos0 % tq == 0 and l_pad % kb == 0 and vt.shape == (bsz, l_pad // kb, k.shape[2], kb)
    slopes = _alibi_slopes(n_heads)
    qrow = lambda c: pl.BlockSpec((1, tq, c), lambda bi, t: (bi, t, 0))
    key_mode = pl.Buffered(2)
    keys = lambda c: pl.BlockSpec((1, l_pad, c), lambda bi, t: (bi, 0, 0), pipeline_mode=key_mode)
    const2 = lambda bi, t: (0, 0)
    sel_rows = -(-l_pad // (2 * kb)) * 2 * kb
    unit_heads = min(max(UNIT_LANES // tq, 1), group)
    assert group % unit_heads == 0
    n_units, unit_w = n_heads // unit_heads, unit_heads * tq
    kern = functools.partial(_sparse_kernel, tq=tq, kb=kb, pos0=pos0, n_keys=n_keys, n_sel=min(TOPK_MAX, n_keys // 4),
                             n_heads=n_heads, group=group, slopes=slopes, alpha=alpha)
    return pl.pallas_call(
        kern,
        grid=(bsz, n_t),
        in_specs=[qrow(d), pl.BlockSpec((n_heads, LANES, tq), lambda bi, t: (0, 0, bi * n_t + t)), qrow(qi.shape[2]),
                  pl.BlockSpec((IDX_HEADS, tq), lambda bi, t: (0, bi * n_t + t)),
                  keys(k.shape[2]),
                  pl.BlockSpec((1,) + vt.shape[1:], lambda bi, t: (bi, 0, 0, 0), pipeline_mode=key_mode),
                  keys(kd.shape[2]),
                  _resident(wo.shape), pl.BlockSpec((1, d), const2), pl.BlockSpec((1, d), const2)],
        out_specs=qrow(d),
        out_shape=jax.ShapeDtypeStruct((bsz, s, d), F32),
        scratch_shapes=[pltpu.VMEM((sel_rows, tq), I32), pltpu.VMEM((sel_rows, tq), I16), pltpu.VMEM((sel_rows, tq), I16),
                        pltpu.VMEM((l_pad, tq), F32),
                        pltpu.VMEM((n_units, kb, unit_w), F32), pltpu.VMEM((n_units, LANES, unit_w), F32),
                        pltpu.VMEM((n_units, 1, unit_w), F32), pltpu.VMEM((n_heads * HEAD_DIM, tq), BF16)],
        compiler_params=_params(2),
        name="sparse_attn",
    )(x, qt, qi, wit, k, vt, kd, wo, g, b)


def _pad_rows(a, n):
    return jnp.pad(a, ((0, 0), (0, n - a.shape[1]), (0, 0)))


def _mixer_a(xp, xs, cache_k, cache_v, w_in, rel_bias, w_out, g, b, alpha):
    bsz, s, d = xp.shape
    dbs, t, _ = xs.shape
    width = w_out.shape[0]
    w_in = w_in.astype(BF16)
    w_out = w_out.astype(BF16)
    bias_p, bias_s = _rel_bias_tables(rel_bias, t)
    keep = min(A_PAST, s)
    q, k, vt, kf, vf = _proj_a(xp, w_in, keep)
    yp = _attn_a_prompt(xp, q, k, vt, bias_p, w_out, g, b, alpha)
    qs, ks, _, ksf, vsf = _proj_a(xs.reshape(1, dbs * t, d), w_in, dbs * t)
    n_cache = cache_k.shape[1]
    kwin = _pad_rows(jnp.concatenate([cache_k.reshape(dbs, n_cache, width).astype(BF16), ks.reshape(dbs, t, width)], 1), A_WIN)
    vwin = _pad_rows(jnp.concatenate([cache_v.reshape(dbs, n_cache, width), vsf.reshape(dbs, t, width)], 1).astype(BF16), A_WIN)
    ys = _attn_a_sample(_pad_rows(xs, A_SUB), _pad_rows(qs.reshape(dbs, t, width), A_SUB), kwin, vwin.transpose(0, 2, 1),
                        bias_s, w_out, g, b, alpha)[:, :t]
    heads = width // HEAD_DIM
    return (yp, ys, kf.reshape(bsz, keep, heads, HEAD_DIM), vf.reshape(bsz, keep, heads, HEAD_DIM),
            ksf.reshape(dbs, t, heads, HEAD_DIM), vsf.reshape(dbs, t, heads, HEAD_DIM))


def _mixer_b(xp, xs, cache_k, cache_v, cache_idx, w_in, w_out, g, b, alpha):
    bsz, s, d = xp.shape
    dbs, t, _ = xs.shape
    nq = w_out.shape[0]
    nkv = cache_k.shape[2] * cache_k.shape[3]
    nqi = IDX_HEADS * HEAD_DIM
    kvh = nkv // HEAD_DIM
    past = cache_k.shape[1]
    w_pad = jnp.pad(w_in, ((0, 0), (0, nq + 2 * nkv + nqi + LANES - w_in.shape[1]))).astype(BF16)
    w_out = w_out.astype(BF16)
    kb = PROJ_ROWS

    qt, qi, k, kf, vf, kif, kd, wit, vt = _proj_b(xp.reshape(bsz * s, d), w_pad, nq, nkv, nqi, kb, 0, s)
    r3 = lambda a: a.reshape(bsz, s, a.shape[1])
    yp = _sparse_attn(xp, qt, r3(qi), wit, r3(k), vt.reshape(bsz, s // kb, 2 * nkv, kb), r3(kd), w_out, g, b,
                      tq=B_QUERY_TILE, kb=kb, pos0=0, n_keys=s, alpha=alpha)

    qts, qis, ks, ksf, vsf, kisf, kds, wits, _ = _proj_b(xs.reshape(dbs * t, d), w_pad, nq, nkv, nqi, dbs * t, past, t)
    tq_s = B_SAMPLE_TILE
    n_keys = past + t
    l_pad = -(-n_keys // kb) * kb
    s3 = lambda a: a.reshape(dbs, t, a.shape[1])
    v_all = jnp.concatenate([cache_v, vsf.reshape(dbs, t, kvh, HEAD_DIM)], axis=1).astype(BF16)
    v_all = jnp.concatenate([v_all, jnp.ones_like(v_all)], axis=-1).reshape(dbs, n_keys, 2 * nkv)
    keys_vt = _pad_rows(v_all, l_pad).reshape(dbs, l_pad // kb, kb, 2 * nkv).transpose(0, 1, 3, 2)
    kd_cache = jnp.concatenate([cache_idx, cache_idx], axis=-1).astype(BF16)
    frame = jnp.arange(past, dtype=I32)[:, None]
    pos = _pos_lanes(frame, HEAD_DIM + jnp.arange(HEAD_DIM, dtype=I32)[None, :]).astype(BF16)
    pos = jnp.broadcast_to(pos[None, :, None, :], (dbs, past, kvh, HEAD_DIM))
    k_cache = jnp.concatenate([cache_k.astype(BF16), pos], axis=-1).reshape(dbs, past, 2 * nkv)
    keys_k = _pad_rows(jnp.concatenate([k_cache, s3(ks)], 1), l_pad)
    keys_kd = _pad_rows(jnp.concatenate([kd_cache, s3(kds)], 1), l_pad)
    wit_pad = jnp.pad(wits.reshape(IDX_HEADS, dbs, t), ((0, 0), (0, 0), (0, tq_s - t)), mode="edge").reshape(IDX_HEADS, dbs * tq_s)
    qis_pad = jnp.pad(s3(qis), ((0, 0), (0, tq_s - t), (0, 0)), mode="edge")
    qts_pad = jnp.pad(qts.reshape(-1, LANES, dbs, t), ((0, 0), (0, 0), (0, 0), (0, tq_s - t))).reshape(-1, LANES, dbs * tq_s)
    ys = _sparse_attn(_pad_rows(xs, tq_s), qts_pad, qis_pad, wit_pad,
                      keys_k, keys_vt, keys_kd, w_out, g, b,
                      tq=tq_s, kb=kb, pos0=past, n_keys=n_keys, alpha=alpha)[:, :t]
    return (yp, ys,
            kf.reshape(bsz, s, kvh, HEAD_DIM), vf.reshape(bsz, s, kvh, HEAD_DIM), kif.reshape(bsz, s, HEAD_DIM),
            ksf.reshape(dbs, t, kvh, HEAD_DIM), vsf.reshape(dbs, t, kvh, HEAD_DIM), kisf.reshape(dbs, t, HEAD_DIM))


def kernel(x_prompt, x_sample, cache_a_k, cache_a_v, cache_b_k, cache_b_v, cache_b_idx, ln_g, ln_b, ffn_w_gate, ffn_w_up, ffn_w_down, a_w_in, a_rel_bias, a_w_out, b_w_in, b_w_out):
    depth = ln_g.shape[0]
    alpha = (2.0 * depth) ** 0.25
    bsz, s, d = x_prompt.shape
    dbs, t, _ = x_sample.shape
    xp, xs = x_prompt, x_sample
    a_out, b_out = [], []

    wg_all, wu_all, wd_all = ffn_w_gate.astype(BF16), ffn_w_up.astype(BF16), ffn_w_down.astype(BF16)

    def ffn(x, layer, i, tm):
        shp = x.shape
        y = _ffn_block(x.reshape(-1, d), wg_all[layer, i], wu_all[layer, i], wd_all[layer, i],
                       ln_g[layer, 2 * i][None], ln_b[layer, 2 * i][None], alpha, tm)
        return y.reshape(shp)

    for layer in range(depth):
        j = layer // 2
        xp = ffn(xp, layer, 0, FFN_ROWS)
        xs = ffn(xs, layer, 0, dbs * t)
        g, b = ln_g[layer, 1][None], ln_b[layer, 1][None]
        if layer % 2 == 0:
            xp, xs, *rest = _mixer_a(xp, xs, cache_a_k[j], cache_a_v[j], a_w_in[j], a_rel_bias[j], a_w_out[j], g, b, alpha)
            a_out.append(rest)
        else:
            xp, xs, *rest = _mixer_b(xp, xs, cache_b_k[j], cache_b_v[j], cache_b_idx[j], b_w_in[j], b_w_out[j], g, b, alpha)
            b_out.append(rest)
        xp = ffn(xp, layer, 1, FFN_ROWS)
        xs = ffn(xs, layer, 1, dbs * t)

    stack = lambda outs, i: jnp.stack([o[i] for o in outs], 0)
    return (xp, xs,
            stack(a_out, 0), stack(a_out, 1), stack(a_out, 2), stack(a_out, 3),
            stack(b_out, 0), stack(b_out, 1), stack(b_out, 2), stack(b_out, 3), stack(b_out, 4), stack(b_out, 5))
```

```python
import functools
import math
import struct

import jax
import jax.numpy as jnp
from jax import lax
from jax.experimental import pallas as pl
from jax.experimental.pallas import tpu as pltpu

F32 = jnp.float32
BF16 = jnp.bfloat16
I32 = jnp.int32
I16 = jnp.int16

CHUNK = 64
CHUNK_BITS = 6
HALF_BITS = 16
FFN_ROWS = 1024
PROJ_ROWS = 512
B_QUERY_TILE = 256
B_SAMPLE_TILE = 128
A_PAST = 512
REL_CLIP = 128
HEAD_DIM = 64
LANES = 128
IDX_HEADS = 8
TOPK_MAX = 256
LN_EPS = 1e-5
A_SUB = 4 * CHUNK
A_WIN = A_PAST + A_SUB
VMEM_LIMIT = 52 * 1024 * 1024

LOG2E = 1.4426950408889634
POS_LANE0 = HEAD_DIM
MASK_BIAS = 1e30
ONES_ROWS = 16
FFN_CHUNK = 768
UNIT_LANES = 512
COUNT_ROWS = 16

assert 1 << CHUNK_BITS == CHUNK

NT_DIMS = (((1,), (1,)), ((), ()))
INT_MIN = -2147483648
I16_MIN = -32768
COUNT16_ROWS = 16
COUNT16_CHAINS = 4
NEG_INF_KEY = -2139095041


def _params(n_grid, flags=None):
    return pltpu.CompilerParams(dimension_semantics=("arbitrary",) * n_grid,
                                vmem_limit_bytes=VMEM_LIMIT, flags=flags)


def _resident(shape):
    zeros = (0,) * len(shape)
    return pl.BlockSpec(shape, lambda *_: zeros, pipeline_mode=pl.Buffered(1))


def _layer_norm(z, g, b):
    mu = jnp.mean(z, axis=-1, keepdims=True)
    d = z - mu
    var = jnp.mean(d * d, axis=-1, keepdims=True)
    return d * lax.rsqrt(var + LN_EPS) * g + b


def _dot(a, b):
    return jnp.dot(a, b, preferred_element_type=F32)


def _dot_nt(a, b):
    return lax.dot_general(a, b, NT_DIMS, preferred_element_type=F32)


def _ffn_kernel(x_ref, wg_ref, wu_ref, wd_ref, g_ref, b_ref, o_ref, h_scr, *, alpha):
    x = x_ref[...]
    xb = x.astype(BF16)
    f = wg_ref.shape[1]
    for c0 in range(0, f, FFN_CHUNK):
        sl = slice(c0, min(c0 + FFN_CHUNK, f))
        gate = _dot(xb, wg_ref[:, sl])
        up = _dot(xb, wu_ref[:, sl])
        h_scr[:, sl] = (gate * (1.0 / (1.0 + jnp.exp(-gate))) * up).astype(BF16)
    y = _dot(h_scr[...], wd_ref[...])
    o_ref[...] = _layer_norm(alpha * x + 0.5 * y, g_ref[...], b_ref[...])


def _ffn_block(x, wg, wu, wd, g, b, alpha, tm):
    m, d = x.shape
    f = wg.shape[1]
    const = lambda i: (0, 0)
    return pl.pallas_call(
        functools.partial(_ffn_kernel, alpha=alpha),
        grid=(m // tm,),
        in_specs=[pl.BlockSpec((tm, d), lambda i: (i, 0)),
                  _resident((d, f)), _resident((d, f)), _resident((f, d)),
                  pl.BlockSpec((1, d), const), pl.BlockSpec((1, d), const)],
        out_specs=pl.BlockSpec((tm, d), lambda i: (i, 0)),
        out_shape=jax.ShapeDtypeStruct((m, d), F32),
        scratch_shapes=[pltpu.VMEM((tm, f), BF16)],
        compiler_params=_params(1),
        name="ffn_ln",
    )(x, wg, wu, wd, g, b)


def _proj_a_kernel(x_ref, w_ref, q_ref, k_ref, vt_ref, kf_ref, vf_ref, *, width):
    h = _dot(x_ref[0].astype(BF16), w_ref[...])
    k = h[:, width:2 * width]
    v = h[:, 2 * width:]
    q_ref[0] = (h[:, :width] * (HEAD_DIM ** -0.5 * LOG2E)).astype(BF16)
    k_ref[0] = k.astype(BF16)
    vt_ref[0] = v.T.astype(BF16)

    @pl.when(pl.program_id(1) == pl.num_programs(1) - 1)
    def _():
        kf_ref[0] = k
        vf_ref[0] = v


def _proj_a(x, w, tm):
    bsz, s, d = x.shape
    width = w.shape[1] // 3
    row = pl.BlockSpec((1, tm, width), lambda b, t: (b, t, 0))
    last = pl.BlockSpec((1, tm, width), lambda b, t: (b, 0, 0))
    return pl.pallas_call(
        functools.partial(_proj_a_kernel, width=width),
        grid=(bsz, s // tm),
        in_specs=[pl.BlockSpec((1, tm, d), lambda b, t: (b, t, 0)), _resident(w.shape)],
        out_specs=[row, row, pl.BlockSpec((1, width, tm), lambda b, t: (b, 0, t)), last, last],
        out_shape=[jax.ShapeDtypeStruct((bsz, s, width), BF16)] * 2 + [jax.ShapeDtypeStruct((bsz, width, s), BF16)]
        + [jax.ShapeDtypeStruct((bsz, tm, width), F32)] * 2,
        compiler_params=_params(2),
        name="proj_a",
    )(x, w)


def _attn_a_kernel(*refs, n_sub, prompt, alpha):
    if prompt:
        (x_ref, q_ref, kp_ref, kc_ref, vtp_ref, vtc_ref, bias_ref, wo_ref, g_ref, b_ref,
         o_ref, kwin, q_scr, s_scr, ot_scr, vtwin) = refs
        vtwin[:, 0:A_PAST] = vtp_ref[0]
        vtwin[:, A_PAST:2 * A_PAST] = vtc_ref[0]
        vt_at = lambda rows, r0: vtwin[rows, r0:r0 + A_WIN]
    else:
        x_ref, q_ref, k_ref, vt_ref, bias_ref, wo_ref, g_ref, b_ref, o_ref, kwin, q_scr, s_scr, ot_scr = refs
        vt_at = lambda rows, r0: vt_ref[0, rows, r0:r0 + A_WIN]
    n_pairs = q_ref.shape[2] // LANES
    for p in range(n_pairs):
        cols = slice(p * LANES, (p + 1) * LANES)
        q_scr[p] = q_ref[0, :, cols]
        if prompt:
            kwin[p, 0:A_PAST] = kp_ref[0, :, cols]
            kwin[p, A_PAST:2 * A_PAST] = kc_ref[0, :, cols]
        else:
            kwin[p] = k_ref[0, :, cols]
    lane = lax.broadcasted_iota(I32, (A_SUB, LANES), 1)
    key_row = lax.broadcasted_iota(I32, (A_WIN, 2 * A_SUB), 0)
    out_row = lax.broadcasted_iota(I32, (LANES, A_SUB), 0)
    ones_rows = jnp.ones((ONES_ROWS, A_WIN), BF16)

    def scores(p, j, side, first_valid):
        r0 = j * A_SUB
        q2 = q_scr[p, r0:r0 + A_SUB, :].astype(F32)
        qt = jnp.concatenate([jnp.where(lane < HEAD_DIM, q2, 0.0).T, jnp.where(lane < HEAD_DIM, 0.0, q2).T], axis=1)
        s = _dot(kwin[p, r0:r0 + A_WIN, :], qt.astype(BF16)) + bias_ref[p]
        if first_valid > r0:
            s = jnp.where(key_row >= first_valid - r0, s, -jnp.inf)
        s_scr[side, j] = s

    def absorb(p, j, side):
        r0 = j * A_SUB
        s = s_scr[side, j]
        e = jnp.exp2((s - jnp.max(s, axis=0, keepdims=True)).astype(BF16))
        rows = pl.ds(p * LANES, LANES)
        o = _dot(jnp.concatenate([vt_at(rows, r0), ones_rows], axis=0), e)
        o = o[:LANES] / o[LANES:LANES + 1]
        ot_scr[rows, r0:r0 + A_SUB] = jnp.where(out_row < HEAD_DIM, o[:, :A_SUB], o[:, A_SUB:]).astype(BF16)

    def overlap(p_scores, p_absorb, side, first_valid):
        for j in range(n_sub):
            scores(p_scores, j, side, first_valid)
        for j in range(n_sub):
            absorb(p_absorb, j, 1 - side)

    def attend(first_valid):
        assert n_pairs % 2 == 0
        for j in range(n_sub):
            scores(0, j, 0, first_valid)

        def two_pairs(i, carry):
            overlap(2 * i + 1, 2 * i, 1, first_valid)
            overlap(2 * i + 2, 2 * i + 1, 0, first_valid)
            return carry

        for i in range(n_pairs // 2 - 1):
            two_pairs(i, 0)
        overlap(n_pairs - 1, n_pairs - 2, 1, first_valid)
        for j in range(n_sub):
            absorb(n_pairs - 1, j, 1)

    if prompt:
        pl.when(pl.program_id(1) == 0)(lambda: attend(A_PAST))
        pl.when(pl.program_id(1) > 0)(lambda: attend(0))
    else:
        attend(0)

    y = lax.dot_general(ot_scr[...], wo_ref[...], (((0,), (0,)), ((), ())), preferred_element_type=F32)
    o_ref[0] = _layer_norm(alpha * x_ref[0] + y, g_ref[...], b_ref[...])


def _attn_a_scratch(width, n_keys, n_queries):
    pairs = width // LANES
    return [pltpu.VMEM((pairs, n_keys, LANES), BF16), pltpu.VMEM((pairs, n_queries, LANES), BF16),
            pltpu.VMEM((2, n_queries // A_SUB, A_WIN, 2 * A_SUB), F32), pltpu.VMEM((width, n_queries), BF16)]


def _attn_a_prompt(x, q, k, vt, bias, wo, g, b, alpha):
    bsz, s, d = x.shape
    width = q.shape[2]
    tq = A_PAST
    cur = lambda bi, t: (bi, t, 0)
    prev = lambda bi, t: (bi, jnp.maximum(t - 1, 0), 0)
    cur_t = lambda bi, t: (bi, 0, t)
    prev_t = lambda bi, t: (bi, 0, jnp.maximum(t - 1, 0))
    const2 = lambda bi, t: (0, 0)
    return pl.pallas_call(
        functools.partial(_attn_a_kernel, n_sub=tq // A_SUB, prompt=True, alpha=alpha),
        grid=(bsz, s // tq),
        in_specs=[pl.BlockSpec((1, tq, d), cur), pl.BlockSpec((1, tq, width), cur),
                  pl.BlockSpec((1, tq, width), prev), pl.BlockSpec((1, tq, width), cur),
                  pl.BlockSpec((1, width, tq), prev_t), pl.BlockSpec((1, width, tq), cur_t),
                  _resident(bias.shape), _resident(wo.shape),
                  pl.BlockSpec((1, d), const2), pl.BlockSpec((1, d), const2)],
        out_specs=pl.BlockSpec((1, tq, d), cur),
        out_shape=jax.ShapeDtypeStruct((bsz, s, d), F32),
        scratch_shapes=_attn_a_scratch(width, 2 * tq, tq) + [pltpu.VMEM((width, 2 * tq), BF16)],
        compiler_params=_params(2),
        name="attn_a_prompt",
    )(x, q, k, k, vt, vt, bias, wo, g, b)


def _attn_a_sample(x, q, kwin, vtwin, bias, wo, g, b, alpha):
    bsz, _, d = x.shape
    width = q.shape[2]
    blk = lambda n, c: pl.BlockSpec((1, n, c), lambda bi: (bi, 0, 0))
    const2 = lambda bi: (0, 0)
    return pl.pallas_call(
        functools.partial(_attn_a_kernel, n_sub=1, prompt=False, alpha=alpha),
        grid=(bsz,),
        in_specs=[blk(A_SUB, d), blk(A_SUB, width), blk(A_WIN, width), blk(width, A_WIN),
                  _resident(bias.shape), _resident(wo.shape),
                  pl.BlockSpec((1, d), const2), pl.BlockSpec((1, d), const2)],
        out_specs=blk(A_SUB, d),
        out_shape=jax.ShapeDtypeStruct((bsz, A_SUB, d), F32),
        scratch_shapes=_attn_a_scratch(width, A_WIN, A_SUB),
        compiler_params=_params(1),
        name="attn_a_sample",
    )(x, q, kwin, vtwin, bias, wo, g, b)


def _rel_bias_tables(rel_bias, n_sample):
    r = jnp.arange(A_SUB)[:, None]
    c = jnp.arange(A_WIN)[None, :]
    period = A_SUB + A_WIN
    diff = jnp.arange(period)
    diff = jnp.where(diff < A_WIN, diff, diff - period)
    line = rel_bias[:, jnp.clip(A_PAST - diff, -REL_CLIP, REL_CLIP) + REL_CLIP].astype(F32) * LOG2E
    table = jnp.tile(line, (1, A_SUB))[:, :A_SUB * (period - 1)].reshape(-1, A_SUB, period - 1)[:, :, :A_WIN]
    lo = (r // CHUNK) * CHUNK
    band = (c >= lo) & (c < lo + A_PAST + CHUNK)
    prompt = jnp.where(band[None], table, -jnp.inf)
    live = (c < A_PAST + n_sample)
    sample = jnp.where(live[None], jnp.where((r < n_sample)[None], table, 0.0), -jnp.inf)

    def pair_layout(tab):
        h = tab.shape[0]
        return tab.transpose(0, 2, 1).reshape(h // 2, 2, A_WIN, A_SUB).transpose(0, 2, 1, 3).reshape(h // 2, A_WIN, 2 * A_SUB)

    return pair_layout(prompt), pair_layout(sample)


def _round_to_bf16(x):
    bits = struct.unpack("<I", struct.pack("<f", x))[0]
    bits = (bits + 0x7FFF + ((bits >> 16) & 1)) & 0xFFFF0000
    return struct.unpack("<f", struct.pack("<I", bits))[0]


def _bf16_pieces(x):
    p1 = _round_to_bf16(x)
    p2 = _round_to_bf16(x - p1)
    return (p1, p2, _round_to_bf16(x - p1 - p2))


def _pos_lanes(pos, lane):
    hi = (lax.shift_right_logical(pos, CHUNK_BITS) * CHUNK).astype(F32)
    lo = (pos & (CHUNK - 1)).astype(F32)
    return jnp.where(lane < POS_LANE0 + 3, hi, jnp.where(lane < POS_LANE0 + 6, lo, 0.0))


def _alibi_slopes(n_heads):
    return tuple(2.0 ** (-8.0 * (h + 1) / n_heads) for h in range(n_heads))


def _proj_b_kernel(x_ref, w_ref, qt_ref, qi_ref, ka_ref, kf_ref, vf_ref, ki_ref, kd_ref, wit_ref, vt_ref,
                   *, nq, nkv, nqi, pos0, period):
    h = _dot(x_ref[...].astype(BF16), w_ref[...])
    n_heads = nq // HEAD_DIM
    slopes = _alibi_slopes(n_heads)
    lane_q = lax.broadcasted_iota(I32, (x_ref.shape[0], LANES), 1)
    for hd in range(n_heads):
        q2 = h[:, (hd // 2) * LANES:(hd // 2 + 1) * LANES] * (HEAD_DIM ** -0.5 * LOG2E)
        if hd % 2:
            q2 = pltpu.roll(q2, HEAD_DIM, 1)
        sl = jnp.zeros((1, LANES), F32)
        for i, piece in enumerate(_bf16_pieces(slopes[hd] * LOG2E) * 2):
            sl = jnp.where(lane_q[:1] == POS_LANE0 + i, piece, sl)
        qt_ref[hd] = jnp.where(lane_q < HEAD_DIM, q2, sl).T.astype(BF16)
    k = h[:, nq:nq + nkv]
    v = h[:, nq + nkv:nq + 2 * nkv]
    kf_ref[...] = k
    vf_ref[...] = v
    o0 = nq + 2 * nkv
    qi_ref[...] = (h[:, o0:o0 + nqi] * (HEAD_DIM ** -0.5)).astype(BF16)
    tail = h[:, o0 + nqi:o0 + nqi + LANES]
    ki_ref[...] = tail[:, :HEAD_DIM]
    lane = lax.broadcasted_iota(I32, tail.shape, 1)
    kd_ref[...] = jnp.where(lane < HEAD_DIM, tail, pltpu.roll(tail, HEAD_DIM, 1)).astype(BF16)
    wit_ref[...] = tail.T[HEAD_DIM:HEAD_DIM + IDX_HEADS, :] * (IDX_HEADS ** -0.5)
    tm = tail.shape[0]
    row = pl.program_id(0) * tm + lax.broadcasted_iota(I32, (tm, 1), 0)
    pos = _pos_lanes(pos0 + (row & (period - 1)), lane)
    for pair in range(nkv // LANES):
        for half in range(2):
            tile = slice((2 * pair + half) * LANES, (2 * pair + half + 1) * LANES)
            vp = v[:, pair * LANES:(pair + 1) * LANES]
            kp = k[:, pair * LANES:(pair + 1) * LANES]
            if half:
                vp, kp = pltpu.roll(vp, HEAD_DIM, 1), pltpu.roll(kp, HEAD_DIM, 1)
            vt_ref[0, tile, :] = jnp.where(lane < HEAD_DIM, vp, 1.0).T.astype(BF16)
            ka_ref[:, tile] = jnp.where(lane < HEAD_DIM, kp, pos).astype(BF16)


def _proj_b(x, w_pad, nq, nkv, nqi, tm, pos0, period):
    m, d = x.shape
    assert period & (period - 1) == 0
    rows = lambda c: pl.BlockSpec((tm, c), lambda i: (i, 0))
    outs = [(nqi, BF16), (2 * nkv, BF16), (nkv, F32), (nkv, F32), (HEAD_DIM, F32), (LANES, BF16)]
    n_heads = nq // HEAD_DIM
    return pl.pallas_call(
        functools.partial(_proj_b_kernel, nq=nq, nkv=nkv, nqi=nqi, pos0=pos0, period=period),
        grid=(m // tm,),
        in_specs=[rows(d), _resident(w_pad.shape)],
        out_specs=[pl.BlockSpec((n_heads, LANES, tm), lambda i: (0, 0, i))] + [rows(c) for c, _ in outs]
        + [pl.BlockSpec((IDX_HEADS, tm), lambda i: (0, i)), pl.BlockSpec((1, 2 * nkv, tm), lambda i: (i, 0, 0))],
        out_shape=[jax.ShapeDtypeStruct((n_heads, LANES, m), BF16)] + [jax.ShapeDtypeStruct((m, c), dt) for c, dt in outs]
        + [jax.ShapeDtypeStruct((IDX_HEADS, m), F32), jax.ShapeDtypeStruct((m // tm, 2 * nkv, tm), BF16)],
        compiler_params=_params(1),
        name="proj_b",
    )(x, w_pad)


def _sparse_kernel(x_ref, qt_ref, qi_ref, wit_ref, ka_ref, vt_ref, kd_ref, wo_ref, g_ref, b_ref, o_ref,
                   key_scr, hi_scr, lo_scr, bias_scr, s_scr, acc_scr, m_scr, ot_scr,
                   *, tq, kb, pos0, n_keys, n_sel, n_heads, group, slopes, alpha):
    t = pl.program_id(1)
    q0 = pos0 + t * tq
    qpos_row = q0 + lax.broadcasted_iota(I32, (1, tq), 1)
    lim_row = jnp.minimum((lax.shift_right_logical(qpos_row, CHUNK_BITS) + 1) * CHUNK, n_keys)
    kmax = jnp.minimum((lax.shift_right_logical(q0 + tq - 1, CHUNK_BITS) + 1) * CHUNK, n_keys)
    nkb = lax.shift_right_logical(kmax + kb - 1, int(math.log2(kb)))
    lane = lax.broadcasted_iota(I32, (tq, LANES), 1)
    lo_half = lane < HEAD_DIM

    def key_block(kbi):
        return pl.multiple_of(kbi * kb, kb)

    def key_index(off):
        return off + lax.broadcasted_iota(I32, (kb, tq), 0)

    qis = []
    for p in range(IDX_HEADS // 2):
        q2 = qi_ref[0, :, p * LANES:(p + 1) * LANES]
        qis.append(jnp.where(lo_half, q2, jnp.zeros_like(q2)))
        qis.append(jnp.where(lo_half, jnp.zeros_like(q2), q2))
    wis = wit_ref[...]
    qi_all = jnp.concatenate(qis, axis=0)

    def score_block(kbi, carry, masked):
        off = key_block(kbi)
        dots = _dot_nt(kd_ref[0, pl.ds(off, kb), :], qi_all)
        acc = jnp.zeros((kb, tq), F32)
        for h in range(IDX_HEADS):
            acc = acc + jnp.maximum(dots[:, h * tq:(h + 1) * tq], 0.0) * wis[h:h + 1, :]
        if masked:
            acc = jnp.where(key_index(off) < lim_row, acc, -jnp.inf)
        bits = lax.bitcast_convert_type(acc, I32)
        key = bits ^ (lax.shift_right_arithmetic(bits, 31) & 0x7FFFFFFF)
        key_scr[pl.ds(off, kb), :] = key
        hi_scr[pl.ds(off, kb), :] = lax.shift_right_arithmetic(key, HALF_BITS).astype(I16)
        lo_scr[pl.ds(off, kb), :] = ((key & ((1 << HALF_BITS) - 1)) + I16_MIN).astype(I16)
        return carry

    lim_min = jnp.minimum((lax.shift_right_logical(q0, CHUNK_BITS) + 1) * CHUNK, n_keys)
    n_open = lax.shift_right_logical(lim_min, int(math.log2(kb)))
    lax.fori_loop(0, n_open, functools.partial(score_block, masked=False), 0)
    lax.fori_loop(n_open, nkb, functools.partial(score_block, masked=True), 0)

    @pl.when(nkb % 2 == 1)
    def _():
        key_scr[pl.ds(key_block(nkb), kb), :] = jnp.full((kb, tq), INT_MIN, I32)
        hi_scr[pl.ds(key_block(nkb), kb), :] = jnp.full((kb, tq), I16_MIN, I16)
        lo_scr[pl.ds(key_block(nkb), kb), :] = jnp.full((kb, tq), I16_MIN, I16)

    n_steps = lax.shift_right_logical(nkb + 1, 1)

    def step_rows(i):
        return pl.ds(pl.multiple_of(i * 2 * kb, 2 * kb), 2 * kb)

    def count(pred):
        def body(i, part):
            hit = pred(key_scr[step_rows(i), :])
            return part + jnp.sum(jnp.where(hit, 1, 0).reshape(-1, COUNT_ROWS, tq), axis=0)
        part = lax.fori_loop(0, n_steps, body, jnp.zeros((COUNT_ROWS, tq), I32))
        return jnp.sum(part, axis=0, keepdims=True)

    def count16(ref, pred):
        def body(i, parts):
            ind = jnp.where(pred(ref[pl.ds(key_block(i), kb), :]), jnp.int16(1), jnp.int16(0))
            parts = list(parts)
            for j, r in enumerate(range(0, kb, COUNT16_ROWS)):
                parts[j % len(parts)] = parts[j % len(parts)] + ind[r:r + COUNT16_ROWS]
            return tuple(parts)
        parts = lax.fori_loop(0, 2 * n_steps, body, (jnp.zeros((COUNT16_ROWS, tq), I16),) * COUNT16_CHAINS)
        return jnp.sum(sum(p.astype(I32) for p in parts), axis=0, keepdims=True)

    def as_i16(row):
        return jnp.broadcast_to(row, (COUNT16_ROWS, tq)).astype(I16)[:1]

    def bisect16(ref, want):
        def body(i, thr):
            cand = thr + lax.shift_left(jnp.int32(1), HALF_BITS - 1 - i)
            cand16 = as_i16(cand)
            return jnp.where(count16(ref, lambda v: v >= cand16) >= want, cand, thr)
        return lax.fori_loop(0, HALF_BITS, body, jnp.full((1, tq), I16_MIN, I32))

    thr_hi = bisect16(hi_scr, n_sel)
    thr_hi16 = as_i16(thr_hi)
    n_hi_gt = count16(hi_scr, lambda v: v > thr_hi16)

    def keep_bucket(i, carry):
        rows = step_rows(i)
        lo_scr[rows, :] = jnp.where(hi_scr[rows, :] == thr_hi16, lo_scr[rows, :], jnp.int16(I16_MIN))
        return carry

    lax.fori_loop(0, n_steps, keep_bucket, 0)
    thr_lo = bisect16(lo_scr, n_sel - n_hi_gt)
    thr = thr_hi * (1 << HALF_BITS) + (thr_lo - I16_MIN)
    thr_lo16 = as_i16(thr_lo)
    n_gt = n_hi_gt + count16(lo_scr, lambda v: v > thr_lo16)
    n_eq = count(lambda key: key == thr)
    need = n_sel - n_gt
    tie = jnp.max(jnp.where((n_eq > need) & (thr > NEG_INF_KEY), 1, 0)) > 0

    @pl.when(jnp.logical_not(tie))
    def _():
        def body(kbi, carry):
            off = key_block(kbi)
            sel = (key_scr[pl.ds(off, kb), :] >= thr) & (key_index(off) < lim_row)
            bias_scr[pl.ds(off, kb), :] = jnp.where(sel, 0.0, -MASK_BIAS)
            return carry
        lax.fori_loop(0, nkb, body, 0)

    @pl.when(tie)
    def _():
        tri = jnp.where(lax.broadcasted_iota(I32, (kb, kb), 0) > lax.broadcasted_iota(I32, (kb, kb), 1),
                        1.0, 0.0).astype(BF16)
        need_f = need.astype(F32)

        def body(kbi, seen):
            off = key_block(kbi)
            key = key_scr[pl.ds(off, kb), :]
            eq = jnp.where(key == thr, 1.0, 0.0)
            earlier = _dot(tri, eq.astype(BF16)) + seen
            sel = ((key > thr) | ((key == thr) & (earlier < need_f))) & (key_index(off) < lim_row)
            bias_scr[pl.ds(off, kb), :] = jnp.where(sel, 0.0, -MASK_BIAS)
            return seen + jnp.sum(eq, axis=0, keepdims=True)
        lax.fori_loop(0, nkb, body, jnp.zeros((1, tq), F32))

    unit_heads = s_scr.shape[2] // tq
    n_units = n_heads // unit_heads
    kv_lanes = lambda u: slice((u * unit_heads // group) * LANES, (u * unit_heads // group + 1) * LANES)
    m_scr[...] = jnp.full(m_scr.shape, -jnp.inf, F32)
    acc_scr[...] = jnp.zeros(acc_scr.shape, F32)

    def scores(kbi, u, last):
        off = key_block(kbi)
        bias = jnp.concatenate([bias_scr[pl.ds(off, kb), :]] * unit_heads, axis=1)
        q_t = jnp.concatenate([qt_ref[u * unit_heads + j] for j in range(unit_heads)], axis=1)
        s = _dot(ka_ref[0, pl.ds(off, kb), kv_lanes(u)], q_t) + bias
        if last:
            ahead = jnp.maximum(key_index(off) - qpos_row, 0).astype(F32)
            s = s - jnp.concatenate([(2.0 * slopes[u * unit_heads + j] * LOG2E) * ahead for j in range(unit_heads)], axis=1)
        s_scr[u] = s

    def absorb(kbi, u):
        s = s_scr[u]
        m_old = m_scr[u]
        m_new = jnp.maximum(m_old, jnp.max(s, axis=0, keepdims=True))
        p = jnp.exp2((s - m_new).astype(BF16))
        acc_scr[u] = acc_scr[u] * jnp.exp2(m_old - m_new) + _dot(vt_ref[0, kbi, kv_lanes(u), :], p)
        m_scr[u] = m_new

    def step(kbi, last):
        for u in range(n_units):
            absorb(kbi - 1, u)
            scores(kbi, u, last)

    def first_scores(last):
        for u in range(n_units):
            scores(0, u, last)

    def step_body(kbi, carry):
        step(kbi, False)
        return carry

    pl.when(nkb == 1)(lambda: first_scores(True))
    pl.when(nkb > 1)(lambda: first_scores(False))
    lax.fori_loop(1, nkb - 1, step_body, 0)
    pl.when(nkb > 1)(lambda: step(nkb - 1, True))
    for u in range(n_units):
        absorb(nkb - 1, u)

    y = jnp.zeros((tq, wo_ref.shape[1]), F32)
    for u in range(n_units):
        acc = acc_scr[u]
        o_t = acc[:HEAD_DIM] / acc[HEAD_DIM:HEAD_DIM + 1]
        o_u = jnp.concatenate([o_t[:, j * tq:(j + 1) * tq] for j in range(unit_heads)], axis=0).astype(BF16)
        rows = slice(u * unit_heads * HEAD_DIM, (u + 1) * unit_heads * HEAD_DIM)
        y = y + lax.dot_general(o_u, wo_ref[rows, :], (((0,), (0,)), ((), ())), preferred_element_type=F32)
    o_ref[0] = _layer_norm(alpha * x_ref[0] + y, g_ref[...], b_ref[...])


def _sparse_attn(x, qt, qi, wit, k, vt, kd, wo, g, b, *, tq, kb, pos0, n_keys, alpha):
    bsz, s, d = x.shape
    n_t = s // tq
    l_pad = k.shape[1]
    n_heads = qt.shape[0]
    n_kv = k.shape[2] // LANES
    group = n_heads // n_kv
    assert kb % tq == 0 and pos0 % tq == 0 and l_pad % kb == 0 and vt.shape == (bsz, l_pad // kb, k.shape[2], kb)
    slopes = _alibi_slopes(n_heads)
    qrow = lambda c: pl.BlockSpec((1, tq, c), lambda bi, t: (bi, t, 0))
    key_mode = pl.Buffered(2)
    keys = lambda c: pl.BlockSpec((1, l_pad, c), lambda bi, t: (bi, 0, 0), pipeline_mode=key_mode)
    const2 = lambda bi, t: (0, 0)
    sel_rows = -(-l_pad // (2 * kb)) * 2 * kb
    unit_heads = min(max(UNIT_LANES // tq, 1), group)
    assert group % unit_heads == 0
    n_units, unit_w = n_heads // unit_heads, unit_heads * tq
    kern = functools.partial(_sparse_kernel, tq=tq, kb=kb, pos0=pos0, n_keys=n_keys, n_sel=min(TOPK_MAX, n_keys // 4),
                             n_heads=n_heads, group=group, slopes=slopes, alpha=alpha)
    return pl.pallas_call(
        kern,
        grid=(bsz, n_t),
        in_specs=[qrow(d), pl.BlockSpec((n_heads, LANES, tq), lambda bi, t: (0, 0, bi * n_t + t)), qrow(qi.shape[2]),
                  pl.BlockSpec((IDX_HEADS, tq), lambda bi, t: (0, bi * n_t + t)),
                  keys(k.shape[2]),
                  pl.BlockSpec((1,) + vt.shape[1:], lambda bi, t: (bi, 0, 0, 0), pipeline_mode=key_mode),
                  keys(kd.shape[2]),
                  _resident(wo.shape), pl.BlockSpec((1, d), const2), pl.BlockSpec((1, d), const2)],
        out_specs=qrow(d),
        out_shape=jax.ShapeDtypeStruct((bsz, s, d), F32),
        scratch_shapes=[pltpu.VMEM((sel_rows, tq), I32), pltpu.VMEM((sel_rows, tq), I16), pltpu.VMEM((sel_rows, tq), I16),
                        pltpu.VMEM((l_pad, tq), F32),
                        pltpu.VMEM((n_units, kb, unit_w), F32), pltpu.VMEM((n_units, LANES, unit_w), F32),
                        pltpu.VMEM((n_units, 1, unit_w), F32), pltpu.VMEM((n_heads * HEAD_DIM, tq), BF16)],
        compiler_params=_params(2),
        name="sparse_attn",
    )(x, qt, qi, wit, k, vt, kd, wo, g, b)


def _pad_rows(a, n):
    return jnp.pad(a, ((0, 0), (0, n - a.shape[1]), (0, 0)))


def _mixer_a(xp, xs, cache_k, cache_v, w_in, rel_bias, w_out, g, b, alpha):
    bsz, s, d = xp.shape
    dbs, t, _ = xs.shape
    width = w_out.shape[0]
    w_in = w_in.astype(BF16)
    w_out = w_out.astype(BF16)
    bias_p, bias_s = _rel_bias_tables(rel_bias, t)
    keep = min(A_PAST, s)
    q, k, vt, kf, vf = _proj_a(xp, w_in, keep)
    yp = _attn_a_prompt(xp, q, k, vt, bias_p, w_out, g, b, alpha)
    qs, ks, _, ksf, vsf = _proj_a(xs.reshape(1, dbs * t, d), w_in, dbs * t)
    n_cache = cache_k.shape[1]
    kwin = _pad_rows(jnp.concatenate([cache_k.reshape(dbs, n_cache, width).astype(BF16), ks.reshape(dbs, t, width)], 1), A_WIN)
    vwin = _pad_rows(jnp.concatenate([cache_v.reshape(dbs, n_cache, width), vsf.reshape(dbs, t, width)], 1).astype(BF16), A_WIN)
    ys = _attn_a_sample(_pad_rows(xs, A_SUB), _pad_rows(qs.reshape(dbs, t, width), A_SUB), kwin, vwin.transpose(0, 2, 1),
                        bias_s, w_out, g, b, alpha)[:, :t]
    heads = width // HEAD_DIM
    return (yp, ys, kf.reshape(bsz, keep, heads, HEAD_DIM), vf.reshape(bsz, keep, heads, HEAD_DIM),
            ksf.reshape(dbs, t, heads, HEAD_DIM), vsf.reshape(dbs, t, heads, HEAD_DIM))


def _mixer_b(xp, xs, cache_k, cache_v, cache_idx, w_in, w_out, g, b, alpha):
    bsz, s, d = xp.shape
    dbs, t, _ = xs.shape
    nq = w_out.shape[0]
    nkv = cache_k.shape[2] * cache_k.shape[3]
    nqi = IDX_HEADS * HEAD_DIM
    kvh = nkv // HEAD_DIM
    past = cache_k.shape[1]
    w_pad = jnp.pad(w_in, ((0, 0), (0, nq + 2 * nkv + nqi + LANES - w_in.shape[1]))).astype(BF16)
    w_out = w_out.astype(BF16)
    kb = PROJ_ROWS

    qt, qi, k, kf, vf, kif, kd, wit, vt = _proj_b(xp.reshape(bsz * s, d), w_pad, nq, nkv, nqi, kb, 0, s)
    r3 = lambda a: a.reshape(bsz, s, a.shape[1])
    yp = _sparse_attn(xp, qt, r3(qi), wit, r3(k), vt.reshape(bsz, s // kb, 2 * nkv, kb), r3(kd), w_out, g, b,
                      tq=B_QUERY_TILE, kb=kb, pos0=0, n_keys=s, alpha=alpha)

    qts, qis, ks, ksf, vsf, kisf, kds, wits, _ = _proj_b(xs.reshape(dbs * t, d), w_pad, nq, nkv, nqi, dbs * t, past, t)
    tq_s = B_SAMPLE_TILE
    n_keys = past + t
    l_pad = -(-n_keys // kb) * kb
    s3 = lambda a: a.reshape(dbs, t, a.shape[1])
    v_all = jnp.concatenate([cache_v, vsf.reshape(dbs, t, kvh, HEAD_DIM)], axis=1).astype(BF16)
    v_all = jnp.concatenate([v_all, jnp.ones_like(v_all)], axis=-1).reshape(dbs, n_keys, 2 * nkv)
    keys_vt = _pad_rows(v_all, l_pad).reshape(dbs, l_pad // kb, kb, 2 * nkv).transpose(0, 1, 3, 2)
    kd_cache = jnp.concatenate([cache_idx, cache_idx], axis=-1).astype(BF16)
    frame = jnp.arange(past, dtype=I32)[:, None]
    pos = _pos_lanes(frame, HEAD_DIM + jnp.arange(HEAD_DIM, dtype=I32)[None, :]).astype(BF16)
    pos = jnp.broadcast_to(pos[None, :, None, :], (dbs, past, kvh, HEAD_DIM))
    k_cache = jnp.concatenate([cache_k.astype(BF16), pos], axis=-1).reshape(dbs, past, 2 * nkv)
    keys_k = _pad_rows(jnp.concatenate([k_cache, s3(ks)], 1), l_pad)
    keys_kd = _pad_rows(jnp.concatenate([kd_cache, s3(kds)], 1), l_pad)
    wit_pad = jnp.pad(wits.reshape(IDX_HEADS, dbs, t), ((0, 0), (0, 0), (0, tq_s - t)), mode="edge").reshape(IDX_HEADS, dbs * tq_s)
    qis_pad = jnp.pad(s3(qis), ((0, 0), (0, tq_s - t), (0, 0)), mode="edge")
    qts_pad = jnp.pad(qts.reshape(-1, LANES, dbs, t), ((0, 0), (0, 0), (0, 0), (0, tq_s - t))).reshape(-1, LANES, dbs * tq_s)
    ys = _sparse_attn(_pad_rows(xs, tq_s), qts_pad, qis_pad, wit_pad,
                      keys_k, keys_vt, keys_kd, w_out, g, b,
                      tq=tq_s, kb=kb, pos0=past, n_keys=n_keys, alpha=alpha)[:, :t]
    return (yp, ys,
            kf.reshape(bsz, s, kvh, HEAD_DIM), vf.reshape(bsz, s, kvh, HEAD_DIM), kif.reshape(bsz, s, HEAD_DIM),
            ksf.reshape(dbs, t, kvh, HEAD_DIM), vsf.reshape(dbs, t, kvh, HEAD_DIM), kisf.reshape(dbs, t, HEAD_DIM))


def kernel(x_prompt, x_sample, cache_a_k, cache_a_v, cache_b_k, cache_b_v, cache_b_idx, ln_g, ln_b, ffn_w_gate, ffn_w_up, ffn_w_down, a_w_in, a_rel_bias, a_w_out, b_w_in, b_w_out):
    depth = ln_g.shape[0]
    alpha = (2.0 * depth) ** 0.25
    bsz, s, d = x_prompt.shape
    dbs, t, _ = x_sample.shape
    xp, xs = x_prompt, x_sample
    a_out, b_out = [], []

    wg_all, wu_all, wd_all = ffn_w_gate.astype(BF16), ffn_w_up.astype(BF16), ffn_w_down.astype(BF16)

    def ffn(x, layer, i, tm):
        shp = x.shape
        y = _ffn_block(x.reshape(-1, d), wg_all[layer, i], wu_all[layer, i], wd_all[layer, i],
                       ln_g[layer, 2 * i][None], ln_b[layer, 2 * i][None], alpha, tm)
        return y.reshape(shp)

    for layer in range(depth):
        j = layer // 2
        xp = ffn(xp, layer, 0, FFN_ROWS)
        xs = ffn(xs, layer, 0, dbs * t)
        g, b = ln_g[layer, 1][None], ln_b[layer, 1][None]
        if layer % 2 == 0:
            xp, xs, *rest = _mixer_a(xp, xs, cache_a_k[j], cache_a_v[j], a_w_in[j], a_rel_bias[j], a_w_out[j], g, b, alpha)
            a_out.append(rest)
        else:
            xp, xs, *rest = _mixer_b(xp, xs, cache_b_k[j], cache_b_v[j], cache_b_idx[j], b_w_in[j], b_w_out[j], g, b, alpha)
            b_out.append(rest)
        xp = ffn(xp, layer, 1, FFN_ROWS)
        xs = ffn(xs, layer, 1, dbs * t)

    stack = lambda outs, i: jnp.stack([o[i] for o in outs], 0)
    return (xp, xs,
            stack(a_out, 0), stack(a_out, 1), stack(a_out, 2), stack(a_out, 3),
            stack(b_out, 0), stack(b_out, 1), stack(b_out, 2), stack(b_out, 3), stack(b_out, 4), stack(b_out, 5))
```
